```python
import jax, jax.numpy as jnp
from jax import lax
import numpy as np

D_MODEL = 1024
BATCH = 4
SEQ = 8192
DEPTH = 1
DEC_BATCH = 2
DEC_SEQ = 8192
PAST_LEN = 128

N_Q_HEADS = 16
N_KV_HEADS = 2
HEAD_DIM = 64
GROUP = N_Q_HEADS // N_KV_HEADS
ROT_DIM = HEAD_DIM // 4
ROPE_THETA = 500000.0
WINDOW = 128
ATTN_BLOCK = 128
ATTN_SCALE = HEAD_DIM ** -0.5
CONV_WIDTH = D_MODEL
CONV_KERNEL = 31
CONV_PAD = CONV_KERNEL // 2
N_BRANCHES = 2
N_EXPERTS = 32
TOP_K = 4
D_FF = D_MODEL
SWIGLU_LIMIT = 7.0
SWIGLU_ALPHA = 1.702
MOE_BLOCK = 512
RMS_EPS = 1e-5
LN_EPS = 1e-5

Q_COLS = N_Q_HEADS * HEAD_DIM
KV_COLS = N_KV_HEADS * HEAD_DIM
SPLITS = (2 * CONV_WIDTH, 2 * CONV_WIDTH + Q_COLS, 2 * CONV_WIDTH + Q_COLS + KV_COLS,
          2 * CONV_WIDTH + Q_COLS + 2 * KV_COLS)
IN_COLS = 2 * CONV_WIDTH + Q_COLS + 2 * KV_COLS + N_BRANCHES * D_MODEL

kernel_name = "hybrid_conv_swa_moe_encoder"


def rmsnorm(x, g):
    xf = x.astype(jnp.float32)
    y = xf * lax.rsqrt(jnp.mean(xf * xf, axis=-1, keepdims=True) + RMS_EPS)
    return (y * g.astype(jnp.float32)).astype(x.dtype)


def layernorm(x, g, b):
    xf = x.astype(jnp.float32)
    mu = jnp.mean(xf, axis=-1, keepdims=True)
    var = jnp.mean(jnp.square(xf - mu), axis=-1, keepdims=True)
    y = (xf - mu) * lax.rsqrt(var + LN_EPS) * g.astype(jnp.float32) + b.astype(jnp.float32)
    return y.astype(x.dtype)


def partial_rotary(x, pos):
    half = ROT_DIM // 2
    inv_freq = ROPE_THETA ** (-jnp.arange(half, dtype=jnp.float32) * 2.0 / ROT_DIM)
    ang = pos.astype(jnp.float32)[:, None] * inv_freq[None, :]
    cos = jnp.cos(ang)[None, :, None, :]
    sin = jnp.sin(ang)[None, :, None, :]
    xr = x[..., :ROT_DIM].astype(jnp.float32)
    x1, x2 = xr[..., :half], xr[..., half:]
    rot = jnp.concatenate([x1 * cos - x2 * sin, x2 * cos + x1 * sin], axis=-1)
    return jnp.concatenate([rot.astype(x.dtype), x[..., ROT_DIM:]], axis=-1)


def window_attention(q, k, v, sinks):
    B, S = q.shape[0], q.shape[1]
    nb = S // ATTN_BLOCK
    qb = q.reshape(B, nb, ATTN_BLOCK, N_KV_HEADS, GROUP, HEAD_DIM).swapaxes(0, 1)

    def band(t):
        tp = jnp.pad(t, ((0, 0), (ATTN_BLOCK, ATTN_BLOCK), (0, 0), (0, 0)))
        tp = tp.reshape(B, nb + 2, ATTN_BLOCK, N_KV_HEADS, HEAD_DIM)
        return jnp.concatenate([tp[:, :-2], tp[:, 1:-1], tp[:, 2:]], axis=2).swapaxes(0, 1)

    kb, vb = band(k), band(v)
    sink = sinks.astype(jnp.float32).reshape(N_KV_HEADS, GROUP)[None, :, :, None, None]
    qi = jnp.arange(ATTN_BLOCK)[:, None]
    kj = jnp.arange(3 * ATTN_BLOCK)[None, :]

    def attend(args):
        q_n, k_n, v_n, n = args
        s = jnp.einsum('bqhgd,bkhd->bhgqk', q_n, k_n).astype(jnp.float32) * ATTN_SCALE
        qpos = n * ATTN_BLOCK + qi
        kpos = (n - 1) * ATTN_BLOCK + kj
        valid = (jnp.abs(qpos - kpos) <= WINDOW) & (kpos >= 0) & (kpos < S)
        s = jnp.where(valid, s, -jnp.inf)
        m = jnp.maximum(jnp.max(s, axis=-1, keepdims=True), sink)
        p = jnp.exp(s - m)
        probs = p / (jnp.sum(p, axis=-1, keepdims=True) + jnp.exp(sink - m))
        return jnp.einsum('bhgqk,bkhd->bqhgd', probs.astype(v_n.dtype), v_n)

    out = lax.map(attend, (qb, kb, vb, jnp.arange(nb)))
    return out.swapaxes(0, 1).reshape(B, S, N_Q_HEADS * HEAD_DIM)


def conv_module(u_in, w_dw, b_dw, ln_g, ln_b, w_pw, b_pw):
    a, gate = jnp.split(u_in, 2, axis=-1)
    u = a * jax.nn.sigmoid(gate)
    y = lax.conv_general_dilated(u, w_dw.astype(u.dtype), window_strides=(1,),
                                 padding=((CONV_PAD, CONV_PAD),),
                                 dimension_numbers=('NWC', 'WIO', 'NWC'),
                                 feature_group_count=CONV_WIDTH) + b_dw
    y = jax.nn.silu(layernorm(y, ln_g, ln_b))
    return y @ w_pw + b_pw


def mixer(xn, w_in, b_in, sinks, w_dw, b_dw, ln_g, ln_b, w_pw, b_pw, w_out):
    B, S, _ = xn.shape
    proj = xn @ w_in + b_in
    conv_in, q, k, v, gates = jnp.split(proj, SPLITS, axis=-1)
    pos = jnp.arange(S)
    q = partial_rotary(q.reshape(B, S, N_Q_HEADS, HEAD_DIM), pos)
    k = partial_rotary(k.reshape(B, S, N_KV_HEADS, HEAD_DIM), pos)
    v = v.reshape(B, S, N_KV_HEADS, HEAD_DIM)
    attn = window_attention(q, k, v, sinks)
    conv = conv_module(conv_in, w_dw, b_dw, ln_g, ln_b, w_pw, b_pw)
    g_conv, g_attn = jnp.split(jax.nn.sigmoid(gates), N_BRANCHES, axis=-1)
    return (g_conv * conv + g_attn * attn) @ w_out


def moe(x, w_router, b_router, w_gate_up, b_gate_up, w_down, b_down):
    B, S, D = x.shape
    N = B * S
    h = x.reshape(N, D)
    logits = (h @ w_router).astype(jnp.float32) + b_router.astype(jnp.float32)
    top_logits, top_idx = lax.top_k(logits, TOP_K)
    top_w = jax.nn.softmax(top_logits, axis=-1)
    A = N * TOP_K
    flat_e = top_idx.reshape(A)
    flat_w = top_w.reshape(A)
    flat_t = jnp.arange(A, dtype=jnp.int32) // TOP_K
    order = jnp.argsort(flat_e)
    se, st, sw = flat_e[order], flat_t[order], flat_w[order]
    counts = jnp.bincount(flat_e, length=N_EXPERTS)
    padded = (counts + MOE_BLOCK - 1) // MOE_BLOCK * MOE_BLOCK
    start = jnp.cumsum(counts) - counts
    pend = jnp.cumsum(padded)
    pstart = pend - padded
    dest = pstart[se] + jnp.arange(A) - start[se]
    n_blocks = -(-A // MOE_BLOCK) + N_EXPERTS
    P = n_blocks * MOE_BLOCK
    row_tok = jnp.full((P,), N, dtype=jnp.int32).at[dest].set(st)
    row_w = jnp.zeros((P,), jnp.float32).at[dest].set(sw)
    block_e = jnp.minimum(jnp.searchsorted(pend, jnp.arange(n_blocks) * MOE_BLOCK, side='right'),
                          N_EXPERTS - 1)
    h_pad = jnp.concatenate([h, jnp.zeros((1, D), h.dtype)], axis=0)
    xs = h_pad[row_tok].reshape(n_blocks, MOE_BLOCK, D)

    def expert_block(args):
        xb, e = args
        gu = xb @ w_gate_up[e] + b_gate_up[e]
        gate, up = jnp.split(gu, 2, axis=-1)
        gate = jnp.minimum(gate, SWIGLU_LIMIT)
        up = jnp.clip(up, -SWIGLU_LIMIT, SWIGLU_LIMIT)
        glu = gate * jax.nn.sigmoid(gate * SWIGLU_ALPHA)
        return ((up + 1) * glu) @ w_down[e] + b_down[e]

    ys = lax.map(expert_block, (xs, block_e)).reshape(P, D)
    ys = ys * row_w[:, None].astype(ys.dtype)
    out = jax.ops.segment_sum(ys, row_tok, num_segments=N + 1)[:N]
    return out.reshape(B, S, D)


def trunk(x, g_mix, w_in, b_in, sinks, w_dw, b_dw, ln_g, ln_b, w_pw, b_pw, w_out,
          g_ffn, w_router, b_router, w_gate_up, b_gate_up, w_down, b_down, g_final):
    for l in range(DEPTH):
        x = x + mixer(rmsnorm(x, g_mix[l]), w_in[l], b_in[l], sinks[l], w_dw[l], b_dw[l],
                      ln_g[l], ln_b[l], w_pw[l], b_pw[l], w_out[l])
        x = x + moe(rmsnorm(x, g_ffn[l]), w_router[l], b_router[l], w_gate_up[l],
                    b_gate_up[l], w_down[l], b_down[l])
    return rmsnorm(x, g_final)


def setup_inputs(seed: int = 0) -> dict:
    key = jax.random.key(seed)
    ks = jax.random.split(key, 24)
    f32 = jnp.float32
    nrm = lambda k, shape, scale: jax.random.normal(k, shape, f32) * scale
    L = DEPTH
    return {
        "x_prompt": nrm(ks[0], (BATCH, SEQ, D_MODEL), 1.0),
        "x_sample": nrm(ks[1], (DEC_BATCH, DEC_SEQ, D_MODEL), 1.0),
        "g_mix": 1.0 + nrm(ks[2], (L, D_MODEL), 0.02),
        "w_in": nrm(ks[3], (L, D_MODEL, IN_COLS), D_MODEL ** -0.5),
        "b_in": nrm(ks[4], (L, IN_COLS), 0.02),
        "sinks": nrm(ks[5], (L, N_Q_HEADS), 0.5),
        "w_dw": nrm(ks[6], (L, CONV_KERNEL, 1, CONV_WIDTH), CONV_KERNEL ** -0.5),
        "b_dw": nrm(ks[7], (L, CONV_WIDTH), 0.02),
        "ln_g": 1.0 + nrm(ks[8], (L, CONV_WIDTH), 0.02),
        "ln_b": nrm(ks[9], (L, CONV_WIDTH), 0.02),
        "w_pw": nrm(ks[10], (L, CONV_WIDTH, D_MODEL), CONV_WIDTH ** -0.5),
        "b_pw": nrm(ks[11], (L, D_MODEL), 0.02),
        "w_out": nrm(ks[12], (L, D_MODEL, D_MODEL), D_MODEL ** -0.5),
        "g_ffn": 1.0 + nrm(ks[13], (L, D_MODEL), 0.02),
        "w_router": nrm(ks[14], (L, D_MODEL, N_EXPERTS), D_MODEL ** -0.5),
        "b_router": nrm(ks[15], (L, N_EXPERTS), 0.01),
        "w_gate_up": nrm(ks[16], (L, N_EXPERTS, D_MODEL, 2 * D_FF), D_MODEL ** -0.5),
        "b_gate_up": nrm(ks[17], (L, N_EXPERTS, 2 * D_FF), 0.01),
        "w_down": nrm(ks[18], (L, N_EXPERTS, D_FF, D_MODEL), D_FF ** -0.5),
        "b_down": nrm(ks[19], (L, N_EXPERTS, D_MODEL), 0.01),
        "g_final": 1.0 + nrm(ks[20], (D_MODEL,), 0.02),
    }


def reference(x_prompt, x_sample, g_mix, w_in, b_in, sinks, w_dw, b_dw, ln_g, ln_b, w_pw, b_pw,
              w_out, g_ffn, w_router, b_router, w_gate_up, b_gate_up, w_down, b_down, g_final):
    y_prompt = trunk(x_prompt, g_mix, w_in, b_in, sinks, w_dw, b_dw, ln_g, ln_b, w_pw, b_pw, w_out,
                     g_ffn, w_router, b_router, w_gate_up, b_gate_up, w_down, b_down, g_final)
    y_sample = trunk(x_sample, g_mix, w_in, b_in, sinks, w_dw, b_dw, ln_g, ln_b, w_pw, b_pw, w_out,
                     g_ffn, w_router, b_router, w_gate_up, b_gate_up, w_down, b_down, g_final)
    return (y_prompt, y_sample)
```

```python
import functools

import jax
import jax.numpy as jnp
from jax import lax
from jax.experimental import pallas as pl
from jax.experimental.pallas import tpu as pltpu

D_MODEL = 1024
N_Q_HEADS = 16
N_KV_HEADS = 2
HEAD_DIM = 64
GROUP = N_Q_HEADS // N_KV_HEADS
ROT_DIM = HEAD_DIM // 4
ROPE_THETA = 500000.0
WINDOW = 128
ATTN_SCALE = HEAD_DIM ** -0.5
CONV_KERNEL = 31
CONV_PAD = CONV_KERNEL // 2
N_EXPERTS = 32
TOP_K = 4
D_FF = D_MODEL
SWIGLU_LIMIT = 7.0
SWIGLU_ALPHA = 1.702
MOE_BLOCK = 512
RMS_EPS = 1e-5
LN_EPS = 1e-5

KV_COLS = N_KV_HEADS * HEAD_DIM
LANES = 128
SUBLANES = 8
HALF = D_MODEL // 2
HALO = 16
NEG_BIG = -1e30

VMEM_LIMIT = 56 * 1024 * 1024

F32 = jnp.float32
BF16 = jnp.bfloat16


def _cparams(*sem):
    return pltpu.CompilerParams(dimension_semantics=sem, vmem_limit_bytes=VMEM_LIMIT)


def _pack_rows(x):
    lo = lax.bitcast_convert_type(x[:, :HALF].astype(BF16).astype(F32), jnp.uint32)
    hi = lax.bitcast_convert_type(x[:, HALF:].astype(BF16).astype(F32), jnp.uint32)
    return (lo >> 16) | (hi & jnp.uint32(0xFFFF0000))


def _unpack_rows(p):
    lo = lax.bitcast_convert_type(p << 16, F32)
    hi = lax.bitcast_convert_type(p & jnp.uint32(0xFFFF0000), F32)
    return lo, hi


def _inproj_body(x_ref, g_ref, w_ref, b_ref, cos_ref, s1_ref, s2_ref,
                 u_ref, q_ref, k_ref, v_ref, gc_ref, ga_ref):
    x = x_ref[...]
    xn = x * lax.rsqrt(jnp.mean(x * x, axis=-1, keepdims=True) + RMS_EPS) * g_ref[...]
    xb = xn.astype(BF16)

    def proj(c0, c1):
        return jnp.dot(xb, w_ref[:, c0:c1], preferred_element_type=F32) + b_ref[:, c0:c1]

    cos = cos_ref[...]
    s1 = s1_ref[...]
    s2 = s2_ref[...]

    def rotary(t):
        half = ROT_DIM // 2
        outs = []
        for c in range(t.shape[1] // LANES):
            tc = t[:, c * LANES:(c + 1) * LANES]
            outs.append(tc * cos + pltpu.roll(tc, half, 1) * s1 + pltpu.roll(tc, LANES - half, 1) * s2)
        return outs[0] if len(outs) == 1 else jnp.concatenate(outs, axis=1)

    d = D_MODEL
    a = proj(0, d)
    gate = proj(d, 2 * d)
    u_ref[...] = (a * jax.nn.sigmoid(gate)).astype(BF16)
    q = proj(2 * d, 3 * d)
    q_ref[...] = (rotary(q) * ATTN_SCALE).astype(BF16)
    kv = proj(3 * d, 3 * d + 2 * KV_COLS)
    k_ref[...] = rotary(kv[:, :KV_COLS]).astype(BF16)
    v_ref[...] = kv[:, KV_COLS:].astype(BF16)
    c0 = 3 * d + 2 * KV_COLS
    gc_ref[...] = jax.nn.sigmoid(proj(c0, c0 + d)).astype(BF16)
    ga_ref[...] = jax.nn.sigmoid(proj(c0 + d, c0 + 2 * d)).astype(BF16)


def _inproj(x, g_mix, w_in, b_in, cos_t, s1_t, s2_t, seq, tm):
    n = x.shape[0]
    in_cols = w_in.shape[1]
    spt = seq // tm
    row = lambda w: pl.BlockSpec((tm, w), lambda i: (i, 0))
    const = lambda r, c: pl.BlockSpec((r, c), lambda i: (0, 0))
    tab = pl.BlockSpec((tm, LANES), lambda i: (i % spt, 0))
    big = jax.ShapeDtypeStruct((n, D_MODEL), BF16)
    small = jax.ShapeDtypeStruct((n, KV_COLS), BF16)
    return pl.pallas_call(
        _inproj_body,
        grid=(n // tm,),
        in_specs=[row(D_MODEL), const(1, D_MODEL), const(D_MODEL, in_cols), const(1, in_cols), tab, tab, tab],
        out_specs=[row(D_MODEL), row(D_MODEL), row(KV_COLS), row(KV_COLS), row(D_MODEL), row(D_MODEL)],
        out_shape=[big, big, small, small, big, big],
        compiler_params=_cparams("parallel"),
        name="inproj",
    )(x, g_mix, w_in, b_in, cos_t, s1_t, s2_t)


def _attn_body(q_ref, kp_ref, kc_ref, kn_ref, vp_ref, vc_ref, vn_ref, sink_ref, o_ref, kbuf, vbuf,
               *, tq, nblk):
    i = pl.program_id(1)
    blk = WINDOW
    kbuf[0:blk, :] = kp_ref[0]
    kbuf[blk:blk + tq, :] = kc_ref[0]
    kbuf[blk + tq:, :] = kn_ref[0]
    vbuf[0:blk, :] = vp_ref[0]
    vbuf[blk:blk + tq, :] = vc_ref[0]
    vbuf[blk + tq:, :] = vn_ref[0]

    qi = lax.broadcasted_iota(jnp.int32, (blk, 3 * blk), 0)
    kj = lax.broadcasted_iota(jnp.int32, (blk, 3 * blk), 1)
    band = jnp.abs(qi + blk - kj) <= WINDOW
    for j in range(tq // blk):
        n = i * (tq // blk) + j
        valid = band & ((kj >= blk) | (n > 0)) & ((kj < 2 * blk) | (n < nblk - 1))
        keys = kbuf[j * blk:(j + 3) * blk, :]
        vals = vbuf[j * blk:(j + 3) * blk, :]
        outs = []
        for h in range(N_Q_HEADS):
            g = h // GROUP
            qh = q_ref[0, j * blk:(j + 1) * blk, h * HEAD_DIM:(h + 1) * HEAD_DIM]
            kh = keys[:, g * HEAD_DIM:(g + 1) * HEAD_DIM]
            vh = vals[:, g * HEAD_DIM:(g + 1) * HEAD_DIM]
            s = lax.dot_general(qh, kh, (((1,), (1,)), ((), ())), preferred_element_type=F32)
            s = jnp.where(valid, s, NEG_BIG)
            sink = sink_ref[h]
            m = jnp.maximum(jnp.max(s, axis=-1, keepdims=True), sink)
            p = jnp.exp(s - m)
            denom = jnp.sum(p, axis=-1, keepdims=True) + jnp.exp(sink - m)
            o = jnp.dot(p.astype(BF16), vh, preferred_element_type=F32)
            outs.append(o / denom)
        o_ref[0, j * blk:(j + 1) * blk, :] = jnp.concatenate(outs, axis=1).astype(BF16)


def _attn(q, k, v, sinks, tq):
    b, s, _ = q.shape
    r = tq // WINDOW
    nblk = s // WINDOW
    cur = lambda w: pl.BlockSpec((1, tq, w), lambda bi, i: (bi, i, 0))
    prev = pl.BlockSpec((1, WINDOW, KV_COLS), lambda bi, i: (bi, jnp.maximum(i * r - 1, 0), 0))
    nxt = pl.BlockSpec((1, WINDOW, KV_COLS), lambda bi, i: (bi, jnp.minimum((i + 1) * r, nblk - 1), 0))
    return pl.pallas_call(
        functools.partial(_attn_body, tq=tq, nblk=nblk),
        grid=(b, s // tq),
        in_specs=[cur(D_MODEL), prev, cur(KV_COLS), nxt, prev, cur(KV_COLS), nxt,
                  pl.BlockSpec(memory_space=pltpu.SMEM)],
        out_specs=cur(D_MODEL),
        out_shape=jax.ShapeDtypeStruct((b, s, D_MODEL), BF16),
        scratch_shapes=[pltpu.VMEM((tq + 2 * WINDOW, KV_COLS), BF16),
                        pltpu.VMEM((tq + 2 * WINDOW, KV_COLS), BF16)],
        compiler_params=_cparams("parallel", "parallel"),
        name="attn",
    )(q, k, k, k, v, v, v, sinks)


CONV_ROWS = 32


def _mix_body(x_ref, up_ref, uc_ref, un_ref, at_ref, gc_ref, ga_ref,
              wdw_ref, bdw_ref, lng_ref, lnb_ref, wpw_ref, bpw_ref, wout_ref,
              gffn_ref, wr_ref, br_ref,
              x2_ref, hp_ref, idx_ref, wt_ref, rank_ref, cnt_ref,
              ubuf, sbuf, ybuf, run_ref, *, tm, spt):
    i = pl.program_id(0)
    first = (i % spt) == 0
    last = (i % spt) == spt - 1

    @pl.when(i == 0)
    def _():
        run_ref[...] = jnp.zeros_like(run_ref)

    ubuf[0:HALO, :] = jnp.where(first, 0.0, up_ref[...].astype(F32))
    ubuf[HALO:HALO + tm, :] = uc_ref[...].astype(F32)
    ubuf[HALO + tm:, :] = jnp.where(last, 0.0, un_ref[...].astype(F32))

    def shift_rows(r0, nrows):
        for c in range(D_MODEL // LANES):
            cs = slice(c * LANES, (c + 1) * LANES)
            win = ubuf[pl.ds(r0, nrows + SUBLANES), cs]
            for s in range(1, SUBLANES):
                sbuf[s - 1, pl.ds(r0, nrows), cs] = win[s:s + nrows]

    def shift_chunk(r, carry):
        shift_rows(pl.multiple_of(r * CONV_ROWS, CONV_ROWS), CONV_ROWS)
        return carry

    lax.fori_loop(0, tm // CONV_ROWS, shift_chunk, 0)
    shift_rows(tm, 2 * HALO - SUBLANES)

    base = HALO - CONV_PAD

    def conv_rows(r, carry):
        r0 = pl.multiple_of(r * CONV_ROWS, CONV_ROWS)
        for c in range(D_MODEL // LANES):
            cs = slice(c * LANES, (c + 1) * LANES)
            acc = jnp.zeros((CONV_ROWS, LANES), F32)
            for t in range(CONV_KERNEL):
                a, s = divmod(base + t, SUBLANES)
                rows = pl.ds(r0 + a * SUBLANES, CONV_ROWS)
                tap = ubuf[rows, cs] if s == 0 else sbuf[s - 1, rows, cs]
                acc = acc + tap * wdw_ref[t:t + 1, cs]
            ybuf[pl.ds(r0, CONV_ROWS), cs] = acc
        return carry

    lax.fori_loop(0, tm // CONV_ROWS, conv_rows, 0)

    y = ybuf[...] + bdw_ref[...]
    mu = jnp.mean(y, axis=-1, keepdims=True)
    yc = y - mu
    var = jnp.mean(yc * yc, axis=-1, keepdims=True)
    yn = yc * lax.rsqrt(var + LN_EPS) * lng_ref[...] + lnb_ref[...]
    act = yn * jax.nn.sigmoid(yn)
    conv = jnp.dot(act.astype(BF16), wpw_ref[...], preferred_element_type=F32) + bpw_ref[...]

    merged = gc_ref[...].astype(F32) * conv + ga_ref[...].astype(F32) * at_ref[...].astype(F32)
    x2 = x_ref[...] + jnp.dot(merged.astype(BF16), wout_ref[...], preferred_element_type=F32)
    x2_ref[...] = x2

    h = x2 * lax.rsqrt(jnp.mean(x2 * x2, axis=-1, keepdims=True) + RMS_EPS) * gffn_ref[...]
    hp_ref[...] = _pack_rows(h)

    logits = lax.dot_general(wr_ref[...], h, (((1,), (1,)), ((), ())),
                             preferred_element_type=F32, precision=lax.Precision.HIGHEST) + br_ref[...]
    eio = lax.broadcasted_iota(jnp.int32, (N_EXPERTS, tm), 0).astype(F32)
    work = logits
    sels, tops, idxs = [], [], []
    for _ in range(TOP_K):
        m = jnp.max(work, axis=0, keepdims=True)
        idx = jnp.min(jnp.where(work == m, eio, float(N_EXPERTS)), axis=0, keepdims=True)
        sel = eio == idx
        work = jnp.where(sel, -jnp.inf, work)
        sels.append(sel)
        tops.append(m)
        idxs.append(idx)
    es = [jnp.exp(t - tops[0]) for t in tops]
    tot = es[0] + es[1] + es[2] + es[3]
    wt_ref[...] = jnp.concatenate([e / tot for e in es], axis=0)
    idx_ref[...] = jnp.concatenate(idxs, axis=0).astype(jnp.int32)

    multihot = (sels[0] | sels[1] | sels[2] | sels[3])
    ti = lax.broadcasted_iota(jnp.int32, (tm, tm), 0)
    tj = lax.broadcasted_iota(jnp.int32, (tm, tm), 1)
    before = (ti < tj).astype(BF16)
    prefix = jnp.dot(multihot.astype(BF16), before, preferred_element_type=F32) + run_ref[...]
    rank_ref[...] = jnp.concatenate(
        [jnp.sum(jnp.where(s, prefix, 0.0), axis=0, keepdims=True) for s in sels], axis=0).astype(jnp.int32)
    run_ref[...] = run_ref[...] + jnp.sum(multihot.astype(F32), axis=1, keepdims=True)
    cnt_ref[...] = jnp.broadcast_to(run_ref[...], cnt_ref.shape)


def _mix(x, u, attn, gc, ga, w_dw, b_dw, ln_g, ln_b, w_pw, b_pw, w_out, g_ffn, w_r_t, b_r, seq, tm):
    n = x.shape[0]
    spt = seq // tm
    hb = tm // HALO
    nh = n // HALO
    row = lambda w: pl.BlockSpec((tm, w), lambda i: (i, 0))
    const = lambda r, c: pl.BlockSpec((r, c), lambda i: (0, 0))
    halo_prev = pl.BlockSpec((HALO, D_MODEL), lambda i: (jnp.maximum(i * hb - 1, 0), 0))
    halo_next = pl.BlockSpec((HALO, D_MODEL), lambda i: (jnp.minimum((i + 1) * hb, nh - 1), 0))
    tok = pl.BlockSpec((TOP_K, tm), lambda i: (0, i))
    return pl.pallas_call(
        functools.partial(_mix_body, tm=tm, spt=spt),
        grid=(n // tm,),
        in_specs=[row(D_MODEL), halo_prev, row(D_MODEL), halo_next, row(D_MODEL), row(D_MODEL), row(D_MODEL),
                  const(CONV_KERNEL, D_MODEL), const(1, D_MODEL), const(1, D_MODEL), const(1, D_MODEL),
                  const(D_MODEL, D_MODEL), const(1, D_MODEL), const(D_MODEL, D_MODEL),
                  const(1, D_MODEL), const(N_EXPERTS, D_MODEL), const(N_EXPERTS, 1)],
        out_specs=[row(D_MODEL), row(HALF), tok, tok, tok, const(N_EXPERTS, LANES)],
        out_shape=[jax.ShapeDtypeStruct((n, D_MODEL), F32),
                   jax.ShapeDtypeStruct((n, HALF), jnp.uint32),
                   jax.ShapeDtypeStruct((TOP_K, n), jnp.int32),
                   jax.ShapeDtypeStruct((TOP_K, n), F32),
                   jax.ShapeDtypeStruct((TOP_K, n), jnp.int32),
                   jax.ShapeDtypeStruct((N_EXPERTS, LANES), F32)],
        scratch_shapes=[pltpu.VMEM((tm + 2 * HALO, D_MODEL), F32),
                        pltpu.VMEM((SUBLANES - 1, tm + 2 * HALO - SUBLANES, D_MODEL), F32),
                        pltpu.VMEM((tm, D_MODEL), F32),
                        pltpu.VMEM((N_EXPERTS, 1), F32)],
        compiler_params=_cparams("arbitrary"),
        name="mix",
    )(x, u, u, u, attn, gc, ga, w_dw, b_dw, ln_g, ln_b, w_pw, b_pw, w_out, g_ffn, w_r_t, b_r)


def _row_copy(src_ref, src_row, dst_ref, dst_row, sem):
    return pltpu.make_async_copy(src_ref.at[pl.ds(src_row, 1)], dst_ref.at[pl.ds(dst_row, 1)], sem)


def _dispatch_body(dest_ref, hp_ref, xs_in_ref, xs_ref, sem, *, tm):
    del xs_in_ref

    def start(t, carry):
        for k in range(TOP_K):
            _row_copy(hp_ref, t, xs_ref, dest_ref[t * TOP_K + k], sem).start()
        return carry

    lax.fori_loop(0, tm, start, 0)

    def wait(t, carry):
        for k in range(TOP_K):
            _row_copy(hp_ref, 0, xs_ref, 0, sem).wait()
        return carry

    lax.fori_loop(0, tm, wait, 0)


def _dispatch(dest_flat, hp, xs_init, tm):
    n = hp.shape[0]
    return pl.pallas_call(
        functools.partial(_dispatch_body, tm=tm),
        grid=(n // tm,),
        in_specs=[pl.BlockSpec((tm * TOP_K,), lambda i: (i,), memory_space=pltpu.SMEM),
                  pl.BlockSpec((tm, HALF), lambda i: (i, 0)),
                  pl.BlockSpec(memory_space=pl.ANY)],
        out_specs=pl.BlockSpec(memory_space=pl.ANY),
        out_shape=jax.ShapeDtypeStruct(xs_init.shape, jnp.uint32),
        scratch_shapes=[pltpu.SemaphoreType.DMA(())],
        input_output_aliases={2: 0},
        compiler_params=pltpu.CompilerParams(dimension_semantics=("arbitrary",), has_side_effects=True,
                                             vmem_limit_bytes=VMEM_LIMIT),
        name="dispatch",
    )(dest_flat, hp, xs_init)


def _experts_body(be_ref, nused_ref, xs_ref, wgu_ref, bgu_ref, wd_ref, bd_ref, ys_ref):
    b = pl.program_id(0)

    @pl.when(b < nused_ref[0])
    def _():
        lo, hi = _unpack_rows(xs_ref[...])
        gu = (jnp.dot(lo.astype(BF16), wgu_ref[0, :HALF, :], preferred_element_type=F32)
              + jnp.dot(hi.astype(BF16), wgu_ref[0, HALF:, :], preferred_element_type=F32)
              + bgu_ref[0])
        gate = jnp.minimum(gu[:, :D_FF], SWIGLU_LIMIT)
        up = jnp.clip(gu[:, D_FF:], -SWIGLU_LIMIT, SWIGLU_LIMIT)
        glu = gate * jax.nn.sigmoid(gate * SWIGLU_ALPHA)
        act = ((up + 1.0) * glu).astype(BF16)
        y = jnp.dot(act, wd_ref[0], preferred_element_type=F32) + bd_ref[0]
        ys_ref[...] = _pack_rows(y)

    @pl.when(b >= nused_ref[0])
    def _():
        ys_ref[...] = jnp.zeros_like(ys_ref)


def _experts(block_e, n_used, xs, w_gu, b_gu, w_d, b_d):
    p = xs.shape[0]
    nb = p // MOE_BLOCK

    def xmap(b, be, nu):
        return (jnp.minimum(b, jnp.maximum(nu[0] - 1, 0)), 0)

    def wmap(b, be, nu):
        return (be[b], 0, 0)

    grid_spec = pltpu.PrefetchScalarGridSpec(
        num_scalar_prefetch=2,
        grid=(nb,),
        in_specs=[pl.BlockSpec((MOE_BLOCK, HALF), xmap),
                  pl.BlockSpec((1, D_MODEL, 2 * D_FF), wmap),
                  pl.BlockSpec((1, 1, 2 * D_FF), wmap),
                  pl.BlockSpec((1, D_FF, D_MODEL), wmap),
                  pl.BlockSpec((1, 1, D_MODEL), wmap)],
        out_specs=pl.BlockSpec((MOE_BLOCK, HALF), lambda b, be, nu: (b, 0)),
    )
    return pl.pallas_call(
        _experts_body,
        grid_spec=grid_spec,
        out_shape=jax.ShapeDtypeStruct((p, HALF), jnp.uint32),
        compiler_params=_cparams("arbitrary"),
        name="experts",
    )(block_e, n_used, xs, w_gu, b_gu, w_d, b_d)


def _combine_body(dest_ref, x2_ref, wt_ref, g_ref, ys_ref, o_ref, gbuf, sem, *, tm):
    def start(t, carry):
        for k in range(TOP_K):
            _row_copy(ys_ref, dest_ref[t * TOP_K + k], gbuf.at[k], t, sem).start()
        return carry

    lax.fori_loop(0, tm, start, 0)

    def wait(t, carry):
        for k in range(TOP_K):
            _row_copy(ys_ref, 0, gbuf.at[k], 0, sem).wait()
        return carry

    lax.fori_loop(0, tm, wait, 0)

    wt = wt_ref[...]
    lo_acc = jnp.zeros((tm, HALF), F32)
    hi_acc = jnp.zeros((tm, HALF), F32)
    for k in range(TOP_K):
        lo, hi = _unpack_rows(gbuf[k])
        wk = wt[:, k:k + 1]
        lo_acc = lo_acc + wk * lo
        hi_acc = hi_acc + wk * hi
    x2 = x2_ref[...]
    y_lo = x2[:, :HALF] + lo_acc
    y_hi = x2[:, HALF:] + hi_acc
    ms = (jnp.sum(y_lo * y_lo, axis=-1, keepdims=True) + jnp.sum(y_hi * y_hi, axis=-1, keepdims=True)) / D_MODEL
    inv = lax.rsqrt(ms + RMS_EPS)
    g = g_ref[...]
    o_ref[:, :HALF] = y_lo * inv * g[:, :HALF]
    o_ref[:, HALF:] = y_hi * inv * g[:, HALF:]


def _combine(dest_flat, x2, wt_tok, g_final, ys, tm):
    n = x2.shape[0]
    return pl.pallas_call(
        functools.partial(_combine_body, tm=tm),
        grid=(n // tm,),
        in_specs=[pl.BlockSpec((tm * TOP_K,), lambda i: (i,), memory_space=pltpu.SMEM),
                  pl.BlockSpec((tm, D_MODEL), lambda i: (i, 0)),
                  pl.BlockSpec((tm, TOP_K), lambda i: (i, 0)),
                  pl.BlockSpec((1, D_MODEL), lambda i: (0, 0)),
                  pl.BlockSpec(memory_space=pl.ANY)],
        out_specs=pl.BlockSpec((tm, D_MODEL), lambda i: (i, 0)),
        out_shape=jax.ShapeDtypeStruct((n, D_MODEL), F32),
        scratch_shapes=[pltpu.VMEM((TOP_K, tm, HALF), jnp.uint32), pltpu.SemaphoreType.DMA(())],
        compiler_params=_cparams("arbitrary"),
        name="combine",
    )(dest_flat, x2, wt_tok, g_final, ys)


def _rotary_tables(seq):
    half = ROT_DIM // 2
    inv_freq = ROPE_THETA ** (-jnp.arange(half, dtype=F32) * 2.0 / ROT_DIM)
    ang = jnp.arange(seq, dtype=F32)[:, None] * inv_freq[None, :]
    cos, sin = jnp.cos(ang), jnp.sin(ang)
    ones = jnp.ones((seq, HEAD_DIM - ROT_DIM), F32)
    zeros = jnp.zeros((seq, HEAD_DIM - ROT_DIM), F32)
    zh = jnp.zeros((seq, half), F32)
    cos_h = jnp.concatenate([cos, cos, ones], axis=1)
    s1_h = jnp.concatenate([zh, sin, zeros], axis=1)
    s2_h = jnp.concatenate([-sin, zh, zeros], axis=1)
    rep = LANES // HEAD_DIM
    return jnp.tile(cos_h, (1, rep)), jnp.tile(s1_h, (1, rep)), jnp.tile(s2_h, (1, rep))


def _prep_weights(g_mix, w_in, b_in, sinks, w_dw, b_dw, ln_g, ln_b, w_pw, b_pw, w_out,
                  g_ffn, w_router, b_router, w_gate_up, b_gate_up, w_down, b_down, g_final):
    r1 = lambda a: a.reshape(1, -1).astype(F32)
    return dict(
        g_mix=r1(g_mix[0]), w_in=w_in[0].astype(BF16), b_in=r1(b_in[0]), sinks=sinks[0].astype(F32),
        w_dw=w_dw[0].reshape(CONV_KERNEL, D_MODEL).astype(F32), b_dw=r1(b_dw[0]),
        ln_g=r1(ln_g[0]), ln_b=r1(ln_b[0]), w_pw=w_pw[0].astype(BF16), b_pw=r1(b_pw[0]),
        w_out=w_out[0].astype(BF16), g_ffn=r1(g_ffn[0]),
        w_r_t=w_router[0].T.astype(F32), b_r=b_router[0].reshape(N_EXPERTS, 1).astype(F32),
        w_gu=w_gate_up[0].astype(BF16), b_gu=b_gate_up[0].reshape(N_EXPERTS, 1, 2 * D_FF).astype(F32),
        w_d=w_down[0].astype(BF16), b_d=b_down[0].reshape(N_EXPERTS, 1, D_MODEL).astype(F32),
        g_final=r1(g_final),
    )


def _trunk(x, w, tables, tm_proj, tq, tm_mix, tm_moe):
    b, s, d = x.shape
    n = b * s
    xf = x.reshape(n, d)
    u, q, k, v, gc, ga = _inproj(xf, w["g_mix"], w["w_in"], w["b_in"], *tables, seq=s, tm=tm_proj)
    attn = _attn(q.reshape(b, s, d), k.reshape(b, s, KV_COLS), v.reshape(b, s, KV_COLS), w["sinks"], tq)
    x2, hp, idx_t, wt_t, rank_t, cnt = _mix(
        xf, u, attn.reshape(n, d), gc, ga, w["w_dw"], w["b_dw"], w["ln_g"], w["ln_b"], w["w_pw"], w["b_pw"],
        w["w_out"], w["g_ffn"], w["w_r_t"], w["b_r"], seq=s, tm=tm_mix)

    counts = cnt[:, 0].astype(jnp.int32)
    padded = (counts + MOE_BLOCK - 1) // MOE_BLOCK * MOE_BLOCK
    pend = jnp.cumsum(padded)
    pstart = pend - padded
    n_blocks = (n * TOP_K) // MOE_BLOCK + N_EXPERTS
    dest_flat = (pstart[idx_t] + rank_t).T.reshape(-1)
    block_e = jnp.minimum(jnp.searchsorted(pend, jnp.arange(n_blocks, dtype=jnp.int32) * MOE_BLOCK, side="right"),
                          N_EXPERTS - 1).astype(jnp.int32)
    n_used = (pend[-1:] // MOE_BLOCK).astype(jnp.int32)

    xs = _dispatch(dest_flat, hp, jnp.zeros((n_blocks * MOE_BLOCK, HALF), jnp.uint32), tm_moe)
    ys = _experts(block_e, n_used, xs, w["w_gu"], w["b_gu"], w["w_d"], w["b_d"])
    out = _combine(dest_flat, x2, wt_t.T, w["g_final"], ys, tm_moe)
    return out.reshape(b, s, d)


def kernel(x_prompt, x_sample, g_mix, w_in, b_in, sinks, w_dw, b_dw, ln_g, ln_b, w_pw, b_pw, w_out, g_ffn,
           w_router, b_router, w_gate_up, b_gate_up, w_down, b_down, g_final):
    w = _prep_weights(g_mix, w_in, b_in, sinks, w_dw, b_dw, ln_g, ln_b, w_pw, b_pw, w_out, g_ffn,
                      w_router, b_router, w_gate_up, b_gate_up, w_down, b_down, g_final)
    outs = []
    for x in (x_prompt, x_sample):
        s = x.shape[1]
        tables = _rotary_tables(s)
        outs.append(_trunk(x, w, tables, tm_proj=min(256, s), tq=min(512, s), tm_mix=min(256, s),
                           tm_moe=min(512, s)))
    return tuple(outs)
```

```python
import functools

import jax
import jax.numpy as jnp
from jax import lax
from jax.experimental import pallas as pl
from jax.experimental.pallas import tpu as pltpu

D_MODEL = 1024
N_Q_HEADS = 16
N_KV_HEADS = 2
HEAD_DIM = 64
GROUP = N_Q_HEADS // N_KV_HEADS
ROT_DIM = HEAD_DIM // 4
ROPE_THETA = 500000.0
WINDOW = 128
ATTN_SCALE = HEAD_DIM ** -0.5
CONV_KERNEL = 31
CONV_PAD = CONV_KERNEL // 2
N_EXPERTS = 32
TOP_K = 4
D_FF = D_MODEL
SWIGLU_LIMIT = 7.0
SWIGLU_ALPHA = 1.702
MOE_BLOCK = 512
RMS_EPS = 1e-5
LN_EPS = 1e-5

KV_COLS = N_KV_HEADS * HEAD_DIM
LANES = 128
SUBLANES = 8
HALF = D_MODEL // 2
HALO = 16
NEG_BIG = -1e30

VMEM_LIMIT = 56 * 1024 * 1024

F32 = jnp.float32
BF16 = jnp.bfloat16


def _cparams(*sem):
    return pltpu.CompilerParams(dimension_semantics=sem, vmem_limit_bytes=VMEM_LIMIT)


def _pack_rows(x):
    lo = lax.bitcast_convert_type(x[:, :HALF].astype(BF16).astype(F32), jnp.uint32)
    hi = lax.bitcast_convert_type(x[:, HALF:].astype(BF16).astype(F32), jnp.uint32)
    return (lo >> 16) | (hi & jnp.uint32(0xFFFF0000))


def _unpack_rows(p):
    lo = lax.bitcast_convert_type(p << 16, F32)
    hi = lax.bitcast_convert_type(p & jnp.uint32(0xFFFF0000), F32)
    return lo, hi


def _inproj_body(x_ref, g_ref, w_ref, b_ref, cos_ref, s1_ref, s2_ref,
                 u_ref, q_ref, k_ref, v_ref, gc_ref, ga_ref):
    x = x_ref[...]
    xn = x * lax.rsqrt(jnp.mean(x * x, axis=-1, keepdims=True) + RMS_EPS) * g_ref[...]
    xb = xn.astype(BF16)

    def proj(c0, c1):
        return jnp.dot(xb, w_ref[:, c0:c1], preferred_element_type=F32) + b_ref[:, c0:c1]

    cos = cos_ref[...]
    s1 = s1_ref[...]
    s2 = s2_ref[...]

    def rotary(t):
        half = ROT_DIM // 2
        outs = []
        for c in range(t.shape[1] // LANES):
            tc = t[:, c * LANES:(c + 1) * LANES]
            outs.append(tc * cos + pltpu.roll(tc, half, 1) * s1 + pltpu.roll(tc, LANES - half, 1) * s2)
        return outs[0] if len(outs) == 1 else jnp.concatenate(outs, axis=1)

    d = D_MODEL
    a = proj(0, d)
    gate = proj(d, 2 * d)
    u_ref[...] = (a * jax.nn.sigmoid(gate)).astype(BF16)
    q = proj(2 * d, 3 * d)
    q_ref[...] = (rotary(q) * ATTN_SCALE).astype(BF16)
    kv = proj(3 * d, 3 * d + 2 * KV_COLS)
    k_ref[...] = rotary(kv[:, :KV_COLS]).astype(BF16)
    v_ref[...] = kv[:, KV_COLS:].astype(BF16)
    c0 = 3 * d + 2 * KV_COLS
    gc_ref[...] = jax.nn.sigmoid(proj(c0, c0 + d)).astype(BF16)
    ga_ref[...] = jax.nn.sigmoid(proj(c0 + d, c0 + 2 * d)).astype(BF16)


def _inproj(x, g_mix, w_in, b_in, cos_t, s1_t, s2_t, seq, tm):
    n = x.shape[0]
    in_cols = w_in.shape[1]
    spt = seq // tm
    row = lambda w: pl.BlockSpec((tm, w), lambda i: (i, 0))
    const = lambda r, c: pl.BlockSpec((r, c), lambda i: (0, 0))
    tab = pl.BlockSpec((tm, LANES), lambda i: (i % spt, 0))
    big = jax.ShapeDtypeStruct((n, D_MODEL), BF16)
    small = jax.ShapeDtypeStruct((n, KV_COLS), BF16)
    return pl.pallas_call(
        _inproj_body,
        grid=(n // tm,),
        in_specs=[row(D_MODEL), const(1, D_MODEL), const(D_MODEL, in_cols), const(1, in_cols), tab, tab, tab],
        out_specs=[row(D_MODEL), row(D_MODEL), row(KV_COLS), row(KV_COLS), row(D_MODEL), row(D_MODEL)],
        out_shape=[big, big, small, small, big, big],
        compiler_params=_cparams("parallel"),
        name="inproj",
    )(x, g_mix, w_in, b_in, cos_t, s1_t, s2_t)


def _attn_body(q_ref, kp_ref, kc_ref, kn_ref, vp_ref, vc_ref, vn_ref, sink_ref, o_ref, kbuf, vbuf,
               *, tq, nblk):
    i = pl.program_id(1)
    blk = WINDOW
    kbuf[0:blk, :] = kp_ref[0]
    kbuf[blk:blk + tq, :] = kc_ref[0]
    kbuf[blk + tq:, :] = kn_ref[0]
    vbuf[0:blk, :] = vp_ref[0]
    vbuf[blk:blk + tq, :] = vc_ref[0]
    vbuf[blk + tq:, :] = vn_ref[0]

    qi = lax.broadcasted_iota(jnp.int32, (blk, 3 * blk), 0)
    kj = lax.broadcasted_iota(jnp.int32, (blk, 3 * blk), 1)
    band = jnp.abs(qi + blk - kj) <= WINDOW
    for j in range(tq // blk):
        n = i * (tq // blk) + j
        valid = band & ((kj >= blk) | (n > 0)) & ((kj < 2 * blk) | (n < nblk - 1))
        keys = kbuf[j * blk:(j + 3) * blk, :]
        vals = vbuf[j * blk:(j + 3) * blk, :]
        outs = []
        for h in range(N_Q_HEADS):
            g = h // GROUP
            qh = q_ref[0, j * blk:(j + 1) * blk, h * HEAD_DIM:(h + 1) * HEAD_DIM]
            kh = keys[:, g * HEAD_DIM:(g + 1) * HEAD_DIM]
            vh = vals[:, g * HEAD_DIM:(g + 1) * HEAD_DIM]
            s = lax.dot_general(qh, kh, (((1,), (1,)), ((), ())), preferred_element_type=F32)
            s = jnp.where(valid, s, NEG_BIG)
            sink = sink_ref[h]
            m = jnp.maximum(jnp.max(s, axis=-1, keepdims=True), sink)
            p = jnp.exp(s - m)
            denom = jnp.sum(p, axis=-1, keepdims=True) + jnp.exp(sink - m)
            o = jnp.dot(p.astype(BF16), vh, preferred_element_type=F32)
            outs.append(o / denom)
        o_ref[0, j * blk:(j + 1) * blk, :] = jnp.concatenate(outs, axis=1).astype(BF16)


def _attn(q, k, v, sinks, tq):
    b, s, _ = q.shape
    r = tq // WINDOW
    nblk = s // WINDOW
    cur = lambda w: pl.BlockSpec((1, tq, w), lambda bi, i: (bi, i, 0))
    prev = pl.BlockSpec((1, WINDOW, KV_COLS), lambda bi, i: (bi, jnp.maximum(i * r - 1, 0), 0))
    nxt = pl.BlockSpec((1, WINDOW, KV_COLS), lambda bi, i: (bi, jnp.minimum((i + 1) * r, nblk - 1), 0))
    return pl.pallas_call(
        functools.partial(_attn_body, tq=tq, nblk=nblk),
        grid=(b, s // tq),
        in_specs=[cur(D_MODEL), prev, cur(KV_COLS), nxt, prev, cur(KV_COLS), nxt,
                  pl.BlockSpec(memory_space=pltpu.SMEM)],
        out_specs=cur(D_MODEL),
        out_shape=jax.ShapeDtypeStruct((b, s, D_MODEL), BF16),
        scratch_shapes=[pltpu.VMEM((tq + 2 * WINDOW, KV_COLS), BF16),
                        pltpu.VMEM((tq + 2 * WINDOW, KV_COLS), BF16)],
        compiler_params=_cparams("parallel", "parallel"),
        name="attn",
    )(q, k, k, k, v, v, v, sinks)


CONV_ROWS = 32


def _mix_body(x_ref, up_ref, uc_ref, un_ref, at_ref, gc_ref, ga_ref,
              wdw_ref, bdw_ref, lng_ref, lnb_ref, wpw_ref, bpw_ref, wout_ref,
              gffn_ref, wr_ref, br_ref,
              x2_ref, hp_ref, idx_ref, wt_ref, rank_ref, cnt_ref,
              ubuf, sbuf, ybuf, run_ref, *, tm, spt):
    i = pl.program_id(0)
    first = (i % spt) == 0
    last = (i % spt) == spt - 1

    @pl.when(i == 0)
    def _():
        run_ref[...] = jnp.zeros_like(run_ref)

    ubuf[0:HALO, :] = jnp.where(first, 0.0, up_ref[...].astype(F32))
    ubuf[HALO:HALO + tm, :] = uc_ref[...].astype(F32)
    ubuf[HALO + tm:, :] = jnp.where(last, 0.0, un_ref[...].astype(F32))

    def shift_rows(r0, nrows):
        for c in range(D_MODEL // LANES):
            cs = slice(c * LANES, (c + 1) * LANES)
            win = ubuf[pl.ds(r0, nrows + SUBLANES), cs]
            for s in range(1, SUBLANES):
                sbuf[s - 1, pl.ds(r0, nrows), cs] = win[s:s + nrows]

    def shift_chunk(r, carry):
        shift_rows(pl.multiple_of(r * CONV_ROWS, CONV_ROWS), CONV_ROWS)
        return carry

    lax.fori_loop(0, tm // CONV_ROWS, shift_chunk, 0)
    shift_rows(tm, 2 * HALO - SUBLANES)

    base = HALO - CONV_PAD

    def conv_rows(r, carry):
        r0 = pl.multiple_of(r * CONV_ROWS, CONV_ROWS)
        for c in range(D_MODEL // LANES):
            cs = slice(c * LANES, (c + 1) * LANES)
            acc = jnp.zeros((CONV_ROWS, LANES), F32)
            for t in range(CONV_KERNEL):
                a, s = divmod(base + t, SUBLANES)
                rows = pl.ds(r0 + a * SUBLANES, CONV_ROWS)
                tap = ubuf[rows, cs] if s == 0 else sbuf[s - 1, rows, cs]
                acc = acc + tap * wdw_ref[t:t + 1, cs]
            ybuf[pl.ds(r0, CONV_ROWS), cs] = acc
        return carry

    lax.fori_loop(0, tm // CONV_ROWS, conv_rows, 0)

    y = ybuf[...] + bdw_ref[...]
    mu = jnp.mean(y, axis=-1, keepdims=True)
    yc = y - mu
    var = jnp.mean(yc * yc, axis=-1, keepdims=True)
    yn = yc * lax.rsqrt(var + LN_EPS) * lng_ref[...] + lnb_ref[...]
    act = yn * jax.nn.sigmoid(yn)
    conv = jnp.dot(act.astype(BF16), wpw_ref[...], preferred_element_type=F32) + bpw_ref[...]

    merged = gc_ref[...].astype(F32) * conv + ga_ref[...].astype(F32) * at_ref[...].astype(F32)
    x2 = x_ref[...] + jnp.dot(merged.astype(BF16), wout_ref[...], preferred_element_type=F32)
    x2_ref[...] = x2

    h = x2 * lax.rsqrt(jnp.mean(x2 * x2, axis=-1, keepdims=True) + RMS_EPS) * gffn_ref[...]
    hp_ref[...] = _pack_rows(h)

    logits = lax.dot_general(wr_ref[...], h, (((1,), (1,)), ((), ())),
                             preferred_element_type=F32, precision=lax.Precision.HIGHEST) + br_ref[...]
    eio = lax.broadcasted_iota(jnp.int32, (N_EXPERTS, tm), 0).astype(F32)
    work = logits
    sels, tops, idxs = [], [], []
    for _ in range(TOP_K):
        m = jnp.max(work, axis=0, keepdims=True)
        idx = jnp.min(jnp.where(work == m, eio, float(N_EXPERTS)), axis=0, keepdims=True)
        sel = eio == idx
        work = jnp.where(sel, -jnp.inf, work)
        sels.append(sel)
        tops.append(m)
        idxs.append(idx)
    es = [jnp.exp(t - tops[0]) for t in tops]
    tot = es[0] + es[1] + es[2] + es[3]
    wt_ref[...] = jnp.concatenate([e / tot for e in es], axis=0)
    idx_ref[...] = jnp.concatenate(idxs, axis=0).astype(jnp.int32)

    multihot = (sels[0] | sels[1] | sels[2] | sels[3])
    ti = lax.broadcasted_iota(jnp.int32, (tm, tm), 0)
    tj = lax.broadcasted_iota(jnp.int32, (tm, tm), 1)
    before = (ti < tj).astype(BF16)
    prefix = jnp.dot(multihot.astype(BF16), before, preferred_element_type=F32) + run_ref[...]
    rank_ref[...] = jnp.concatenate(
        [jnp.sum(jnp.where(s, prefix, 0.0), axis=0, keepdims=True) for s in sels], axis=0).astype(jnp.int32)
    run_ref[...] = run_ref[...] + jnp.sum(multihot.astype(F32), axis=1, keepdims=True)
    cnt_ref[...] = jnp.broadcast_to(run_ref[...], cnt_ref.shape)


def _mix(x, u, attn, gc, ga, w_dw, b_dw, ln_g, ln_b, w_pw, b_pw, w_out, g_ffn, w_r_t, b_r, seq, tm):
    n = x.shape[0]
    spt = seq // tm
    hb = tm // HALO
    nh = n // HALO
    row = lambda w: pl.BlockSpec((tm, w), lambda i: (i, 0))
    const = lambda r, c: pl.BlockSpec((r, c), lambda i: (0, 0))
    halo_prev = pl.BlockSpec((HALO, D_MODEL), lambda i: (jnp.maximum(i * hb - 1, 0), 0))
    halo_next = pl.BlockSpec((HALO, D_MODEL), lambda i: (jnp.minimum((i + 1) * hb, nh - 1), 0))
    tok = pl.BlockSpec((TOP_K, tm), lambda i: (0, i))
    return pl.pallas_call(
        functools.partial(_mix_body, tm=tm, spt=spt),
        grid=(n // tm,),
        in_specs=[row(D_MODEL), halo_prev, row(D_MODEL), halo_next, row(D_MODEL), row(D_MODEL), row(D_MODEL),
                  const(CONV_KERNEL, D_MODEL), const(1, D_MODEL), const(1, D_MODEL), const(1, D_MODEL),
                  const(D_MODEL, D_MODEL), const(1, D_MODEL), const(D_MODEL, D_MODEL),
                  const(1, D_MODEL), const(N_EXPERTS, D_MODEL), const(N_EXPERTS, 1)],
        out_specs=[row(D_MODEL), row(HALF), tok, tok, tok, const(N_EXPERTS, LANES)],
        out_shape=[jax.ShapeDtypeStruct((n, D_MODEL), F32),
                   jax.ShapeDtypeStruct((n, HALF), jnp.uint32),
                   jax.ShapeDtypeStruct((TOP_K, n), jnp.int32),
                   jax.ShapeDtypeStruct((TOP_K, n), F32),
                   jax.ShapeDtypeStruct((TOP_K, n), jnp.int32),
                   jax.ShapeDtypeStruct((N_EXPERTS, LANES), F32)],
        scratch_shapes=[pltpu.VMEM((tm + 2 * HALO, D_MODEL), F32),
                        pltpu.VMEM((SUBLANES - 1, tm + 2 * HALO - SUBLANES, D_MODEL), F32),
                        pltpu.VMEM((tm, D_MODEL), F32),
                        pltpu.VMEM((N_EXPERTS, 1), F32)],
        compiler_params=_cparams("arbitrary"),
        name="mix",
    )(x, u, u, u, attn, gc, ga, w_dw, b_dw, ln_g, ln_b, w_pw, b_pw, w_out, g_ffn, w_r_t, b_r)


def _row_copy(src_ref, src_row, dst_ref, dst_row, sem):
    return pltpu.make_async_copy(src_ref.at[pl.ds(src_row, 1)], dst_ref.at[pl.ds(dst_row, 1)], sem)


def _dispatch_body(dest_ref, hp_ref, xs_in_ref, xs_ref, sem, *, tm):
    del xs_in_ref

    def start(t, carry):
        for k in range(TOP_K):
            _row_copy(hp_ref, t, xs_ref, dest_ref[t * TOP_K + k], sem).start()
        return carry

    lax.fori_loop(0, tm, start, 0)

    def wait(t, carry):
        for k in range(TOP_K):
            _row_copy(hp_ref, 0, xs_ref, 0, sem).wait()
        return carry

    lax.fori_loop(0, tm, wait, 0)


def _dispatch(dest_flat, hp, xs_init, tm):
    n = hp.shape[0]
    return pl.pallas_call(
        functools.partial(_dispatch_body, tm=tm),
        grid=(n // tm,),
        in_specs=[pl.BlockSpec((tm * TOP_K,), lambda i: (i,), memory_space=pltpu.SMEM),
                  pl.BlockSpec((tm, HALF), lambda i: (i, 0)),
                  pl.BlockSpec(memory_space=pl.ANY)],
        out_specs=pl.BlockSpec(memory_space=pl.ANY),
        out_shape=jax.ShapeDtypeStruct(xs_init.shape, jnp.uint32),
        scratch_shapes=[pltpu.SemaphoreType.DMA(())],
        input_output_aliases={2: 0},
        compiler_params=pltpu.CompilerParams(dimension_semantics=("arbitrary",), has_side_effects=True,
                                             vmem_limit_bytes=VMEM_LIMIT),
        name="dispatch",
    )(dest_flat, hp, xs_init)


def _experts_body(be_ref, nused_ref, xs_ref, wgu_ref, bgu_ref, wd_ref, bd_ref, ys_ref):
    b = pl.program_id(0)

    @pl.when(b < nused_ref[0])
    def _():
        lo, hi = _unpack_rows(xs_ref[...])
        gu = (jnp.dot(lo.astype(BF16), wgu_ref[0, :HALF, :], preferred_element_type=F32)
              + jnp.dot(hi.astype(BF16), wgu_ref[0, HALF:, :], preferred_element_type=F32)
              + bgu_ref[0])
        gate = jnp.minimum(gu[:, :D_FF], SWIGLU_LIMIT)
        up = jnp.clip(gu[:, D_FF:], -SWIGLU_LIMIT, SWIGLU_LIMIT)
        glu = gate * jax.nn.sigmoid(gate * SWIGLU_ALPHA)
        act = ((up + 1.0) * glu).astype(BF16)
        y = jnp.dot(act, wd_ref[0], preferred_element_type=F32) + bd_ref[0]
        ys_ref[...] = _pack_rows(y)

    @pl.when(b >= nused_ref[0])
    def _():
        ys_ref[...] = jnp.zeros_like(ys_ref)


def _experts(block_e, n_used, xs, w_gu, b_gu, w_d, b_d):
    p = xs.shape[0]
    nb = p // MOE_BLOCK

    def xmap(b, be, nu):
        return (jnp.minimum(b, jnp.maximum(nu[0] - 1, 0)), 0)

    def wmap(b, be, nu):
        return (be[b], 0, 0)

    grid_spec = pltpu.PrefetchScalarGridSpec(
        num_scalar_prefetch=2,
        grid=(nb,),
        in_specs=[pl.BlockSpec((MOE_BLOCK, HALF), xmap),
                  pl.BlockSpec((1, D_MODEL, 2 * D_FF), wmap),
                  pl.BlockSpec((1, 1, 2 * D_FF), wmap),
                  pl.BlockSpec((1, D_FF, D_MODEL), wmap),
                  pl.BlockSpec((1, 1, D_MODEL), wmap)],
        out_specs=pl.BlockSpec((MOE_BLOCK, HALF), lambda b, be, nu: (b, 0)),
    )
    return pl.pallas_call(
        _experts_body,
        grid_spec=grid_spec,
        out_shape=jax.ShapeDtypeStruct((p, HALF), jnp.uint32),
        compiler_params=_cparams("arbitrary"),
        name="experts",
    )(block_e, n_used, xs, w_gu, b_gu, w_d, b_d)


def _combine_body(dest_ref, x2_ref, wt_ref, g_ref, ys_ref, o_ref, gbuf, sem, *, tm):
    def start(t, carry):
        for k in range(TOP_K):
            _row_copy(ys_ref, dest_ref[t * TOP_K + k], gbuf.at[k], t, sem).start()
        return carry

    lax.fori_loop(0, tm, start, 0)

    def wait(t, carry):
        for k in range(TOP_K):
            _row_copy(ys_ref, 0, gbuf.at[k], 0, sem).wait()
        return carry

    lax.fori_loop(0, tm, wait, 0)

    wt = wt_ref[...]
    lo_acc = jnp.zeros((tm, HALF), F32)
    hi_acc = jnp.zeros((tm, HALF), F32)
    for k in range(TOP_K):
        lo, hi = _unpack_rows(gbuf[k])
        wk = wt[:, k:k + 1]
        lo_acc = lo_acc + wk * lo
        hi_acc = hi_acc + wk * hi
    x2 = x2_ref[...]
    y_lo = x2[:, :HALF] + lo_acc
    y_hi = x2[:, HALF:] + hi_acc
    ms = (jnp.sum(y_lo * y_lo, axis=-1, keepdims=True) + jnp.sum(y_hi * y_hi, axis=-1, keepdims=True)) / D_MODEL
    inv = lax.rsqrt(ms + RMS_EPS)
    g = g_ref[...]
    o_ref[:, :HALF] = y_lo * inv * g[:, :HALF]
    o_ref[:, HALF:] = y_hi * inv * g[:, HALF:]


def _combine(dest_flat, x2, wt_tok, g_final, ys, tm):
    n = x2.shape[0]
    return pl.pallas_call(
        functools.partial(_combine_body, tm=tm),
        grid=(n // tm,),
        in_specs=[pl.BlockSpec((tm * TOP_K,), lambda i: (i,), memory_space=pltpu.SMEM),
                  pl.BlockSpec((tm, D_MODEL), lambda i: (i, 0)),
                  pl.BlockSpec((tm, TOP_K), lambda i: (i, 0)),
                  pl.BlockSpec((1, D_MODEL), lambda i: (0, 0)),
                  pl.BlockSpec(memory_space=pl.ANY)],
        out_specs=pl.BlockSpec((tm, D_MODEL), lambda i: (i, 0)),
        out_shape=jax.ShapeDtypeStruct((n, D_MODEL), F32),
        scratch_shapes=[pltpu.VMEM((TOP_K, tm, HALF), jnp.uint32), pltpu.SemaphoreType.DMA(())],
        compiler_params=_cparams("arbitrary"),
        name="combine",
    )(dest_flat, x2, wt_tok, g_final, ys)


def _rotary_tables(seq):
    half = ROT_DIM // 2
    inv_freq = ROPE_THETA ** (-jnp.arange(half, dtype=F32) * 2.0 / ROT_DIM)
    ang = jnp.arange(seq, dtype=F32)[:, None] * inv_freq[None, :]
    cos, sin = jnp.cos(ang), jnp.sin(ang)
    ones = jnp.ones((seq, HEAD_DIM - ROT_DIM), F32)
    zeros = jnp.zeros((seq, HEAD_DIM - ROT_DIM), F32)
    zh = jnp.zeros((seq, half), F32)
    cos_h = jnp.concatenate([cos, cos, ones], axis=1)
    s1_h = jnp.concatenate([zh, sin, zeros], axis=1)
    s2_h = jnp.concatenate([-sin, zh, zeros], axis=1)
    rep = LANES // HEAD_DIM
    return jnp.tile(cos_h, (1, rep)), jnp.tile(s1_h, (1, rep)), jnp.tile(s2_h, (1, rep))


def _prep_weights(g_mix, w_in, b_in, sinks, w_dw, b_dw, ln_g, ln_b, w_pw, b_pw, w_out,
                  g_ffn, w_router, b_router, w_gate_up, b_gate_up, w_down, b_down, g_final):
    r1 = lambda a: a.reshape(1, -1).astype(F32)
    return dict(
        g_mix=r1(g_mix[0]), w_in=w_in[0].astype(BF16), b_in=r1(b_in[0]), sinks=sinks[0].astype(F32),
        w_dw=w_dw[0].reshape(CONV_KERNEL, D_MODEL).astype(F32), b_dw=r1(b_dw[0]),
        ln_g=r1(ln_g[0]), ln_b=r1(ln_b[0]), w_pw=w_pw[0].astype(BF16), b_pw=r1(b_pw[0]),
        w_out=w_out[0].astype(BF16), g_ffn=r1(g_ffn[0]),
        w_r_t=w_router[0].T.astype(F32), b_r=b_router[0].reshape(N_EXPERTS, 1).astype(F32),
        w_gu=w_gate_up[0].astype(BF16), b_gu=b_gate_up[0].reshape(N_EXPERTS, 1, 2 * D_FF).astype(F32),
        w_d=w_down[0].astype(BF16), b_d=b_down[0].reshape(N_EXPERTS, 1, D_MODEL).astype(F32),
        g_final=r1(g_final),
    )


def _trunk(x, w, tables, tm_proj, tq, tm_mix, tm_moe):
    b, s, d = x.shape
    n = b * s
    xf = x.reshape(n, d)
    u, q, k, v, gc, ga = _inproj(xf, w["g_mix"], w["w_in"], w["b_in"], *tables, seq=s, tm=tm_proj)
    attn = _attn(q.reshape(b, s, d), k.reshape(b, s, KV_COLS), v.reshape(b, s, KV_COLS), w["sinks"], tq)
    x2, hp, idx_t, wt_t, rank_t, cnt = _mix(
        xf, u, attn.reshape(n, d), gc, ga, w["w_dw"], w["b_dw"], w["ln_g"], w["ln_b"], w["w_pw"], w["b_pw"],
        w["w_out"], w["g_ffn"], w["w_r_t"], w["b_r"], seq=s, tm=tm_mix)

    counts = cnt[:, 0].astype(jnp.int32)
    padded = (counts + MOE_BLOCK - 1) // MOE_BLOCK * MOE_BLOCK
    pend = jnp.cumsum(padded)
    pstart = pend - padded
    n_blocks = (n * TOP_K) // MOE_BLOCK + N_EXPERTS
    experts = jnp.arange(N_EXPERTS, dtype=jnp.int32)
    seg_start = jnp.sum(jnp.where(idx_t[:, :, None] == experts, pstart, 0), axis=-1)
    dest_flat = (seg_start + rank_t).T.reshape(-1)
    block_row = jnp.arange(n_blocks, dtype=jnp.int32) * MOE_BLOCK
    block_e = jnp.minimum(jnp.sum((pend[None, :] <= block_row[:, None]).astype(jnp.int32), axis=1), N_EXPERTS - 1)
    n_used = (pend[-1:] // MOE_BLOCK).astype(jnp.int32)

    xs = _dispatch(dest_flat, hp, jnp.zeros((n_blocks * MOE_BLOCK, HALF), jnp.uint32), tm_moe)
    ys = _experts(block_e, n_used, xs, w["w_gu"], w["b_gu"], w["w_d"], w["b_d"])
    out = _combine(dest_flat, x2, wt_t.T, w["g_final"], ys, tm_moe)
    return out.reshape(b, s, d)


def kernel(x_prompt, x_sample, g_mix, w_in, b_in, sinks, w_dw, b_dw, ln_g, ln_b, w_pw, b_pw, w_out, g_ffn,
           w_router, b_router, w_gate_up, b_gate_up, w_down, b_down, g_final):
    w = _prep_weights(g_mix, w_in, b_in, sinks, w_dw, b_dw, ln_g, ln_b, w_pw, b_pw, w_out, g_ffn,
                      w_router, b_router, w_gate_up, b_gate_up, w_down, b_down, g_final)
    outs = []
    for x in (x_prompt, x_sample):
        s = x.shape[1]
        tables = _rotary_tables(s)
        outs.append(_trunk(x, w, tables, tm_proj=min(256, s), tq=min(512, s), tm_mix=min(256, s),
                           tm_moe=min(512, s)))
    return tuple(outs)
```

```python
import functools

import jax
import jax.numpy as jnp
from jax import lax
from jax.experimental import pallas as pl
from jax.experimental.pallas import tpu as pltpu

D_MODEL = 1024
N_Q_HEADS = 16
N_KV_HEADS = 2
HEAD_DIM = 64
GROUP = N_Q_HEADS // N_KV_HEADS
ROT_DIM = HEAD_DIM // 4
ROPE_THETA = 500000.0
WINDOW = 128
ATTN_SCALE = HEAD_DIM ** -0.5
CONV_KERNEL = 31
CONV_PAD = CONV_KERNEL // 2
N_EXPERTS = 32
TOP_K = 4
D_FF = D_MODEL
SWIGLU_LIMIT = 7.0
SWIGLU_ALPHA = 1.702
MOE_BLOCK = 512
RMS_EPS = 1e-5
LN_EPS = 1e-5

KV_COLS = N_KV_HEADS * HEAD_DIM
LANES = 128
SUBLANES = 8
HALF = D_MODEL // 2
HALO = 16
NEG_BIG = -1e30
ATTN_LOOKAHEAD = 2

VMEM_LIMIT = 56 * 1024 * 1024

F32 = jnp.float32
BF16 = jnp.bfloat16


def _cparams(*sem):
    return pltpu.CompilerParams(dimension_semantics=sem, vmem_limit_bytes=VMEM_LIMIT)


def _pack_rows(x):
    lo = lax.bitcast_convert_type(x[:, :HALF].astype(BF16).astype(F32), jnp.uint32)
    hi = lax.bitcast_convert_type(x[:, HALF:].astype(BF16).astype(F32), jnp.uint32)
    return (lo >> 16) | (hi & jnp.uint32(0xFFFF0000))


def _unpack_rows(p):
    lo = lax.bitcast_convert_type(p << 16, F32)
    hi = lax.bitcast_convert_type(p & jnp.uint32(0xFFFF0000), F32)
    return lo, hi


def _inproj_body(x_ref, g_ref, w_ref, b_ref, cos_ref, s1_ref, s2_ref,
                 u_ref, q_ref, k_ref, v_ref, gc_ref, ga_ref):
    x = x_ref[...]
    xn = x * lax.rsqrt(jnp.mean(x * x, axis=-1, keepdims=True) + RMS_EPS) * g_ref[...]
    xb = xn.astype(BF16)

    def proj(c0, c1):
        return jnp.dot(xb, w_ref[:, c0:c1], preferred_element_type=F32) + b_ref[:, c0:c1]

    cos = cos_ref[...]
    s1 = s1_ref[...]
    s2 = s2_ref[...]

    def rotary(t):
        half = ROT_DIM // 2
        outs = []
        for c in range(t.shape[1] // LANES):
            tc = t[:, c * LANES:(c + 1) * LANES]
            outs.append(tc * cos + pltpu.roll(tc, half, 1) * s1 + pltpu.roll(tc, LANES - half, 1) * s2)
        return outs[0] if len(outs) == 1 else jnp.concatenate(outs, axis=1)

    d = D_MODEL
    a = proj(0, d)
    gate = proj(d, 2 * d)
    u_ref[...] = (a * jax.nn.sigmoid(gate)).astype(BF16)
    q = proj(2 * d, 3 * d)
    q_ref[...] = (rotary(q) * ATTN_SCALE).astype(BF16)
    kv = proj(3 * d, 3 * d + 2 * KV_COLS)
    k_ref[...] = rotary(kv[:, :KV_COLS]).astype(BF16)
    v_ref[...] = kv[:, KV_COLS:].astype(BF16)
    c0 = 3 * d + 2 * KV_COLS
    gc_ref[...] = jax.nn.sigmoid(proj(c0, c0 + d)).astype(BF16)
    ga_ref[...] = jax.nn.sigmoid(proj(c0 + d, c0 + 2 * d)).astype(BF16)


def _inproj(x, g_mix, w_in, b_in, cos_t, s1_t, s2_t, seq, tm):
    n = x.shape[0]
    in_cols = w_in.shape[1]
    spt = seq // tm
    row = lambda w: pl.BlockSpec((tm, w), lambda i: (i, 0))
    const = lambda r, c: pl.BlockSpec((r, c), lambda i: (0, 0))
    tab = pl.BlockSpec((tm, LANES), lambda i: (i % spt, 0))
    big = jax.ShapeDtypeStruct((n, D_MODEL), BF16)
    small = jax.ShapeDtypeStruct((n, KV_COLS), BF16)
    return pl.pallas_call(
        _inproj_body,
        grid=(n // tm,),
        in_specs=[row(D_MODEL), const(1, D_MODEL), const(D_MODEL, in_cols), const(1, in_cols), tab, tab, tab],
        out_specs=[row(D_MODEL), row(D_MODEL), row(KV_COLS), row(KV_COLS), row(D_MODEL), row(D_MODEL)],
        out_shape=[big, big, small, small, big, big],
        compiler_params=_cparams("arbitrary"),
        name="inproj",
    )(x, g_mix, w_in, b_in, cos_t, s1_t, s2_t)


def _attn_body(q_ref, kp_ref, kc_ref, kn_ref, vp_ref, vc_ref, vn_ref, sink_ref, o_ref, kbuf, vbuf,
               *, tq, nblk):
    i = pl.program_id(1)
    blk = WINDOW
    kbuf[0:blk, :] = kp_ref[0]
    kbuf[blk:blk + tq, :] = kc_ref[0]
    kbuf[blk + tq:, :] = kn_ref[0]
    vbuf[0:blk, :] = vp_ref[0]
    vbuf[blk:blk + tq, :] = vc_ref[0]
    vbuf[blk + tq:, :] = vn_ref[0]

    qi = lax.broadcasted_iota(jnp.int32, (blk, 3 * blk), 0)
    kj = lax.broadcasted_iota(jnp.int32, (blk, 3 * blk), 1)
    band = jnp.abs(qi + blk - kj) <= WINDOW
    def scores(j, h):
        g = h // GROUP
        qh = q_ref[0, j * blk:(j + 1) * blk, h * HEAD_DIM:(h + 1) * HEAD_DIM]
        kh = kbuf[j * blk:(j + 3) * blk, g * HEAD_DIM:(g + 1) * HEAD_DIM]
        return lax.dot_general(qh, kh, (((1,), (1,)), ((), ())), preferred_element_type=F32)

    units = [(j, h) for j in range(tq // blk) for h in range(N_Q_HEADS)]
    pending = [scores(*u) for u in units[:ATTN_LOOKAHEAD]]
    outs = []
    for idx, (j, h) in enumerate(units):
        if idx + ATTN_LOOKAHEAD < len(units):
            pending.append(scores(*units[idx + ATTN_LOOKAHEAD]))
        s = pending.pop(0)
        n = i * (tq // blk) + j
        valid = band & ((kj >= blk) | (n > 0)) & ((kj < 2 * blk) | (n < nblk - 1))
        g = h // GROUP
        vh = vbuf[j * blk:(j + 3) * blk, g * HEAD_DIM:(g + 1) * HEAD_DIM]
        s = jnp.where(valid, s, NEG_BIG)
        sink = sink_ref[h]
        m = jnp.maximum(jnp.max(s, axis=-1, keepdims=True), sink)
        p = jnp.exp(s - m)
        denom = jnp.sum(p, axis=-1, keepdims=True) + jnp.exp(sink - m)
        o = jnp.dot(p.astype(BF16), vh, preferred_element_type=F32)
        outs.append(o / denom)
        if h == N_Q_HEADS - 1:
            o_ref[0, j * blk:(j + 1) * blk, :] = jnp.concatenate(outs, axis=1).astype(BF16)
            outs = []


def _attn(q, k, v, sinks, tq):
    b, s, _ = q.shape
    r = tq // WINDOW
    nblk = s // WINDOW
    cur = lambda w: pl.BlockSpec((1, tq, w), lambda bi, i: (bi, i, 0))
    prev = pl.BlockSpec((1, WINDOW, KV_COLS), lambda bi, i: (bi, jnp.maximum(i * r - 1, 0), 0))
    nxt = pl.BlockSpec((1, WINDOW, KV_COLS), lambda bi, i: (bi, jnp.minimum((i + 1) * r, nblk - 1), 0))
    return pl.pallas_call(
        functools.partial(_attn_body, tq=tq, nblk=nblk),
        grid=(b, s // tq),
        in_specs=[cur(D_MODEL), prev, cur(KV_COLS), nxt, prev, cur(KV_COLS), nxt,
                  pl.BlockSpec(memory_space=pltpu.SMEM)],
        out_specs=cur(D_MODEL),
        out_shape=jax.ShapeDtypeStruct((b, s, D_MODEL), BF16),
        scratch_shapes=[pltpu.VMEM((tq + 2 * WINDOW, KV_COLS), BF16),
                        pltpu.VMEM((tq + 2 * WINDOW, KV_COLS), BF16)],
        compiler_params=_cparams("arbitrary", "arbitrary"),
        name="attn",
    )(q, k, k, k, v, v, v, sinks)


CONV_ROWS = 32


def _mix_body(x_ref, up_ref, uc_ref, un_ref, at_ref, gc_ref, ga_ref,
              wdw_ref, bdw_ref, lng_ref, lnb_ref, wpw_ref, bpw_ref, wout_ref,
              gffn_ref, wr_ref, br_ref,
              x2_ref, h_ref, wt_ref, lposp_ref, cnt_ref,
              ubuf, sbuf, ybuf, *, tm, spt):
    i = pl.program_id(0)
    first = (i % spt) == 0
    last = (i % spt) == spt - 1

    @pl.when(i == 0)
    def _():
        cnt_ref[...] = jnp.zeros_like(cnt_ref)

    ubuf[0:HALO, :] = jnp.where(first, 0.0, up_ref[...].astype(F32))
    ubuf[HALO:HALO + tm, :] = uc_ref[...].astype(F32)
    ubuf[HALO + tm:, :] = jnp.where(last, 0.0, un_ref[...].astype(F32))

    def shift_rows(r0, nrows):
        for c in range(D_MODEL // LANES):
            cs = slice(c * LANES, (c + 1) * LANES)
            win = ubuf[pl.ds(r0, nrows + SUBLANES), cs]
            for s in range(1, SUBLANES):
                sbuf[s - 1, pl.ds(r0, nrows), cs] = win[s:s + nrows]

    def shift_chunk(r, carry):
        shift_rows(pl.multiple_of(r * CONV_ROWS, CONV_ROWS), CONV_ROWS)
        return carry

    lax.fori_loop(0, tm // CONV_ROWS, shift_chunk, 0)
    shift_rows(tm, 2 * HALO - SUBLANES)

    base = HALO - CONV_PAD

    def conv_rows(r, carry):
        r0 = pl.multiple_of(r * CONV_ROWS, CONV_ROWS)
        for c in range(D_MODEL // LANES):
            cs = slice(c * LANES, (c + 1) * LANES)
            acc = jnp.zeros((CONV_ROWS, LANES), F32)
            for t in range(CONV_KERNEL):
                a, s = divmod(base + t, SUBLANES)
                rows = pl.ds(r0 + a * SUBLANES, CONV_ROWS)
                tap = ubuf[rows, cs] if s == 0 else sbuf[s - 1, rows, cs]
                acc = acc + tap * wdw_ref[t:t + 1, cs]
            ybuf[pl.ds(r0, CONV_ROWS), cs] = acc
        return carry

    lax.fori_loop(0, tm // CONV_ROWS, conv_rows, 0)

    y = ybuf[...] + bdw_ref[...]
    mu = jnp.mean(y, axis=-1, keepdims=True)
    yc = y - mu
    var = jnp.mean(yc * yc, axis=-1, keepdims=True)
    yn = yc * lax.rsqrt(var + LN_EPS) * lng_ref[...] + lnb_ref[...]
    act = yn * jax.nn.sigmoid(yn)
    conv = jnp.dot(act.astype(BF16), wpw_ref[...], preferred_element_type=F32) + bpw_ref[...]

    merged = gc_ref[...] * conv.astype(BF16) + ga_ref[...] * at_ref[...]
    x2 = x_ref[...] + jnp.dot(merged, wout_ref[...], preferred_element_type=F32)
    x2_ref[...] = x2

    h = x2 * lax.rsqrt(jnp.mean(x2 * x2, axis=-1, keepdims=True) + RMS_EPS) * gffn_ref[...]
    h_ref[...] = h.astype(BF16)

    logits = lax.dot_general(wr_ref[...], h, (((1,), (1,)), ((), ())),
                             preferred_element_type=F32, precision=lax.Precision.HIGHEST) + br_ref[...]
    eio = lax.broadcasted_iota(jnp.int32, (N_EXPERTS, tm), 0).astype(F32)
    work = logits
    sels, tops = [], []
    for _ in range(TOP_K):
        m = jnp.max(work, axis=0, keepdims=True)
        idx = jnp.min(jnp.where(work == m, eio, float(N_EXPERTS)), axis=0, keepdims=True)
        sel = eio == idx
        work = jnp.where(sel, -jnp.inf, work)
        sels.append(sel)
        tops.append(m)
    es = [jnp.exp(t - tops[0]) for t in tops]
    tot = es[0] + es[1] + es[2] + es[3]
    wt_ref[...] = jnp.concatenate([e / tot for e in es], axis=0)

    multihot = (sels[0] | sels[1] | sels[2] | sels[3])
    ti = lax.broadcasted_iota(jnp.int32, (tm, tm), 0)
    tj = lax.broadcasted_iota(jnp.int32, (tm, tm), 1)
    before = (ti < tj).astype(BF16)
    prefix = jnp.dot(multihot.astype(BF16), before, preferred_element_type=F32)
    cnt = jnp.sum(multihot.astype(F32), axis=1, keepdims=True)
    cnt_slot = jnp.floor((cnt + (SUBLANES - 1)) * (1.0 / SUBLANES)) * SUBLANES
    ei = lax.broadcasted_iota(jnp.int32, (N_EXPERTS, N_EXPERTS), 0)
    ej = lax.broadcasted_iota(jnp.int32, (N_EXPERTS, N_EXPERTS), 1)
    earlier = (ej < ei).astype(BF16)

    run_start = jnp.dot(earlier, jnp.broadcast_to(cnt_slot, (N_EXPERTS, LANES)).astype(BF16),
                        preferred_element_type=F32)[:, :1]
    pos = prefix + run_start
    lposp_ref[...] = jnp.concatenate(
        [jnp.sum(jnp.where(s, pos, 0.0), axis=0, keepdims=True) for s in sels], axis=0).astype(jnp.int32)
    tile_lane = lax.broadcasted_iota(jnp.int32, cnt_ref.shape, 1)
    cnt_ref[...] = jnp.where(tile_lane == i, cnt, cnt_ref[...])


def _mix(x, u, attn, gc, ga, w_dw, b_dw, ln_g, ln_b, w_pw, b_pw, w_out, g_ffn, w_r_t, b_r, seq, tm):
    n = x.shape[0]
    spt = seq // tm
    hb = tm // HALO
    nh = n // HALO
    row = lambda w: pl.BlockSpec((tm, w), lambda i: (i, 0))
    const = lambda r, c: pl.BlockSpec((r, c), lambda i: (0, 0))
    halo_prev = pl.BlockSpec((HALO, D_MODEL), lambda i: (jnp.maximum(i * hb - 1, 0), 0))
    halo_next = pl.BlockSpec((HALO, D_MODEL), lambda i: (jnp.minimum((i + 1) * hb, nh - 1), 0))
    tok = pl.BlockSpec((TOP_K, tm), lambda i: (0, i))
    return pl.pallas_call(
        functools.partial(_mix_body, tm=tm, spt=spt),
        grid=(n // tm,),
        in_specs=[row(D_MODEL), halo_prev, row(D_MODEL), halo_next, row(D_MODEL), row(D_MODEL), row(D_MODEL),
                  const(CONV_KERNEL, D_MODEL), const(1, D_MODEL), const(1, D_MODEL), const(1, D_MODEL),
                  const(D_MODEL, D_MODEL), const(1, D_MODEL), const(D_MODEL, D_MODEL),
                  const(1, D_MODEL), const(N_EXPERTS, D_MODEL), const(N_EXPERTS, 1)],
        out_specs=[row(D_MODEL), row(D_MODEL), tok, tok, const(N_EXPERTS, n // tm)],
        out_shape=[jax.ShapeDtypeStruct((n, D_MODEL), F32),
                   jax.ShapeDtypeStruct((n, D_MODEL), BF16),
                   jax.ShapeDtypeStruct((TOP_K, n), F32),
                   jax.ShapeDtypeStruct((TOP_K, n), jnp.int32),
                   jax.ShapeDtypeStruct((N_EXPERTS, n // tm), F32)],
        scratch_shapes=[pltpu.VMEM((tm + 2 * HALO, D_MODEL), F32),
                        pltpu.VMEM((SUBLANES - 1, tm + 2 * HALO - SUBLANES, D_MODEL), F32),
                        pltpu.VMEM((tm, D_MODEL), F32)],
        compiler_params=_cparams("arbitrary"),
        name="mix",
    )(x, u, u, u, attn, gc, ga, w_dw, b_dw, ln_g, ln_b, w_pw, b_pw, w_out, g_ffn, w_r_t, b_r)


RUN_ROWS = SUBLANES
MM_ROWS = 256


def _tile_slots(tm):
    return -(-(TOP_K * tm + N_EXPERTS * (RUN_ROWS - 1)) // MM_ROWS) * MM_ROWS


def _for_each_run_chunk(cnt_ref, tile, fn):
    def per_expert(e, carry):
        n_chunks = lax.shift_right_logical(cnt_ref[tile * N_EXPERTS + e] + (RUN_ROWS - 1),
                                           RUN_ROWS.bit_length() - 1)

        def per_chunk(j, c):
            fn(e, j)
            return c

        return lax.fori_loop(0, n_chunks, per_chunk, carry)

    lax.fori_loop(0, N_EXPERTS, per_expert, 0)


def _run_copy(src_ref, src_row, dst_ref, dst_row, sem):
    rows = lambda r: pl.ds(pl.multiple_of(r, RUN_ROWS), RUN_ROWS)
    return pltpu.make_async_copy(src_ref.at[rows(src_row)], dst_ref.at[rows(dst_row)], sem)


def _dispatch_body(cnt_ref, lstart_ref, gstart_ref, h_ref, lpos_ref, xs_in_ref, xs_ref, obuf, sem, *, tm):
    del xs_in_ref
    i = pl.program_id(0)

    lpos = lpos_ref[...]
    hb = h_ref[...]
    for c in range(obuf.shape[0] // MM_ROWS):
        rio = lax.broadcasted_iota(jnp.int32, (MM_ROWS, tm), 0) + c * MM_ROWS
        onehot = (rio == lpos[0:1]) | (rio == lpos[1:2]) | (rio == lpos[2:3]) | (rio == lpos[3:4])
        srt = jnp.dot(onehot.astype(BF16), hb, preferred_element_type=F32)
        lo = lax.bitcast_convert_type(srt[:, :HALF], jnp.uint32)
        hi = lax.bitcast_convert_type(srt[:, HALF:], jnp.uint32)
        obuf[c * MM_ROWS:(c + 1) * MM_ROWS, :] = (lo >> 16) | hi

    def copy(e, j):
        return _run_copy(obuf, lstart_ref[i * N_EXPERTS + e] + j * RUN_ROWS,
                         xs_ref, gstart_ref[i * N_EXPERTS + e] + j * RUN_ROWS, sem)

    _for_each_run_chunk(cnt_ref, i, lambda e, j: copy(e, j).start())
    _for_each_run_chunk(cnt_ref, i, lambda e, j: copy(e, j).wait())


def _dispatch(cnt, lstart, gstart, h, lpos_t, xs_init, tm):
    n = h.shape[0]
    grid_spec = pltpu.PrefetchScalarGridSpec(
        num_scalar_prefetch=3,
        grid=(n // tm,),
        in_specs=[pl.BlockSpec((tm, D_MODEL), lambda i, *_: (i, 0)),
                  pl.BlockSpec((TOP_K, tm), lambda i, *_: (0, i)),
                  pl.BlockSpec(memory_space=pl.ANY)],
        out_specs=pl.BlockSpec(memory_space=pl.ANY),
        scratch_shapes=[pltpu.VMEM((_tile_slots(tm), HALF), jnp.uint32),
                        pltpu.SemaphoreType.DMA(())],
    )
    return pl.pallas_call(
        functools.partial(_dispatch_body, tm=tm),
        grid_spec=grid_spec,
        out_shape=jax.ShapeDtypeStruct(xs_init.shape, jnp.uint32),
        input_output_aliases={5: 0},
        compiler_params=pltpu.CompilerParams(dimension_semantics=("arbitrary",), has_side_effects=True,
                                             vmem_limit_bytes=VMEM_LIMIT),
        name="dispatch",
    )(cnt, lstart, gstart, h, lpos_t, xs_init)


def _experts_body(be_ref, nused_ref, xs_ref, wgu_ref, bgu_ref, wd_ref, bd_ref, ys_ref):
    b = pl.program_id(0)

    @pl.when(b < nused_ref[0])
    def _():
        lo, hi = _unpack_rows(xs_ref[...])
        gu = (jnp.dot(lo.astype(BF16), wgu_ref[0, :HALF, :], preferred_element_type=F32)
              + jnp.dot(hi.astype(BF16), wgu_ref[0, HALF:, :], preferred_element_type=F32)
              + bgu_ref[0])
        gate = jnp.minimum(gu[:, :D_FF], SWIGLU_LIMIT)
        up = jnp.clip(gu[:, D_FF:], -SWIGLU_LIMIT, SWIGLU_LIMIT)
        glu = gate * jax.nn.sigmoid(gate * SWIGLU_ALPHA)
        act = ((up + 1.0) * glu).astype(BF16)
        y = jnp.dot(act, wd_ref[0], preferred_element_type=F32) + bd_ref[0]
        ys_ref[...] = _pack_rows(y)

    @pl.when(b >= nused_ref[0])
    def _():
        ys_ref[...] = jnp.zeros_like(ys_ref)


def _experts(block_e, n_used, xs, w_gu, b_gu, w_d, b_d):
    p = xs.shape[0]
    nb = p // MOE_BLOCK

    def xmap(b, be, nu):
        return (jnp.minimum(b, jnp.maximum(nu[0] - 1, 0)), 0)

    def wmap(b, be, nu):
        return (be[b], 0, 0)

    grid_spec = pltpu.PrefetchScalarGridSpec(
        num_scalar_prefetch=2,
        grid=(nb,),
        in_specs=[pl.BlockSpec((MOE_BLOCK, HALF), xmap),
                  pl.BlockSpec((1, D_MODEL, 2 * D_FF), wmap),
                  pl.BlockSpec((1, 1, 2 * D_FF), wmap),
                  pl.BlockSpec((1, D_FF, D_MODEL), wmap),
                  pl.BlockSpec((1, 1, D_MODEL), wmap)],
        out_specs=pl.BlockSpec((MOE_BLOCK, HALF), lambda b, be, nu: (b, 0)),
    )
    return pl.pallas_call(
        _experts_body,
        grid_spec=grid_spec,
        out_shape=jax.ShapeDtypeStruct((p, HALF), jnp.uint32),
        compiler_params=_cparams("arbitrary"),
        name="experts",
    )(block_e, n_used, xs, w_gu, b_gu, w_d, b_d)


def _combine_body(cnt_ref, lstartp_ref, gstart_ref, x2_ref, lposp_ref, wt_ref, g_ref, ys_ref, o_ref, gbuf, sem,
                  *, tm, slots):
    i = pl.program_id(0)

    @pl.when(i == 0)
    def _():
        gbuf[...] = jnp.zeros_like(gbuf)

    def copy(e, j):
        return _run_copy(ys_ref, gstart_ref[i * N_EXPERTS + e] + j * RUN_ROWS,
                         gbuf, lstartp_ref[i * N_EXPERTS + e] + j * RUN_ROWS, sem)

    _for_each_run_chunk(cnt_ref, i, lambda e, j: copy(e, j).start())
    _for_each_run_chunk(cnt_ref, i, lambda e, j: copy(e, j).wait())

    lposp = lposp_ref[...]
    wt = wt_ref[...]
    sio = lax.broadcasted_iota(jnp.int32, (tm, slots), 1)
    comb = jnp.zeros((tm, slots), F32)
    for k in range(TOP_K):
        comb = jnp.where(sio == lposp[:, k:k + 1], wt[:, k:k + 1], comb)
    comb = comb.astype(BF16)
    lo, hi = _unpack_rows(gbuf[...])
    moe_lo = jnp.dot(comb, lo.astype(BF16), preferred_element_type=F32)
    moe_hi = jnp.dot(comb, hi.astype(BF16), preferred_element_type=F32)

    x2 = x2_ref[...]
    y_lo = x2[:, :HALF] + moe_lo
    y_hi = x2[:, HALF:] + moe_hi
    ms = (jnp.sum(y_lo * y_lo, axis=-1, keepdims=True) + jnp.sum(y_hi * y_hi, axis=-1, keepdims=True)) / D_MODEL
    inv = lax.rsqrt(ms + RMS_EPS)
    g = g_ref[...]
    o_ref[:, :HALF] = y_lo * inv * g[:, :HALF]
    o_ref[:, HALF:] = y_hi * inv * g[:, HALF:]


def _combine(cnt, lstartp, gstart, x2, lposp_tok, wt_tok, g_final, ys, tm):
    n = x2.shape[0]
    slots = _tile_slots(tm)
    grid_spec = pltpu.PrefetchScalarGridSpec(
        num_scalar_prefetch=3,
        grid=(n // tm,),
        in_specs=[pl.BlockSpec((tm, D_MODEL), lambda i, *_: (i, 0)),
                  pl.BlockSpec((tm, TOP_K), lambda i, *_: (i, 0)),
                  pl.BlockSpec((tm, TOP_K), lambda i, *_: (i, 0)),
                  pl.BlockSpec((1, D_MODEL), lambda i, *_: (0, 0)),
                  pl.BlockSpec(memory_space=pl.ANY)],
        out_specs=pl.BlockSpec((tm, D_MODEL), lambda i, *_: (i, 0)),
        scratch_shapes=[pltpu.VMEM((slots, HALF), jnp.uint32), pltpu.SemaphoreType.DMA(())],
    )
    return pl.pallas_call(
        functools.partial(_combine_body, tm=tm, slots=slots),
        grid_spec=grid_spec,
        out_shape=jax.ShapeDtypeStruct((n, D_MODEL), F32),
        compiler_params=_cparams("arbitrary"),
        name="combine",
    )(cnt, lstartp, gstart, x2, lposp_tok, wt_tok, g_final, ys)


def _rotary_tables(seq):
    half = ROT_DIM // 2
    inv_freq = ROPE_THETA ** (-jnp.arange(half, dtype=F32) * 2.0 / ROT_DIM)
    ang = jnp.arange(seq, dtype=F32)[:, None] * inv_freq[None, :]
    cos, sin = jnp.cos(ang), jnp.sin(ang)
    ones = jnp.ones((seq, HEAD_DIM - ROT_DIM), F32)
    zeros = jnp.zeros((seq, HEAD_DIM - ROT_DIM), F32)
    zh = jnp.zeros((seq, half), F32)
    cos_h = jnp.concatenate([cos, cos, ones], axis=1)
    s1_h = jnp.concatenate([zh, sin, zeros], axis=1)
    s2_h = jnp.concatenate([-sin, zh, zeros], axis=1)
    rep = LANES // HEAD_DIM
    return jnp.tile(cos_h, (1, rep)), jnp.tile(s1_h, (1, rep)), jnp.tile(s2_h, (1, rep))


def _prep_weights(g_mix, w_in, b_in, sinks, w_dw, b_dw, ln_g, ln_b, w_pw, b_pw, w_out,
                  g_ffn, w_router, b_router, w_gate_up, b_gate_up, w_down, b_down, g_final):
    r1 = lambda a: a.reshape(1, -1).astype(F32)
    return dict(
        g_mix=r1(g_mix[0]), w_in=w_in[0].astype(BF16), b_in=r1(b_in[0]), sinks=sinks[0].astype(F32),
        w_dw=w_dw[0].reshape(CONV_KERNEL, D_MODEL).astype(F32), b_dw=r1(b_dw[0]),
        ln_g=r1(ln_g[0]), ln_b=r1(ln_b[0]), w_pw=w_pw[0].astype(BF16), b_pw=r1(b_pw[0]),
        w_out=w_out[0].astype(BF16), g_ffn=r1(g_ffn[0]),
        w_r_t=w_router[0].T.astype(F32), b_r=b_router[0].reshape(N_EXPERTS, 1).astype(F32),
        w_gu=w_gate_up[0].astype(BF16), b_gu=b_gate_up[0].reshape(N_EXPERTS, 1, 2 * D_FF).astype(F32),
        w_d=w_down[0].astype(BF16), b_d=b_down[0].reshape(N_EXPERTS, 1, D_MODEL).astype(F32),
        g_final=r1(g_final),
    )


def _trunk(x, w, tables, tm_proj, tq, tm_mix):
    b, s, d = x.shape
    n = b * s
    xf = x.reshape(n, d)
    u, q, k, v, gc, ga = _inproj(xf, w["g_mix"], w["w_in"], w["b_in"], *tables, seq=s, tm=tm_proj)
    attn = _attn(q.reshape(b, s, d), k.reshape(b, s, KV_COLS), v.reshape(b, s, KV_COLS), w["sinks"], tq)
    x2, h, wt_t, lposp_t, cnt = _mix(
        xf, u, attn.reshape(n, d), gc, ga, w["w_dw"], w["b_dw"], w["ln_g"], w["ln_b"], w["w_pw"], w["b_pw"],
        w["w_out"], w["g_ffn"], w["w_r_t"], w["b_r"], seq=s, tm=tm_mix)

    n_tiles = n // tm_mix
    cnt_te = cnt.T.astype(jnp.int32)
    run_te = (cnt_te + RUN_ROWS - 1) // RUN_ROWS * RUN_ROWS
    total = jnp.sum(run_te, axis=0)
    padded = (total + MOE_BLOCK - 1) // MOE_BLOCK * MOE_BLOCK
    pend = jnp.cumsum(padded)
    pstart = pend - padded
    gstart = pstart[None, :] + jnp.cumsum(run_te, axis=0) - run_te
    lstart = jnp.cumsum(run_te, axis=1) - run_te
    flat = lambda a: a.reshape(-1).astype(jnp.int32)
    max_rows = n * TOP_K + n_tiles * N_EXPERTS * (RUN_ROWS - 1) + N_EXPERTS * (MOE_BLOCK - 1)
    n_blocks = -(-max_rows // MOE_BLOCK)
    block_row = jnp.arange(n_blocks, dtype=jnp.int32) * MOE_BLOCK
    block_e = jnp.minimum(jnp.sum((pend[None, :] <= block_row[:, None]).astype(jnp.int32), axis=1), N_EXPERTS - 1)
    n_used = (pend[-1:] // MOE_BLOCK).astype(jnp.int32)

    xs = _dispatch(flat(cnt_te), flat(lstart), flat(gstart), h, lposp_t,
                   jnp.zeros((n_blocks * MOE_BLOCK, HALF), jnp.uint32), tm_mix)
    ys = _experts(block_e, n_used, xs, w["w_gu"], w["b_gu"], w["w_d"], w["b_d"])
    out = _combine(flat(cnt_te), flat(lstart), flat(gstart), x2, lposp_t.T, wt_t.T, w["g_final"], ys, tm_mix)
    return out.reshape(b, s, d)


def kernel(x_prompt, x_sample, g_mix, w_in, b_in, sinks, w_dw, b_dw, ln_g, ln_b, w_pw, b_pw, w_out, g_ffn,
           w_router, b_router, w_gate_up, b_gate_up, w_down, b_down, g_final):
    w = _prep_weights(g_mix, w_in, b_in, sinks, w_dw, b_dw, ln_g, ln_b, w_pw, b_pw, w_out, g_ffn,
                      w_router, b_router, w_gate_up, b_gate_up, w_down, b_down, g_final)
    outs = []
    for x in (x_prompt, x_sample):
        s = x.shape[1]
        tables = _rotary_tables(s)
        outs.append(_trunk(x, w, tables, tm_proj=min(256, s), tq=min(512, s), tm_mix=min(256, s)))
    return tuple(outs)
```

```python
import functools

import jax
import jax.numpy as jnp
from jax import lax
from jax.experimental import pallas as pl
from jax.experimental.pallas import tpu as pltpu

D_MODEL = 1024
N_Q_HEADS = 16
N_KV_HEADS = 2
HEAD_DIM = 64
GROUP = N_Q_HEADS // N_KV_HEADS
ROT_DIM = HEAD_DIM // 4
ROPE_THETA = 500000.0
WINDOW = 128
ATTN_SCALE = HEAD_DIM ** -0.5
CONV_KERNEL = 31
CONV_PAD = CONV_KERNEL // 2
N_EXPERTS = 32
TOP_K = 4
D_FF = D_MODEL
SWIGLU_LIMIT = 7.0
SWIGLU_ALPHA = 1.702
MOE_BLOCK = 512
RMS_EPS = 1e-5
LN_EPS = 1e-5

KV_COLS = N_KV_HEADS * HEAD_DIM
LANES = 128
SUBLANES = 8
HALF = D_MODEL // 2
HALO = 16
NEG_BIG = -1e30
ATTN_LOOKAHEAD = 2

VMEM_LIMIT = 56 * 1024 * 1024

F32 = jnp.float32
BF16 = jnp.bfloat16


def _cparams(*sem):
    return pltpu.CompilerParams(dimension_semantics=sem, vmem_limit_bytes=VMEM_LIMIT)


def _pack_rows(x):
    lo = lax.bitcast_convert_type(x[:, :HALF].astype(BF16).astype(F32), jnp.uint32)
    hi = lax.bitcast_convert_type(x[:, HALF:].astype(BF16).astype(F32), jnp.uint32)
    return (lo >> 16) | (hi & jnp.uint32(0xFFFF0000))


def _unpack_rows(p):
    lo = lax.bitcast_convert_type(p << 16, F32)
    hi = lax.bitcast_convert_type(p & jnp.uint32(0xFFFF0000), F32)
    return lo, hi


def _inproj_body(x_ref, g_ref, w_ref, b_ref, cos_ref, s1_ref, s2_ref,
                 u_ref, q_ref, k_ref, v_ref, gc_ref, ga_ref):
    x = x_ref[...]
    xn = x * lax.rsqrt(jnp.mean(x * x, axis=-1, keepdims=True) + RMS_EPS) * g_ref[...]
    xb = xn.astype(BF16)

    def proj(c0, c1):
        return jnp.dot(xb, w_ref[:, c0:c1], preferred_element_type=F32) + b_ref[:, c0:c1]

    cos = cos_ref[...]
    s1 = s1_ref[...]
    s2 = s2_ref[...]

    def rotary(t):
        half = ROT_DIM // 2
        outs = []
        for c in range(t.shape[1] // LANES):
            tc = t[:, c * LANES:(c + 1) * LANES]
            outs.append(tc * cos + pltpu.roll(tc, half, 1) * s1 + pltpu.roll(tc, LANES - half, 1) * s2)
        return outs[0] if len(outs) == 1 else jnp.concatenate(outs, axis=1)

    d = D_MODEL
    a = proj(0, d)
    gate = proj(d, 2 * d)
    u_ref[...] = (a * jax.nn.sigmoid(gate)).astype(BF16)
    q = proj(2 * d, 3 * d)
    q_ref[...] = (rotary(q) * ATTN_SCALE).astype(BF16)
    kv = proj(3 * d, 3 * d + 2 * KV_COLS)
    k_ref[...] = rotary(kv[:, :KV_COLS]).astype(BF16)
    v_ref[...] = kv[:, KV_COLS:].astype(BF16)
    c0 = 3 * d + 2 * KV_COLS
    gc_ref[...] = jax.nn.sigmoid(proj(c0, c0 + d)).astype(BF16)
    ga_ref[...] = jax.nn.sigmoid(proj(c0 + d, c0 + 2 * d)).astype(BF16)


def _inproj(x, g_mix, w_in, b_in, cos_t, s1_t, s2_t, seq, tm):
    n = x.shape[0]
    in_cols = w_in.shape[1]
    spt = seq // tm
    row = lambda w: pl.BlockSpec((tm, w), lambda i: (i, 0))
    const = lambda r, c: pl.BlockSpec((r, c), lambda i: (0, 0))
    tab = pl.BlockSpec((tm, LANES), lambda i: (i % spt, 0))
    big = jax.ShapeDtypeStruct((n, D_MODEL), BF16)
    small = jax.ShapeDtypeStruct((n, KV_COLS), BF16)
    return pl.pallas_call(
        _inproj_body,
        grid=(n // tm,),
        in_specs=[row(D_MODEL), const(1, D_MODEL), const(D_MODEL, in_cols), const(1, in_cols), tab, tab, tab],
        out_specs=[row(D_MODEL), row(D_MODEL), row(KV_COLS), row(KV_COLS), row(D_MODEL), row(D_MODEL)],
        out_shape=[big, big, small, small, big, big],
        compiler_params=_cparams("arbitrary"),
        name="inproj",
    )(x, g_mix, w_in, b_in, cos_t, s1_t, s2_t)


def _attn_body(q_ref, kp_ref, kc_ref, kn_ref, vp_ref, vc_ref, vn_ref, sink_ref, o_ref, kbuf, vbuf,
               *, tq, nblk):
    i = pl.program_id(1)
    blk = WINDOW
    kbuf[0:blk, :] = kp_ref[0]
    kbuf[blk:blk + tq, :] = kc_ref[0]
    kbuf[blk + tq:, :] = kn_ref[0]
    vbuf[0:blk, :] = vp_ref[0]
    vbuf[blk:blk + tq, :] = vc_ref[0]
    vbuf[blk + tq:, :] = vn_ref[0]

    qi = lax.broadcasted_iota(jnp.int32, (blk, 3 * blk), 0)
    kj = lax.broadcasted_iota(jnp.int32, (blk, 3 * blk), 1)
    band = jnp.abs(qi + blk - kj) <= WINDOW

    def scores(j, h):
        g = h // GROUP
        qh = q_ref[0, j * blk:(j + 1) * blk, h * HEAD_DIM:(h + 1) * HEAD_DIM]
        kh = kbuf[j * blk:(j + 3) * blk, g * HEAD_DIM:(g + 1) * HEAD_DIM]
        return lax.dot_general(qh, kh, (((1,), (1,)), ((), ())), preferred_element_type=F32)

    units = [(j, h) for j in range(tq // blk) for h in range(N_Q_HEADS)]
    pending = [scores(*u) for u in units[:ATTN_LOOKAHEAD]]
    outs = []
    for idx, (j, h) in enumerate(units):
        if idx + ATTN_LOOKAHEAD < len(units):
            pending.append(scores(*units[idx + ATTN_LOOKAHEAD]))
        s = pending.pop(0)
        n = i * (tq // blk) + j
        valid = band & ((kj >= blk) | (n > 0)) & ((kj < 2 * blk) | (n < nblk - 1))
        g = h // GROUP
        vh = vbuf[j * blk:(j + 3) * blk, g * HEAD_DIM:(g + 1) * HEAD_DIM]
        s = jnp.where(valid, s, NEG_BIG)
        sink = sink_ref[h]
        m = jnp.maximum(jnp.max(s, axis=-1, keepdims=True), sink)
        p = jnp.exp(s - m)
        denom = jnp.sum(p, axis=-1, keepdims=True) + jnp.exp(sink - m)
        o = jnp.dot(p.astype(BF16), vh, preferred_element_type=F32)
        outs.append(o / denom)
        if h == N_Q_HEADS - 1:
            o_ref[0, j * blk:(j + 1) * blk, :] = jnp.concatenate(outs, axis=1).astype(BF16)
            outs = []


def _attn(q, k, v, sinks, tq):
    b, s, _ = q.shape
    r = tq // WINDOW
    nblk = s // WINDOW
    cur = lambda w: pl.BlockSpec((1, tq, w), lambda bi, i: (bi, i, 0))
    prev = pl.BlockSpec((1, WINDOW, KV_COLS), lambda bi, i: (bi, jnp.maximum(i * r - 1, 0), 0))
    nxt = pl.BlockSpec((1, WINDOW, KV_COLS), lambda bi, i: (bi, jnp.minimum((i + 1) * r, nblk - 1), 0))
    return pl.pallas_call(
        functools.partial(_attn_body, tq=tq, nblk=nblk),
        grid=(b, s // tq),
        in_specs=[cur(D_MODEL), prev, cur(KV_COLS), nxt, prev, cur(KV_COLS), nxt,
                  pl.BlockSpec(memory_space=pltpu.SMEM)],
        out_specs=cur(D_MODEL),
        out_shape=jax.ShapeDtypeStruct((b, s, D_MODEL), BF16),
        scratch_shapes=[pltpu.VMEM((tq + 2 * WINDOW, KV_COLS), BF16),
                        pltpu.VMEM((tq + 2 * WINDOW, KV_COLS), BF16)],
        compiler_params=_cparams("arbitrary", "arbitrary"),
        name="attn",
    )(q, k, k, k, v, v, v, sinks)


CONV_ROWS = 64


def _mix_body(x_ref, up_ref, uc_ref, un_ref, at_ref, gc_ref, ga_ref,
              wdw_ref, bdw_ref, lng_ref, lnb_ref, wpw_ref, bpw_ref, wout_ref,
              gffn_ref, wrh_ref, wrl_ref, br_ref,
              x2_ref, h_ref, wt_ref, lposp_ref, cnt_ref,
              ubuf, sbuf, ybuf, *, tm, spt):
    i = pl.program_id(0)
    first = (i % spt) == 0
    last = (i % spt) == spt - 1

    @pl.when(i == 0)
    def _():
        cnt_ref[...] = jnp.zeros_like(cnt_ref)

    for c in range(D_MODEL // LANES):
        cs = slice(c * LANES, (c + 1) * LANES)
        ubuf[c, 0:HALO, :] = jnp.where(first, 0.0, up_ref[:, cs].astype(F32))
        ubuf[c, HALO:HALO + tm, :] = uc_ref[:, cs].astype(F32)
        ubuf[c, HALO + tm:, :] = jnp.where(last, 0.0, un_ref[:, cs].astype(F32))

    def shift_rows(r0, nrows):
        for c in range(D_MODEL // LANES):
            win = ubuf[c, pl.ds(r0, nrows + SUBLANES), :]
            for s in range(1, SUBLANES):
                sbuf[s - 1, c, pl.ds(r0, nrows), :] = win[s:s + nrows]

    def shift_chunk(r, carry):
        shift_rows(pl.multiple_of(r * CONV_ROWS, CONV_ROWS), CONV_ROWS)
        return carry

    lax.fori_loop(0, tm // CONV_ROWS, shift_chunk, 0)
    shift_rows(tm, 2 * HALO - SUBLANES)

    base = HALO - CONV_PAD

    for c in range(D_MODEL // LANES):
        cs = slice(c * LANES, (c + 1) * LANES)
        wts = [jnp.broadcast_to(wdw_ref[t:t + 1, cs], (SUBLANES, LANES)) for t in range(CONV_KERNEL)]
        bias = jnp.broadcast_to(bdw_ref[:, cs], (SUBLANES, LANES))

        def conv_rows(r, carry, c=c, cs=cs, wts=wts, bias=bias):
            r0 = pl.multiple_of(r * CONV_ROWS, CONV_ROWS)
            accs = [bias] * (CONV_ROWS // SUBLANES)
            for t in range(CONV_KERNEL):
                a, s = divmod(base + t, SUBLANES)
                rows = pl.ds(r0 + a * SUBLANES, CONV_ROWS)
                tap = ubuf[c, rows, :] if s == 0 else sbuf[s - 1, c, rows, :]
                accs = [acc + tap[k * SUBLANES:(k + 1) * SUBLANES] * wts[t] for k, acc in enumerate(accs)]
            ybuf[pl.ds(r0, CONV_ROWS), cs] = jnp.concatenate(accs, axis=0)
            return carry

        lax.fori_loop(0, tm // CONV_ROWS, conv_rows, 0)

    y = ybuf[...]
    mu = jnp.mean(y, axis=-1, keepdims=True)
    yc = y - mu
    var = jnp.mean(yc * yc, axis=-1, keepdims=True)
    yn = yc * lax.rsqrt(var + LN_EPS) * lng_ref[...] + lnb_ref[...]
    act = yn * jax.nn.sigmoid(yn)
    conv = jnp.dot(act.astype(BF16), wpw_ref[...], preferred_element_type=F32) + bpw_ref[...]

    merged = gc_ref[...] * conv.astype(BF16) + ga_ref[...] * at_ref[...]
    x2 = x_ref[...] + jnp.dot(merged, wout_ref[...], preferred_element_type=F32)
    x2_ref[...] = x2

    h = x2 * lax.rsqrt(jnp.mean(x2 * x2, axis=-1, keepdims=True) + RMS_EPS) * gffn_ref[...]
    h_hi = h.astype(BF16)
    h_ref[...] = h_hi

    h_lo = (h - h_hi.astype(F32)).astype(BF16)
    nt = lambda a, b: lax.dot_general(a, b, (((1,), (1,)), ((), ())), preferred_element_type=F32)
    logits = nt(wrh_ref[...], h_hi) + (nt(wrh_ref[...], h_lo) + nt(wrl_ref[...], h_hi)) + br_ref[...]
    eio = lax.broadcasted_iota(jnp.int32, (N_EXPERTS, tm), 0).astype(F32)
    work = logits
    sels, tops = [], []
    for _ in range(TOP_K):
        m = jnp.max(work, axis=0, keepdims=True)
        idx = jnp.min(jnp.where(work == m, eio, float(N_EXPERTS)), axis=0, keepdims=True)
        sel = eio == idx
        work = jnp.where(sel, -jnp.inf, work)
        sels.append(sel)
        tops.append(m)
    es = [jnp.exp(t - tops[0]) for t in tops]
    tot = es[0] + es[1] + es[2] + es[3]
    wt_ref[...] = jnp.concatenate([e / tot for e in es], axis=0)

    multihot = (sels[0] | sels[1] | sels[2] | sels[3])
    ti = lax.broadcasted_iota(jnp.int32, (tm, tm), 0)
    tj = lax.broadcasted_iota(jnp.int32, (tm, tm), 1)
    before = (ti < tj).astype(BF16)
    prefix = jnp.dot(multihot.astype(BF16), before, preferred_element_type=F32)
    cnt = jnp.sum(multihot.astype(F32), axis=1, keepdims=True)
    cnt_slot = jnp.floor((cnt + (SUBLANES - 1)) * (1.0 / SUBLANES)) * SUBLANES
    ei = lax.broadcasted_iota(jnp.int32, (N_EXPERTS, N_EXPERTS), 0)
    ej = lax.broadcasted_iota(jnp.int32, (N_EXPERTS, N_EXPERTS), 1)
    earlier = (ej < ei).astype(BF16)

    run_start = jnp.dot(earlier, jnp.broadcast_to(cnt_slot, (N_EXPERTS, LANES)).astype(BF16),
                        preferred_element_type=F32)[:, :1]
    pos = prefix + run_start
    lposp_ref[...] = jnp.concatenate(
        [jnp.sum(jnp.where(s, pos, 0.0), axis=0, keepdims=True) for s in sels], axis=0).astype(jnp.int32)
    tile_lane = lax.broadcasted_iota(jnp.int32, cnt_ref.shape, 1)
    cnt_ref[...] = jnp.where(tile_lane == i, cnt, cnt_ref[...])


def _mix(x, u, attn, gc, ga, w_dw, b_dw, ln_g, ln_b, w_pw, b_pw, w_out, g_ffn, w_r_hi, w_r_lo, b_r, seq, tm):
    n = x.shape[0]
    spt = seq // tm
    hb = tm // HALO
    nh = n // HALO
    row = lambda w: pl.BlockSpec((tm, w), lambda i: (i, 0))
    const = lambda r, c: pl.BlockSpec((r, c), lambda i: (0, 0))
    halo_prev = pl.BlockSpec((HALO, D_MODEL), lambda i: (jnp.maximum(i * hb - 1, 0), 0))
    halo_next = pl.BlockSpec((HALO, D_MODEL), lambda i: (jnp.minimum((i + 1) * hb, nh - 1), 0))
    tok = pl.BlockSpec((TOP_K, tm), lambda i: (0, i))
    return pl.pallas_call(
        functools.partial(_mix_body, tm=tm, spt=spt),
        grid=(n // tm,),
        in_specs=[row(D_MODEL), halo_prev, row(D_MODEL), halo_next, row(D_MODEL), row(D_MODEL), row(D_MODEL),
                  const(CONV_KERNEL, D_MODEL), const(1, D_MODEL), const(1, D_MODEL), const(1, D_MODEL),
                  const(D_MODEL, D_MODEL), const(1, D_MODEL), const(D_MODEL, D_MODEL),
                  const(1, D_MODEL), const(N_EXPERTS, D_MODEL), const(N_EXPERTS, D_MODEL), const(N_EXPERTS, 1)],
        out_specs=[row(D_MODEL), row(D_MODEL), tok, tok, const(N_EXPERTS, n // tm)],
        out_shape=[jax.ShapeDtypeStruct((n, D_MODEL), F32),
                   jax.ShapeDtypeStruct((n, D_MODEL), BF16),
                   jax.ShapeDtypeStruct((TOP_K, n), F32),
                   jax.ShapeDtypeStruct((TOP_K, n), jnp.int32),
                   jax.ShapeDtypeStruct((N_EXPERTS, n // tm), F32)],
        scratch_shapes=[pltpu.VMEM((D_MODEL // LANES, tm + 2 * HALO, LANES), F32),
                        pltpu.VMEM((SUBLANES - 1, D_MODEL // LANES, tm + 2 * HALO - SUBLANES, LANES), F32),
                        pltpu.VMEM((tm, D_MODEL), F32)],
        compiler_params=_cparams("arbitrary"),
        name="mix",
    )(x, u, u, u, attn, gc, ga, w_dw, b_dw, ln_g, ln_b, w_pw, b_pw, w_out, g_ffn, w_r_hi, w_r_lo, b_r)


RUN_ROWS = SUBLANES
MM_ROWS = 256


def _tile_slots(tm):
    return -(-(TOP_K * tm + N_EXPERTS * (RUN_ROWS - 1)) // MM_ROWS) * MM_ROWS


PIECE_ROWS = (32, RUN_ROWS)
WAIT_ROWS = (512, 64, RUN_ROWS)


def _rows_copy(src_ref, src_row, dst_ref, dst_row, n_rows, sem):
    rows = lambda r: pl.ds(pl.multiple_of(r, RUN_ROWS), n_rows)
    return pltpu.make_async_copy(src_ref.at[rows(src_row)], dst_ref.at[rows(dst_row)], sem)


def _repeat(n, fn):
    def body(j, c):
        fn(j)
        return c
    lax.fori_loop(0, n, body, 0)


def _start_run_copies(cnt_ref, tile, start):
    def per_expert(e, carry):
        run = (cnt_ref[tile * N_EXPERTS + e] + (RUN_ROWS - 1)) & -RUN_ROWS
        done = 0
        for n_rows in PIECE_ROWS:
            n = lax.shift_right_logical(run - done, n_rows.bit_length() - 1)
            _repeat(n, lambda j, n_rows=n_rows, done=done: start(e, done + j * n_rows, n_rows))
            done = done + n * n_rows
        return carry

    lax.fori_loop(0, N_EXPERTS, per_expert, 0)


def _wait_rows(total_rows, wait):
    done = 0
    for n_rows in WAIT_ROWS:
        n = lax.shift_right_logical(total_rows - done, n_rows.bit_length() - 1)
        _repeat(n, lambda j, n_rows=n_rows: wait(n_rows))
        done = done + n * n_rows


def _dispatch_body(cnt_ref, lstart_ref, gstart_ref, rows_ref, tail_ref, h_ref, lpos_ref, xs_ref, obuf, zbuf, sem,
                   *, tm):
    i = pl.program_id(0)

    lpos = lpos_ref[...]
    hb = h_ref[...]
    for c in range(obuf.shape[0] // MM_ROWS):
        rio = lax.broadcasted_iota(jnp.int32, (MM_ROWS, tm), 0) + c * MM_ROWS
        onehot = (rio == lpos[0:1]) | (rio == lpos[1:2]) | (rio == lpos[2:3]) | (rio == lpos[3:4])
        srt = jnp.dot(onehot.astype(BF16), hb, preferred_element_type=F32)
        lo = lax.bitcast_convert_type(srt[:, :HALF], jnp.uint32)
        hi = lax.bitcast_convert_type(srt[:, HALF:], jnp.uint32)
        obuf[c * MM_ROWS:(c + 1) * MM_ROWS, :] = (lo >> 16) | hi

    _start_run_copies(cnt_ref, i, lambda e, off, n_rows: _rows_copy(
        obuf, lstart_ref[i * N_EXPERTS + e] + off, xs_ref, gstart_ref[i * N_EXPERTS + e] + off, n_rows, sem).start())
    _wait_rows(rows_ref[i], lambda n_rows: _rows_copy(obuf, 0, xs_ref, 0, n_rows, sem).wait())

    @pl.when(i == pl.num_programs(0) - 1)
    def _():
        zbuf[...] = jnp.zeros_like(zbuf)

        def per_expert(e, carry):
            n = tail_ref[N_EXPERTS + e]
            _repeat(n, lambda j: _rows_copy(zbuf, 0, xs_ref, tail_ref[e] + j * RUN_ROWS, RUN_ROWS, sem).start())
            _repeat(n, lambda j: _rows_copy(zbuf, 0, xs_ref, 0, RUN_ROWS, sem).wait())
            return carry

        lax.fori_loop(0, N_EXPERTS, per_expert, 0)


def _dispatch(cnt, lstart, gstart, tile_rows, tail, h, lpos_t, n_rows_out, tm):
    n = h.shape[0]
    grid_spec = pltpu.PrefetchScalarGridSpec(
        num_scalar_prefetch=5,
        grid=(n // tm,),
        in_specs=[pl.BlockSpec((tm, D_MODEL), lambda i, *_: (i, 0)),
                  pl.BlockSpec((TOP_K, tm), lambda i, *_: (0, i))],
        out_specs=pl.BlockSpec(memory_space=pl.ANY),
        scratch_shapes=[pltpu.VMEM((_tile_slots(tm), HALF), jnp.uint32),
                        pltpu.VMEM((RUN_ROWS, HALF), jnp.uint32),
                        pltpu.SemaphoreType.DMA(())],
    )
    return pl.pallas_call(
        functools.partial(_dispatch_body, tm=tm),
        grid_spec=grid_spec,
        out_shape=jax.ShapeDtypeStruct((n_rows_out, HALF), jnp.uint32),
        compiler_params=_cparams("arbitrary"),
        name="dispatch",
    )(cnt, lstart, gstart, tile_rows, tail, h, lpos_t)


def _experts_body(be_ref, nused_ref, xs_ref, wgu_ref, bgu_ref, wd_ref, bd_ref, ys_ref):
    b = pl.program_id(0)

    @pl.when(b < nused_ref[0])
    def _():
        lo, hi = _unpack_rows(xs_ref[...])
        gu = (jnp.dot(lo.astype(BF16), wgu_ref[0, :HALF, :], preferred_element_type=F32)
              + jnp.dot(hi.astype(BF16), wgu_ref[0, HALF:, :], preferred_element_type=F32)
              + bgu_ref[0])
        gate = jnp.minimum(gu[:, :D_FF], SWIGLU_LIMIT)
        up = jnp.clip(gu[:, D_FF:], -SWIGLU_LIMIT, SWIGLU_LIMIT)
        glu = gate * jax.nn.sigmoid(gate * SWIGLU_ALPHA)
        act = ((up + 1.0) * glu).astype(BF16)
        y = jnp.dot(act, wd_ref[0], preferred_element_type=F32) + bd_ref[0]
        ys_ref[...] = _pack_rows(y)

    @pl.when(b >= nused_ref[0])
    def _():
        ys_ref[...] = jnp.zeros_like(ys_ref)


def _experts(block_e, n_used, xs, w_gu, b_gu, w_d, b_d):
    p = xs.shape[0]
    nb = p // MOE_BLOCK

    def xmap(b, be, nu):
        return (jnp.minimum(b, jnp.maximum(nu[0] - 1, 0)), 0)

    def wmap(b, be, nu):
        return (be[b], 0, 0)

    grid_spec = pltpu.PrefetchScalarGridSpec(
        num_scalar_prefetch=2,
        grid=(nb,),
        in_specs=[pl.BlockSpec((MOE_BLOCK, HALF), xmap),
                  pl.BlockSpec((1, D_MODEL, 2 * D_FF), wmap),
                  pl.BlockSpec((1, 1, 2 * D_FF), wmap),
                  pl.BlockSpec((1, D_FF, D_MODEL), wmap),
                  pl.BlockSpec((1, 1, D_MODEL), wmap)],
        out_specs=pl.BlockSpec((MOE_BLOCK, HALF), lambda b, be, nu: (b, 0)),
    )
    return pl.pallas_call(
        _experts_body,
        grid_spec=grid_spec,
        out_shape=jax.ShapeDtypeStruct((p, HALF), jnp.uint32),
        compiler_params=_cparams("arbitrary"),
        name="experts",
    )(block_e, n_used, xs, w_gu, b_gu, w_d, b_d)


def _combine_body(cnt_ref, lstartp_ref, gstart_ref, rows_ref, x2_ref, lposp_ref, wt_ref, g_ref, ys_ref, o_ref,
                  gbuf, sem, *, tm, slots):
    i = pl.program_id(0)

    @pl.when(i == 0)
    def _():
        gbuf[...] = jnp.zeros_like(gbuf)

    _start_run_copies(cnt_ref, i, lambda e, off, n_rows: _rows_copy(
        ys_ref, gstart_ref[i * N_EXPERTS + e] + off, gbuf, lstartp_ref[i * N_EXPERTS + e] + off, n_rows, sem).start())
    _wait_rows(rows_ref[i], lambda n_rows: _rows_copy(ys_ref, 0, gbuf, 0, n_rows, sem).wait())

    lposp = lposp_ref[...]
    wt = wt_ref[...]
    sio = lax.broadcasted_iota(jnp.int32, (tm, slots), 1)
    comb = jnp.zeros((tm, slots), F32)
    for k in range(TOP_K):
        comb = jnp.where(sio == lposp[:, k:k + 1], wt[:, k:k + 1], comb)
    comb = comb.astype(BF16)
    lo, hi = _unpack_rows(gbuf[...])
    moe_lo = jnp.dot(comb, lo.astype(BF16), preferred_element_type=F32)
    moe_hi = jnp.dot(comb, hi.astype(BF16), preferred_element_type=F32)

    x2 = x2_ref[...]
    y_lo = x2[:, :HALF] + moe_lo
    y_hi = x2[:, HALF:] + moe_hi
    ms = (jnp.sum(y_lo * y_lo, axis=-1, keepdims=True) + jnp.sum(y_hi * y_hi, axis=-1, keepdims=True)) / D_MODEL
    inv = lax.rsqrt(ms + RMS_EPS)
    g = g_ref[...]
    o_ref[:, :HALF] = y_lo * inv * g[:, :HALF]
    o_ref[:, HALF:] = y_hi * inv * g[:, HALF:]


def _combine(cnt, lstartp, gstart, tile_rows, x2, lposp_tok, wt_tok, g_final, ys, tm):
    n = x2.shape[0]
    slots = _tile_slots(tm)
    grid_spec = pltpu.PrefetchScalarGridSpec(
        num_scalar_prefetch=4,
        grid=(n // tm,),
        in_specs=[pl.BlockSpec((tm, D_MODEL), lambda i, *_: (i, 0)),
                  pl.BlockSpec((tm, TOP_K), lambda i, *_: (i, 0)),
                  pl.BlockSpec((tm, TOP_K), lambda i, *_: (i, 0)),
                  pl.BlockSpec((1, D_MODEL), lambda i, *_: (0, 0)),
                  pl.BlockSpec(memory_space=pl.ANY)],
        out_specs=pl.BlockSpec((tm, D_MODEL), lambda i, *_: (i, 0)),
        scratch_shapes=[pltpu.VMEM((slots, HALF), jnp.uint32), pltpu.SemaphoreType.DMA(())],
    )
    return pl.pallas_call(
        functools.partial(_combine_body, tm=tm, slots=slots),
        grid_spec=grid_spec,
        out_shape=jax.ShapeDtypeStruct((n, D_MODEL), F32),
        compiler_params=_cparams("arbitrary"),
        name="combine",
    )(cnt, lstartp, gstart, tile_rows, x2, lposp_tok, wt_tok, g_final, ys)


def _rotary_tables(seq):
    half = ROT_DIM // 2
    inv_freq = ROPE_THETA ** (-jnp.arange(half, dtype=F32) * 2.0 / ROT_DIM)
    ang = jnp.arange(seq, dtype=F32)[:, None] * inv_freq[None, :]
    cos, sin = jnp.cos(ang), jnp.sin(ang)
    ones = jnp.ones((seq, HEAD_DIM - ROT_DIM), F32)
    zeros = jnp.zeros((seq, HEAD_DIM - ROT_DIM), F32)
    zh = jnp.zeros((seq, half), F32)
    cos_h = jnp.concatenate([cos, cos, ones], axis=1)
    s1_h = jnp.concatenate([zh, sin, zeros], axis=1)
    s2_h = jnp.concatenate([-sin, zh, zeros], axis=1)
    rep = LANES // HEAD_DIM
    return jnp.tile(cos_h, (1, rep)), jnp.tile(s1_h, (1, rep)), jnp.tile(s2_h, (1, rep))


def _prep_weights(g_mix, w_in, b_in, sinks, w_dw, b_dw, ln_g, ln_b, w_pw, b_pw, w_out,
                  g_ffn, w_router, b_router, w_gate_up, b_gate_up, w_down, b_down, g_final):
    r1 = lambda a: a.reshape(1, -1).astype(F32)
    w_r_t = w_router[0].T.astype(F32)
    w_r_hi = w_r_t.astype(BF16)
    return dict(
        g_mix=r1(g_mix[0]), w_in=w_in[0].astype(BF16), b_in=r1(b_in[0]), sinks=sinks[0].astype(F32),
        w_dw=w_dw[0].reshape(CONV_KERNEL, D_MODEL).astype(F32), b_dw=r1(b_dw[0]),
        ln_g=r1(ln_g[0]), ln_b=r1(ln_b[0]), w_pw=w_pw[0].astype(BF16), b_pw=r1(b_pw[0]),
        w_out=w_out[0].astype(BF16), g_ffn=r1(g_ffn[0]),
        w_r_hi=w_r_hi, w_r_lo=(w_r_t - w_r_hi.astype(F32)).astype(BF16),
        b_r=b_router[0].reshape(N_EXPERTS, 1).astype(F32),
        w_gu=w_gate_up[0].astype(BF16), b_gu=b_gate_up[0].reshape(N_EXPERTS, 1, 2 * D_FF).astype(F32),
        w_d=w_down[0].astype(BF16), b_d=b_down[0].reshape(N_EXPERTS, 1, D_MODEL).astype(F32),
        g_final=r1(g_final),
    )


def _trunk(x, w, tables, tm_proj, tq, tm_mix):
    b, s, d = x.shape
    n = b * s
    xf = x.reshape(n, d)
    u, q, k, v, gc, ga = _inproj(xf, w["g_mix"], w["w_in"], w["b_in"], *tables, seq=s, tm=tm_proj)
    attn = _attn(q.reshape(b, s, d), k.reshape(b, s, KV_COLS), v.reshape(b, s, KV_COLS), w["sinks"], tq)
    x2, h, wt_t, lposp_t, cnt = _mix(
        xf, u, attn.reshape(n, d), gc, ga, w["w_dw"], w["b_dw"], w["ln_g"], w["ln_b"], w["w_pw"], w["b_pw"],
        w["w_out"], w["g_ffn"], w["w_r_hi"], w["w_r_lo"], w["b_r"], seq=s, tm=tm_mix)

    n_tiles = n // tm_mix
    cnt_te = cnt.T.astype(jnp.int32)
    run_te = (cnt_te + RUN_ROWS - 1) // RUN_ROWS * RUN_ROWS
    total = jnp.sum(run_te, axis=0)
    padded = (total + MOE_BLOCK - 1) // MOE_BLOCK * MOE_BLOCK
    pend = jnp.cumsum(padded)
    pstart = pend - padded
    gstart = pstart[None, :] + jnp.cumsum(run_te, axis=0) - run_te
    lstart = jnp.cumsum(run_te, axis=1) - run_te
    flat = lambda a: a.reshape(-1).astype(jnp.int32)
    max_rows = n * TOP_K + n_tiles * N_EXPERTS * (RUN_ROWS - 1) + N_EXPERTS * (MOE_BLOCK - 1)
    n_blocks = -(-max_rows // MOE_BLOCK)
    block_row = jnp.arange(n_blocks, dtype=jnp.int32) * MOE_BLOCK
    block_e = jnp.minimum(jnp.sum((pend[None, :] <= block_row[:, None]).astype(jnp.int32), axis=1), N_EXPERTS - 1)
    n_used = (pend[-1:] // MOE_BLOCK).astype(jnp.int32)

    tile_rows = flat(jnp.sum(run_te, axis=1))
    tail = flat(jnp.concatenate([pstart + total, (padded - total) // RUN_ROWS]))
    xs = _dispatch(flat(cnt_te), flat(lstart), flat(gstart), tile_rows, tail, h, lposp_t,
                   n_blocks * MOE_BLOCK, tm_mix)
    ys = _experts(block_e, n_used, xs, w["w_gu"], w["b_gu"], w["w_d"], w["b_d"])
    out = _combine(flat(cnt_te), flat(lstart), flat(gstart), tile_rows, x2, lposp_t.T, wt_t.T, w["g_final"], ys,
                   tm_mix)
    return out.reshape(b, s, d)


def kernel(x_prompt, x_sample, g_mix, w_in, b_in, sinks, w_dw, b_dw, ln_g, ln_b, w_pw, b_pw, w_out, g_ffn,
           w_router, b_router, w_gate_up, b_gate_up, w_down, b_down, g_final):
    w = _prep_weights(g_mix, w_in, b_in, sinks, w_dw, b_dw, ln_g, ln_b, w_pw, b_pw, w_out, g_ffn,
                      w_router, b_router, w_gate_up, b_gate_up, w_down, b_down, g_final)
    outs = []
    for x in (x_prompt, x_sample):
        s = x.shape[1]
        tables = _rotary_tables(s)
        outs.append(_trunk(x, w, tables, tm_proj=min(512, s), tq=min(512, s), tm_mix=min(256, s)))
    return tuple(outs)
```

```python
import functools

import jax
import jax.numpy as jnp
from jax import lax
from jax.experimental import pallas as pl
from jax.experimental.pallas import tpu as pltpu

D_MODEL = 1024
N_Q_HEADS = 16
N_KV_HEADS = 2
HEAD_DIM = 64
GROUP = N_Q_HEADS // N_KV_HEADS
ROT_DIM = HEAD_DIM // 4
ROPE_THETA = 500000.0
WINDOW = 128
ATTN_SCALE = HEAD_DIM ** -0.5
CONV_KERNEL = 31
CONV_PAD = CONV_KERNEL // 2
N_EXPERTS = 32
TOP_K = 4
D_FF = D_MODEL
SWIGLU_LIMIT = 7.0
SWIGLU_ALPHA = 1.702
MOE_BLOCK = 512
RMS_EPS = 1e-5
LN_EPS = 1e-5

KV_COLS = N_KV_HEADS * HEAD_DIM
LANES = 128
SUBLANES = 8
HALF = D_MODEL // 2
HALO = 16
NEG_BIG = -1e30
ATTN_LOOKAHEAD = 2

VMEM_LIMIT = 56 * 1024 * 1024

F32 = jnp.float32
BF16 = jnp.bfloat16


def _cparams(*sem):
    return pltpu.CompilerParams(dimension_semantics=sem, vmem_limit_bytes=VMEM_LIMIT)


def _pack_rows(x):
    lo = lax.bitcast_convert_type(x[:, :HALF].astype(BF16).astype(F32), jnp.uint32)
    hi = lax.bitcast_convert_type(x[:, HALF:].astype(BF16).astype(F32), jnp.uint32)
    return (lo >> 16) | (hi & jnp.uint32(0xFFFF0000))


def _unpack_rows(p):
    lo = lax.bitcast_convert_type(p << 16, F32)
    hi = lax.bitcast_convert_type(p & jnp.uint32(0xFFFF0000), F32)
    return lo, hi


def _inproj_body(x_ref, g_ref, w_ref, b_ref, cos_ref, s1_ref, s2_ref,
                 u_ref, q_ref, k_ref, v_ref, gc_ref, ga_ref):
    x = x_ref[...]
    xn = x * lax.rsqrt(jnp.mean(x * x, axis=-1, keepdims=True) + RMS_EPS) * g_ref[...]
    xb = xn.astype(BF16)

    def proj(c0, c1):
        return jnp.dot(xb, w_ref[:, c0:c1], preferred_element_type=F32) + b_ref[:, c0:c1]

    cos = cos_ref[...]
    s1 = s1_ref[...]
    s2 = s2_ref[...]

    def rotary(t):
        half = ROT_DIM // 2
        outs = []
        for c in range(t.shape[1] // LANES):
            tc = t[:, c * LANES:(c + 1) * LANES]
            outs.append(tc * cos + pltpu.roll(tc, half, 1) * s1 + pltpu.roll(tc, LANES - half, 1) * s2)
        return outs[0] if len(outs) == 1 else jnp.concatenate(outs, axis=1)

    d = D_MODEL
    a = proj(0, d)
    gate = proj(d, 2 * d)
    u_ref[...] = (a * jax.nn.sigmoid(gate)).astype(BF16)
    q = proj(2 * d, 3 * d)
    q_ref[...] = (rotary(q) * ATTN_SCALE).astype(BF16)
    kv = proj(3 * d, 3 * d + 2 * KV_COLS)
    k_ref[...] = rotary(kv[:, :KV_COLS]).astype(BF16)
    v_ref[...] = kv[:, KV_COLS:].astype(BF16)
    c0 = 3 * d + 2 * KV_COLS
    gc_ref[...] = jax.nn.sigmoid(proj(c0, c0 + d)).astype(BF16)
    ga_ref[...] = jax.nn.sigmoid(proj(c0 + d, c0 + 2 * d)).astype(BF16)


def _inproj(x, g_mix, w_in, b_in, cos_t, s1_t, s2_t, seq, tm):
    n = x.shape[0]
    in_cols = w_in.shape[1]
    spt = seq // tm
    row = lambda w: pl.BlockSpec((tm, w), lambda i: (i, 0))
    const = lambda r, c: pl.BlockSpec((r, c), lambda i: (0, 0))
    tab = pl.BlockSpec((tm, LANES), lambda i: (i % spt, 0))
    big = jax.ShapeDtypeStruct((n, D_MODEL), BF16)
    small = jax.ShapeDtypeStruct((n, KV_COLS), BF16)
    return pl.pallas_call(
        _inproj_body,
        grid=(n // tm,),
        in_specs=[row(D_MODEL), const(1, D_MODEL), const(D_MODEL, in_cols), const(1, in_cols), tab, tab, tab],
        out_specs=[row(D_MODEL), row(D_MODEL), row(KV_COLS), row(KV_COLS), row(D_MODEL), row(D_MODEL)],
        out_shape=[big, big, small, small, big, big],
        compiler_params=_cparams("arbitrary"),
        name="inproj",
    )(x, g_mix, w_in, b_in, cos_t, s1_t, s2_t)


def _attn_body(q_ref, kp_ref, kc_ref, kn_ref, vp_ref, vc_ref, vn_ref, sink_ref, o_ref, kbuf, vbuf,
               *, tq, nblk):
    i = pl.program_id(1)
    blk = WINDOW
    kbuf[0:blk, :] = kp_ref[0]
    kbuf[blk:blk + tq, :] = kc_ref[0]
    kbuf[blk + tq:, :] = kn_ref[0]
    vbuf[0:blk, :] = vp_ref[0]
    vbuf[blk:blk + tq, :] = vc_ref[0]
    vbuf[blk + tq:, :] = vn_ref[0]

    qi = lax.broadcasted_iota(jnp.int32, (blk, 3 * blk), 0)
    kj = lax.broadcasted_iota(jnp.int32, (blk, 3 * blk), 1)
    band = jnp.abs(qi + blk - kj) <= WINDOW

    def scores(j, h):
        g = h // GROUP
        qh = q_ref[0, j * blk:(j + 1) * blk, h * HEAD_DIM:(h + 1) * HEAD_DIM]
        kh = kbuf[j * blk:(j + 3) * blk, g * HEAD_DIM:(g + 1) * HEAD_DIM]
        return lax.dot_general(qh, kh, (((1,), (1,)), ((), ())), preferred_element_type=F32)

    units = [(j, h) for j in range(tq // blk) for h in range(N_Q_HEADS)]
    pending = [scores(*u) for u in units[:ATTN_LOOKAHEAD]]
    outs = []
    for idx, (j, h) in enumerate(units):
        if idx + ATTN_LOOKAHEAD < len(units):
            pending.append(scores(*units[idx + ATTN_LOOKAHEAD]))
        s = pending.pop(0)
        n = i * (tq // blk) + j
        valid = band & ((kj >= blk) | (n > 0)) & ((kj < 2 * blk) | (n < nblk - 1))
        g = h // GROUP
        vh = vbuf[j * blk:(j + 3) * blk, g * HEAD_DIM:(g + 1) * HEAD_DIM]
        s = jnp.where(valid, s, NEG_BIG)
        sink = sink_ref[h]
        m = jnp.maximum(jnp.max(s, axis=-1, keepdims=True), sink)
        p = jnp.exp(s - m)
        denom = jnp.sum(p, axis=-1, keepdims=True) + jnp.exp(sink - m)
        o = jnp.dot(p.astype(BF16), vh, preferred_element_type=F32)
        outs.append(o / denom)
        if h == N_Q_HEADS - 1:
            o_ref[0, j * blk:(j + 1) * blk, :] = jnp.concatenate(outs, axis=1).astype(BF16)
            outs = []


def _attn(q, k, v, sinks, tq):
    b, s, _ = q.shape
    r = tq // WINDOW
    nblk = s // WINDOW
    cur = lambda w: pl.BlockSpec((1, tq, w), lambda bi, i: (bi, i, 0))
    prev = pl.BlockSpec((1, WINDOW, KV_COLS), lambda bi, i: (bi, jnp.maximum(i * r - 1, 0), 0))
    nxt = pl.BlockSpec((1, WINDOW, KV_COLS), lambda bi, i: (bi, jnp.minimum((i + 1) * r, nblk - 1), 0))
    return pl.pallas_call(
        functools.partial(_attn_body, tq=tq, nblk=nblk),
        grid=(b, s // tq),
        in_specs=[cur(D_MODEL), prev, cur(KV_COLS), nxt, prev, cur(KV_COLS), nxt,
                  pl.BlockSpec(memory_space=pltpu.SMEM)],
        out_specs=cur(D_MODEL),
        out_shape=jax.ShapeDtypeStruct((b, s, D_MODEL), BF16),
        scratch_shapes=[pltpu.VMEM((tq + 2 * WINDOW, KV_COLS), BF16),
                        pltpu.VMEM((tq + 2 * WINDOW, KV_COLS), BF16)],
        compiler_params=_cparams("arbitrary", "arbitrary"),
        name="attn",
    )(q, k, k, k, v, v, v, sinks)


CONV_ROWS = 64


def _mix_body(x_ref, up_ref, uc_ref, un_ref, at_ref, gc_ref, ga_ref,
              wdw_ref, bdw_ref, lng_ref, lnb_ref, wpw_ref, bpw_ref, wout_ref,
              gffn_ref, wrh_ref, wrl_ref, br_ref,
              x2_ref, h_ref, wt_ref, lposp_ref, cnt_ref,
              ubuf, sbuf, ybuf, *, tm, spt):
    i = pl.program_id(0)
    first = (i % spt) == 0
    last = (i % spt) == spt - 1

    @pl.when(i == 0)
    def _():
        cnt_ref[...] = jnp.zeros_like(cnt_ref)

    for c in range(D_MODEL // LANES):
        cs = slice(c * LANES, (c + 1) * LANES)
        ubuf[c, 0:HALO, :] = jnp.where(first, 0.0, up_ref[:, cs].astype(F32))
        ubuf[c, HALO:HALO + tm, :] = uc_ref[:, cs].astype(F32)
        ubuf[c, HALO + tm:, :] = jnp.where(last, 0.0, un_ref[:, cs].astype(F32))

    def shift_rows(r0, nrows):
        for c in range(D_MODEL // LANES):
            win = ubuf[c, pl.ds(r0, nrows + SUBLANES), :]
            for s in range(1, SUBLANES):
                sbuf[s - 1, c, pl.ds(r0, nrows), :] = win[s:s + nrows]

    def shift_chunk(r, carry):
        shift_rows(pl.multiple_of(r * CONV_ROWS, CONV_ROWS), CONV_ROWS)
        return carry

    lax.fori_loop(0, tm // CONV_ROWS, shift_chunk, 0)
    shift_rows(tm, 2 * HALO - SUBLANES)

    base = HALO - CONV_PAD

    for c in range(D_MODEL // LANES):
        cs = slice(c * LANES, (c + 1) * LANES)
        wts = [jnp.broadcast_to(wdw_ref[t:t + 1, cs], (SUBLANES, LANES)) for t in range(CONV_KERNEL)]
        bias = jnp.broadcast_to(bdw_ref[:, cs], (SUBLANES, LANES))

        def conv_rows(r, carry, c=c, cs=cs, wts=wts, bias=bias):
            r0 = pl.multiple_of(r * CONV_ROWS, CONV_ROWS)
            accs = [bias] * (CONV_ROWS // SUBLANES)
            for t in range(CONV_KERNEL):
                a, s = divmod(base + t, SUBLANES)
                rows = pl.ds(r0 + a * SUBLANES, CONV_ROWS)
                tap = ubuf[c, rows, :] if s == 0 else sbuf[s - 1, c, rows, :]
                accs = [acc + tap[k * SUBLANES:(k + 1) * SUBLANES] * wts[t] for k, acc in enumerate(accs)]
            ybuf[pl.ds(r0, CONV_ROWS), cs] = jnp.concatenate(accs, axis=0)
            return carry

        lax.fori_loop(0, tm // CONV_ROWS, conv_rows, 0)

    y = ybuf[...]
    mu = jnp.mean(y, axis=-1, keepdims=True)
    yc = y - mu
    var = jnp.mean(yc * yc, axis=-1, keepdims=True)
    yn = yc * lax.rsqrt(var + LN_EPS) * lng_ref[...] + lnb_ref[...]
    act = yn * jax.nn.sigmoid(yn)
    conv = jnp.dot(act.astype(BF16), wpw_ref[...], preferred_element_type=F32) + bpw_ref[...]

    merged = gc_ref[...] * conv.astype(BF16) + ga_ref[...] * at_ref[...]
    x2 = x_ref[...] + jnp.dot(merged, wout_ref[...], preferred_element_type=F32)
    x2_ref[...] = x2

    h = x2 * lax.rsqrt(jnp.mean(x2 * x2, axis=-1, keepdims=True) + RMS_EPS) * gffn_ref[...]
    h_hi = h.astype(BF16)
    h_ref[...] = h_hi

    h_lo = (h - h_hi.astype(F32)).astype(BF16)
    nt = lambda a, b: lax.dot_general(a, b, (((1,), (1,)), ((), ())), preferred_element_type=F32)
    logits = nt(wrh_ref[...], h_hi) + (nt(wrh_ref[...], h_lo) + nt(wrl_ref[...], h_hi)) + br_ref[...]
    eio = lax.broadcasted_iota(jnp.int32, (N_EXPERTS, tm), 0).astype(F32)
    work = logits
    sels, tops = [], []
    for _ in range(TOP_K):
        m = jnp.max(work, axis=0, keepdims=True)
        idx = jnp.min(jnp.where(work == m, eio, float(N_EXPERTS)), axis=0, keepdims=True)
        sel = eio == idx
        work = jnp.where(sel, -jnp.inf, work)
        sels.append(sel)
        tops.append(m)
    es = [jnp.exp(t - tops[0]) for t in tops]
    tot = es[0] + es[1] + es[2] + es[3]
    wt_ref[...] = jnp.concatenate([e / tot for e in es], axis=0)

    multihot = (sels[0] | sels[1] | sels[2] | sels[3])
    ti = lax.broadcasted_iota(jnp.int32, (tm, tm), 0)
    tj = lax.broadcasted_iota(jnp.int32, (tm, tm), 1)
    before = (ti < tj).astype(BF16)
    prefix = jnp.dot(multihot.astype(BF16), before, preferred_element_type=F32)
    cnt = jnp.sum(multihot.astype(F32), axis=1, keepdims=True)
    cnt_slot = jnp.floor((cnt + (SUBLANES - 1)) * (1.0 / SUBLANES)) * SUBLANES
    ei = lax.broadcasted_iota(jnp.int32, (N_EXPERTS, N_EXPERTS), 0)
    ej = lax.broadcasted_iota(jnp.int32, (N_EXPERTS, N_EXPERTS), 1)
    earlier = (ej < ei).astype(BF16)

    run_start = jnp.dot(earlier, jnp.broadcast_to(cnt_slot, (N_EXPERTS, LANES)).astype(BF16),
                        preferred_element_type=F32)[:, :1]
    pos = prefix + run_start
    lposp_ref[...] = jnp.concatenate(
        [jnp.sum(jnp.where(s, pos, 0.0), axis=0, keepdims=True) for s in sels], axis=0).astype(jnp.int32)
    tile_lane = lax.broadcasted_iota(jnp.int32, cnt_ref.shape, 1)
    cnt_ref[...] = jnp.where(tile_lane == i, cnt, cnt_ref[...])


def _mix(x, u, attn, gc, ga, w_dw, b_dw, ln_g, ln_b, w_pw, b_pw, w_out, g_ffn, w_r_hi, w_r_lo, b_r, seq, tm):
    n = x.shape[0]
    spt = seq // tm
    hb = tm // HALO
    nh = n // HALO
    row = lambda w: pl.BlockSpec((tm, w), lambda i: (i, 0))
    const = lambda r, c: pl.BlockSpec((r, c), lambda i: (0, 0))
    halo_prev = pl.BlockSpec((HALO, D_MODEL), lambda i: (jnp.maximum(i * hb - 1, 0), 0))
    halo_next = pl.BlockSpec((HALO, D_MODEL), lambda i: (jnp.minimum((i + 1) * hb, nh - 1), 0))
    tok = pl.BlockSpec((TOP_K, tm), lambda i: (0, i))
    return pl.pallas_call(
        functools.partial(_mix_body, tm=tm, spt=spt),
        grid=(n // tm,),
        in_specs=[row(D_MODEL), halo_prev, row(D_MODEL), halo_next, row(D_MODEL), row(D_MODEL), row(D_MODEL),
                  const(CONV_KERNEL, D_MODEL), const(1, D_MODEL), const(1, D_MODEL), const(1, D_MODEL),
                  const(D_MODEL, D_MODEL), const(1, D_MODEL), const(D_MODEL, D_MODEL),
                  const(1, D_MODEL), const(N_EXPERTS, D_MODEL), const(N_EXPERTS, D_MODEL), const(N_EXPERTS, 1)],
        out_specs=[row(D_MODEL), row(D_MODEL), tok, tok, const(N_EXPERTS, n // tm)],
        out_shape=[jax.ShapeDtypeStruct((n, D_MODEL), F32),
                   jax.ShapeDtypeStruct((n, D_MODEL), BF16),
                   jax.ShapeDtypeStruct((TOP_K, n), F32),
                   jax.ShapeDtypeStruct((TOP_K, n), jnp.int32),
                   jax.ShapeDtypeStruct((N_EXPERTS, n // tm), F32)],
        scratch_shapes=[pltpu.VMEM((D_MODEL // LANES, tm + 2 * HALO, LANES), F32),
                        pltpu.VMEM((SUBLANES - 1, D_MODEL // LANES, tm + 2 * HALO - SUBLANES, LANES), F32),
                        pltpu.VMEM((tm, D_MODEL), F32)],
        compiler_params=_cparams("arbitrary"),
        name="mix",
    )(x, u, u, u, attn, gc, ga, w_dw, b_dw, ln_g, ln_b, w_pw, b_pw, w_out, g_ffn, w_r_hi, w_r_lo, b_r)


RUN_ROWS = SUBLANES
MM_ROWS = 256


def _tile_slots(tm):
    return -(-(TOP_K * tm + N_EXPERTS * (RUN_ROWS - 1)) // MM_ROWS) * MM_ROWS


PIECE_ROWS = (32, RUN_ROWS)
WAIT_ROWS = (512, 64, RUN_ROWS)


def _rows_copy(src_ref, src_row, dst_ref, dst_row, n_rows, sem):
    rows = lambda r: pl.ds(pl.multiple_of(r, RUN_ROWS), n_rows)
    return pltpu.make_async_copy(src_ref.at[rows(src_row)], dst_ref.at[rows(dst_row)], sem)


def _repeat(n, fn):
    def body(j, c):
        fn(j)
        return c
    lax.fori_loop(0, n, body, 0)


def _start_run_copies(cnt_ref, tile, start):
    def per_expert(e, carry):
        run = (cnt_ref[tile * N_EXPERTS + e] + (RUN_ROWS - 1)) & -RUN_ROWS
        done = 0
        for n_rows in PIECE_ROWS:
            n = lax.shift_right_logical(run - done, n_rows.bit_length() - 1)
            _repeat(n, lambda j, n_rows=n_rows, done=done: start(e, done + j * n_rows, n_rows))
            done = done + n * n_rows
        return carry

    lax.fori_loop(0, N_EXPERTS, per_expert, 0)


def _wait_rows(total_rows, wait):
    done = 0
    for n_rows in WAIT_ROWS:
        n = lax.shift_right_logical(total_rows - done, n_rows.bit_length() - 1)
        _repeat(n, lambda j, n_rows=n_rows: wait(n_rows))
        done = done + n * n_rows


def _dispatch_body(cnt_ref, lstart_ref, gstart_ref, rows_ref, tail_ref, h_ref, lpos_ref, xs_ref, obuf, zbuf, sem,
                   *, tm):
    i = pl.program_id(0)
    slot = i % 2
    buf, other = obuf.at[slot], obuf.at[1 - slot]

    lpos = lpos_ref[...]
    hb = h_ref[...]
    for c in range(obuf.shape[1] // MM_ROWS):
        rio = lax.broadcasted_iota(jnp.int32, (MM_ROWS, tm), 0) + c * MM_ROWS
        onehot = (rio == lpos[0:1]) | (rio == lpos[1:2]) | (rio == lpos[2:3]) | (rio == lpos[3:4])
        srt = jnp.dot(onehot.astype(BF16), hb, preferred_element_type=F32)
        lo = lax.bitcast_convert_type(srt[:, :HALF], jnp.uint32)
        hi = lax.bitcast_convert_type(srt[:, HALF:], jnp.uint32)
        buf[c * MM_ROWS:(c + 1) * MM_ROWS, :] = (lo >> 16) | hi

    _start_run_copies(cnt_ref, i, lambda e, off, n_rows: _rows_copy(
        buf, lstart_ref[i * N_EXPERTS + e] + off, xs_ref, gstart_ref[i * N_EXPERTS + e] + off, n_rows,
        sem.at[slot]).start())

    @pl.when(i > 0)
    def _():
        _wait_rows(rows_ref[i - 1], lambda n_rows: _rows_copy(other, 0, xs_ref, 0, n_rows, sem.at[1 - slot]).wait())

    @pl.when(i == pl.num_programs(0) - 1)
    def _():
        _wait_rows(rows_ref[i], lambda n_rows: _rows_copy(buf, 0, xs_ref, 0, n_rows, sem.at[slot]).wait())

        zbuf[...] = jnp.zeros_like(zbuf)

        def per_expert(e, carry):
            n = tail_ref[N_EXPERTS + e]
            _repeat(n, lambda j: _rows_copy(zbuf, 0, xs_ref, tail_ref[e] + j * RUN_ROWS, RUN_ROWS,
                                            sem.at[slot]).start())
            _repeat(n, lambda j: _rows_copy(zbuf, 0, xs_ref, 0, RUN_ROWS, sem.at[slot]).wait())
            return carry

        lax.fori_loop(0, N_EXPERTS, per_expert, 0)


def _dispatch(cnt, lstart, gstart, tile_rows, tail, h, lpos_t, n_rows_out, tm):
    n = h.shape[0]
    grid_spec = pltpu.PrefetchScalarGridSpec(
        num_scalar_prefetch=5,
        grid=(n // tm,),
        in_specs=[pl.BlockSpec((tm, D_MODEL), lambda i, *_: (i, 0)),
                  pl.BlockSpec((TOP_K, tm), lambda i, *_: (0, i))],
        out_specs=pl.BlockSpec(memory_space=pl.ANY),
        scratch_shapes=[pltpu.VMEM((2, _tile_slots(tm), HALF), jnp.uint32),
                        pltpu.VMEM((RUN_ROWS, HALF), jnp.uint32),
                        pltpu.SemaphoreType.DMA((2,))],
    )
    return pl.pallas_call(
        functools.partial(_dispatch_body, tm=tm),
        grid_spec=grid_spec,
        out_shape=jax.ShapeDtypeStruct((n_rows_out, HALF), jnp.uint32),
        compiler_params=_cparams("arbitrary"),
        name="dispatch",
    )(cnt, lstart, gstart, tile_rows, tail, h, lpos_t)


CAST_ROWS = 128


def _experts_body(be_ref, nused_ref, xs_ref, wgu_ref, bgu_ref, wd_ref, bd_ref, ys_ref, wgu_bf, wd_bf):
    b = pl.program_id(0)
    used = b < nused_ref[0]

    @pl.when(used & ((b == 0) | (be_ref[b] != be_ref[jnp.maximum(b - 1, 0)])))
    def _():
        def cast(r, carry):
            rows = pl.ds(pl.multiple_of(r * CAST_ROWS, CAST_ROWS), CAST_ROWS)
            wgu_bf[rows, :] = wgu_ref[0, rows, :].astype(BF16)
            wd_bf[rows, :] = wd_ref[0, rows, :].astype(BF16)
            return carry

        lax.fori_loop(0, D_MODEL // CAST_ROWS, cast, 0)

    @pl.when(used)
    def _():
        lo, hi = _unpack_rows(xs_ref[...])
        gu = (jnp.dot(lo.astype(BF16), wgu_bf[:HALF, :], preferred_element_type=F32)
              + jnp.dot(hi.astype(BF16), wgu_bf[HALF:, :], preferred_element_type=F32)
              + bgu_ref[0])
        gate = jnp.minimum(gu[:, :D_FF], SWIGLU_LIMIT)
        up = jnp.clip(gu[:, D_FF:], -SWIGLU_LIMIT, SWIGLU_LIMIT)
        glu = gate * jax.nn.sigmoid(gate * SWIGLU_ALPHA)
        act = ((up + 1.0) * glu).astype(BF16)
        y = jnp.dot(act, wd_bf[...], preferred_element_type=F32) + bd_ref[0]
        ys_ref[...] = _pack_rows(y)

    @pl.when(b >= nused_ref[0])
    def _():
        ys_ref[...] = jnp.zeros_like(ys_ref)


def _experts(block_e, n_used, xs, w_gu, b_gu, w_d, b_d):
    p = xs.shape[0]
    nb = p // MOE_BLOCK

    def xmap(b, be, nu):
        return (jnp.minimum(b, jnp.maximum(nu[0] - 1, 0)), 0)

    def wmap(b, be, nu):
        return (be[b], 0, 0)

    grid_spec = pltpu.PrefetchScalarGridSpec(
        num_scalar_prefetch=2,
        grid=(nb,),
        in_specs=[pl.BlockSpec((MOE_BLOCK, HALF), xmap),
                  pl.BlockSpec((1, D_MODEL, 2 * D_FF), wmap),
                  pl.BlockSpec((1, 1, 2 * D_FF), wmap),
                  pl.BlockSpec((1, D_FF, D_MODEL), wmap),
                  pl.BlockSpec((1, 1, D_MODEL), wmap)],
        out_specs=pl.BlockSpec((MOE_BLOCK, HALF), lambda b, be, nu: (b, 0)),
        scratch_shapes=[pltpu.VMEM((D_MODEL, 2 * D_FF), BF16), pltpu.VMEM((D_FF, D_MODEL), BF16)],
    )
    return pl.pallas_call(
        _experts_body,
        grid_spec=grid_spec,
        out_shape=jax.ShapeDtypeStruct((p, HALF), jnp.uint32),
        compiler_params=_cparams("arbitrary"),
        name="experts",
    )(block_e, n_used, xs, w_gu, b_gu, w_d, b_d)


def _combine_body(cnt_ref, lstartp_ref, gstart_ref, rows_ref, x2_ref, lposp_ref, wt_ref, g_ref, ys_ref, o_ref,
                  gbuf, sem, *, tm, slots):
    i = pl.program_id(0)
    slot = i % 2

    def fetch(tile, s):
        _start_run_copies(cnt_ref, tile, lambda e, off, n_rows: _rows_copy(
            ys_ref, gstart_ref[tile * N_EXPERTS + e] + off, gbuf.at[s], lstartp_ref[tile * N_EXPERTS + e] + off,
            n_rows, sem.at[s]).start())

    @pl.when(i == 0)
    def _():
        gbuf[...] = jnp.zeros_like(gbuf)
        fetch(i, slot)

    @pl.when(i + 1 < pl.num_programs(0))
    def _():
        fetch(i + 1, 1 - slot)

    _wait_rows(rows_ref[i], lambda n_rows: _rows_copy(ys_ref, 0, gbuf.at[slot], 0, n_rows, sem.at[slot]).wait())

    lposp = lposp_ref[...]
    wt = wt_ref[...]
    sio = lax.broadcasted_iota(jnp.int32, (tm, slots), 1)
    comb = jnp.zeros((tm, slots), F32)
    for k in range(TOP_K):
        comb = jnp.where(sio == lposp[:, k:k + 1], wt[:, k:k + 1], comb)
    comb = comb.astype(BF16)
    lo, hi = _unpack_rows(gbuf[slot])
    moe_lo = jnp.dot(comb, lo.astype(BF16), preferred_element_type=F32)
    moe_hi = jnp.dot(comb, hi.astype(BF16), preferred_element_type=F32)

    x2 = x2_ref[...]
    y_lo = x2[:, :HALF] + moe_lo
    y_hi = x2[:, HALF:] + moe_hi
    ms = (jnp.sum(y_lo * y_lo, axis=-1, keepdims=True) + jnp.sum(y_hi * y_hi, axis=-1, keepdims=True)) / D_MODEL
    inv = lax.rsqrt(ms + RMS_EPS)
    g = g_ref[...]
    o_ref[:, :HALF] = y_lo * inv * g[:, :HALF]
    o_ref[:, HALF:] = y_hi * inv * g[:, HALF:]


def _combine(cnt, lstartp, gstart, tile_rows, x2, lposp_tok, wt_tok, g_final, ys, tm):
    n = x2.shape[0]
    slots = _tile_slots(tm)
    grid_spec = pltpu.PrefetchScalarGridSpec(
        num_scalar_prefetch=4,
        grid=(n // tm,),
        in_specs=[pl.BlockSpec((tm, D_MODEL), lambda i, *_: (i, 0)),
                  pl.BlockSpec((tm, TOP_K), lambda i, *_: (i, 0)),
                  pl.BlockSpec((tm, TOP_K), lambda i, *_: (i, 0)),
                  pl.BlockSpec((1, D_MODEL), lambda i, *_: (0, 0)),
                  pl.BlockSpec(memory_space=pl.ANY)],
        out_specs=pl.BlockSpec((tm, D_MODEL), lambda i, *_: (i, 0)),
        scratch_shapes=[pltpu.VMEM((2, slots, HALF), jnp.uint32), pltpu.SemaphoreType.DMA((2,))],
    )
    return pl.pallas_call(
        functools.partial(_combine_body, tm=tm, slots=slots),
        grid_spec=grid_spec,
        out_shape=jax.ShapeDtypeStruct((n, D_MODEL), F32),
        compiler_params=_cparams("arbitrary"),
        name="combine",
    )(cnt, lstartp, gstart, tile_rows, x2, lposp_tok, wt_tok, g_final, ys)


def _rotary_tables(seq):
    half = ROT_DIM // 2
    inv_freq = ROPE_THETA ** (-jnp.arange(half, dtype=F32) * 2.0 / ROT_DIM)
    ang = jnp.arange(seq, dtype=F32)[:, None] * inv_freq[None, :]
    cos, sin = jnp.cos(ang), jnp.sin(ang)
    ones = jnp.ones((seq, HEAD_DIM - ROT_DIM), F32)
    zeros = jnp.zeros((seq, HEAD_DIM - ROT_DIM), F32)
    zh = jnp.zeros((seq, half), F32)
    cos_h = jnp.concatenate([cos, cos, ones], axis=1)
    s1_h = jnp.concatenate([zh, sin, zeros], axis=1)
    s2_h = jnp.concatenate([-sin, zh, zeros], axis=1)
    rep = LANES // HEAD_DIM
    return jnp.tile(cos_h, (1, rep)), jnp.tile(s1_h, (1, rep)), jnp.tile(s2_h, (1, rep))


def _prep_weights(g_mix, w_in, b_in, sinks, w_dw, b_dw, ln_g, ln_b, w_pw, b_pw, w_out,
                  g_ffn, w_router, b_router, w_gate_up, b_gate_up, w_down, b_down, g_final):
    r1 = lambda a: a.reshape(1, -1).astype(F32)
    w_r_t = w_router[0].T.astype(F32)
    w_r_hi = w_r_t.astype(BF16)
    return dict(
        g_mix=r1(g_mix[0]), w_in=w_in[0].astype(BF16), b_in=r1(b_in[0]), sinks=sinks[0].astype(F32),
        w_dw=w_dw[0].reshape(CONV_KERNEL, D_MODEL).astype(F32), b_dw=r1(b_dw[0]),
        ln_g=r1(ln_g[0]), ln_b=r1(ln_b[0]), w_pw=w_pw[0].astype(BF16), b_pw=r1(b_pw[0]),
        w_out=w_out[0].astype(BF16), g_ffn=r1(g_ffn[0]),
        w_r_hi=w_r_hi, w_r_lo=(w_r_t - w_r_hi.astype(F32)).astype(BF16),
        b_r=b_router[0].reshape(N_EXPERTS, 1).astype(F32),
        w_gu=w_gate_up[0].astype(F32), b_gu=b_gate_up[0].reshape(N_EXPERTS, 1, 2 * D_FF).astype(F32),
        w_d=w_down[0].astype(F32), b_d=b_down[0].reshape(N_EXPERTS, 1, D_MODEL).astype(F32),
        g_final=r1(g_final),
    )


def _trunk(x, w, tables, tm_proj, tq, tm_mix):
    b, s, d = x.shape
    n = b * s
    xf = x.reshape(n, d)
    u, q, k, v, gc, ga = _inproj(xf, w["g_mix"], w["w_in"], w["b_in"], *tables, seq=s, tm=tm_proj)
    attn = _attn(q.reshape(b, s, d), k.reshape(b, s, KV_COLS), v.reshape(b, s, KV_COLS), w["sinks"], tq)
    x2, h, wt_t, lposp_t, cnt = _mix(
        xf, u, attn.reshape(n, d), gc, ga, w["w_dw"], w["b_dw"], w["ln_g"], w["ln_b"], w["w_pw"], w["b_pw"],
        w["w_out"], w["g_ffn"], w["w_r_hi"], w["w_r_lo"], w["b_r"], seq=s, tm=tm_mix)

    n_tiles = n // tm_mix
    cnt_te = cnt.T.astype(jnp.int32)
    run_te = (cnt_te + RUN_ROWS - 1) // RUN_ROWS * RUN_ROWS
    total = jnp.sum(run_te, axis=0)
    padded = (total + MOE_BLOCK - 1) // MOE_BLOCK * MOE_BLOCK
    pend = jnp.cumsum(padded)
    pstart = pend - padded
    gstart = pstart[None, :] + jnp.cumsum(run_te, axis=0) - run_te
    lstart = jnp.cumsum(run_te, axis=1) - run_te
    flat = lambda a: a.reshape(-1).astype(jnp.int32)
    max_rows = n * TOP_K + n_tiles * N_EXPERTS * (RUN_ROWS - 1) + N_EXPERTS * (MOE_BLOCK - 1)
    n_blocks = -(-max_rows // MOE_BLOCK)
    block_row = jnp.arange(n_blocks, dtype=jnp.int32) * MOE_BLOCK
    block_e = jnp.minimum(jnp.sum((pend[None, :] <= block_row[:, None]).astype(jnp.int32), axis=1), N_EXPERTS - 1)
    n_used = (pend[-1:] // MOE_BLOCK).astype(jnp.int32)

    tile_rows = flat(jnp.sum(run_te, axis=1))
    tail = flat(jnp.concatenate([pstart + total, (padded - total) // RUN_ROWS]))
    xs = _dispatch(flat(cnt_te), flat(lstart), flat(gstart), tile_rows, tail, h, lposp_t,
                   n_blocks * MOE_BLOCK, tm_mix)
    ys = _experts(block_e, n_used, xs, w["w_gu"], w["b_gu"], w["w_d"], w["b_d"])
    out = _combine(flat(cnt_te), flat(lstart), flat(gstart), tile_rows, x2, lposp_t.T, wt_t.T, w["g_final"], ys,
                   tm_mix)
    return out.reshape(b, s, d)


def kernel(x_prompt, x_sample, g_mix, w_in, b_in, sinks, w_dw, b_dw, ln_g, ln_b, w_pw, b_pw, w_out, g_ffn,
           w_router, b_router, w_gate_up, b_gate_up, w_down, b_down, g_final):
    w = _prep_weights(g_mix, w_in, b_in, sinks, w_dw, b_dw, ln_g, ln_b, w_pw, b_pw, w_out, g_ffn,
                      w_router, b_router, w_gate_up, b_gate_up, w_down, b_down, g_final)
    outs = []
    for x in (x_prompt, x_sample):
        s = x.shape[1]
        tables = _rotary_tables(s)
        outs.append(_trunk(x, w, tables, tm_proj=min(512, s), tq=min(512, s), tm_mix=min(256, s)))
    return tuple(outs)
```

```python
import functools

import jax
import jax.numpy as jnp
from jax import lax
from jax.experimental import pallas as pl
from jax.experimental.pallas import tpu as pltpu

D_MODEL = 1024
N_Q_HEADS = 16
N_KV_HEADS = 2
HEAD_DIM = 64
GROUP = N_Q_HEADS // N_KV_HEADS
ROT_DIM = HEAD_DIM // 4
ROPE_THETA = 500000.0
WINDOW = 128
ATTN_SCALE = HEAD_DIM ** -0.5
CONV_KERNEL = 31
CONV_PAD = CONV_KERNEL // 2
N_EXPERTS = 32
TOP_K = 4
D_FF = D_MODEL
SWIGLU_LIMIT = 7.0
SWIGLU_ALPHA = 1.702
MOE_BLOCK = 512
RMS_EPS = 1e-5
LN_EPS = 1e-5

KV_COLS = N_KV_HEADS * HEAD_DIM
LANES = 128
SUBLANES = 8
HALF = D_MODEL // 2
HALO = 16
NEG_BIG = -1e30
ATTN_LOOKAHEAD = 2

VMEM_LIMIT = 56 * 1024 * 1024

F32 = jnp.float32
BF16 = jnp.bfloat16


def _cparams(*sem):
    return pltpu.CompilerParams(dimension_semantics=sem, vmem_limit_bytes=VMEM_LIMIT)


def _pack_rows(x):
    lo = lax.bitcast_convert_type(x[:, :HALF].astype(BF16).astype(F32), jnp.uint32)
    hi = lax.bitcast_convert_type(x[:, HALF:].astype(BF16).astype(F32), jnp.uint32)
    return (lo >> 16) | (hi & jnp.uint32(0xFFFF0000))


def _unpack_rows(p):
    lo = lax.bitcast_convert_type(p << 16, F32)
    hi = lax.bitcast_convert_type(p & jnp.uint32(0xFFFF0000), F32)
    return lo, hi


def _inproj_body(x_ref, g_ref, w_ref, b_ref, cos_ref, s1_ref, s2_ref,
                 u_ref, q_ref, k_ref, v_ref, gc_ref, ga_ref):
    x = x_ref[...]
    xn = x * lax.rsqrt(jnp.mean(x * x, axis=-1, keepdims=True) + RMS_EPS) * g_ref[...]
    xb = xn.astype(BF16)

    def proj(c0, c1):
        return jnp.dot(xb, w_ref[:, c0:c1], preferred_element_type=F32) + b_ref[:, c0:c1]

    cos = cos_ref[...]
    s1 = s1_ref[...]
    s2 = s2_ref[...]

    def rotary(t):
        half = ROT_DIM // 2
        outs = []
        for c in range(t.shape[1] // LANES):
            tc = t[:, c * LANES:(c + 1) * LANES]
            outs.append(tc * cos + pltpu.roll(tc, half, 1) * s1 + pltpu.roll(tc, LANES - half, 1) * s2)
        return outs[0] if len(outs) == 1 else jnp.concatenate(outs, axis=1)

    d = D_MODEL
    a = proj(0, d)
    gate = proj(d, 2 * d)
    u_ref[...] = (a * jax.nn.sigmoid(gate)).astype(BF16)
    q = proj(2 * d, 3 * d)
    q_ref[...] = (rotary(q) * ATTN_SCALE).astype(BF16)
    kv = proj(3 * d, 3 * d + 2 * KV_COLS)
    k_ref[...] = rotary(kv[:, :KV_COLS]).astype(BF16)
    v_ref[...] = kv[:, KV_COLS:].astype(BF16)
    c0 = 3 * d + 2 * KV_COLS
    gc_ref[...] = jax.nn.sigmoid(proj(c0, c0 + d)).astype(BF16)
    ga_ref[...] = jax.nn.sigmoid(proj(c0 + d, c0 + 2 * d)).astype(BF16)


def _inproj(x, g_mix, w_in, b_in, cos_t, s1_t, s2_t, seq, tm):
    n = x.shape[0]
    in_cols = w_in.shape[1]
    spt = seq // tm
    row = lambda w: pl.BlockSpec((tm, w), lambda i: (i, 0))
    const = lambda r, c: pl.BlockSpec((r, c), lambda i: (0, 0))
    tab = pl.BlockSpec((tm, LANES), lambda i: (i % spt, 0))
    big = jax.ShapeDtypeStruct((n, D_MODEL), BF16)
    small = jax.ShapeDtypeStruct((n, KV_COLS), BF16)
    return pl.pallas_call(
        _inproj_body,
        grid=(n // tm,),
        in_specs=[row(D_MODEL), const(1, D_MODEL), const(D_MODEL, in_cols), const(1, in_cols), tab, tab, tab],
        out_specs=[row(D_MODEL), row(D_MODEL), row(KV_COLS), row(KV_COLS), row(D_MODEL), row(D_MODEL)],
        out_shape=[big, big, small, small, big, big],
        compiler_params=_cparams("arbitrary"),
        name="inproj",
    )(x, g_mix, w_in, b_in, cos_t, s1_t, s2_t)


def _attn_body(q_ref, kp_ref, kc_ref, kn_ref, vp_ref, vc_ref, vn_ref, sink_ref, o_ref, kbuf, vbuf,
               *, tq, nblk):
    i = pl.program_id(1)
    blk = WINDOW
    kbuf[0:blk, :] = kp_ref[0]
    kbuf[blk:blk + tq, :] = kc_ref[0]
    kbuf[blk + tq:, :] = kn_ref[0]
    vbuf[0:blk, :] = vp_ref[0]
    vbuf[blk:blk + tq, :] = vc_ref[0]
    vbuf[blk + tq:, :] = vn_ref[0]

    qi = lax.broadcasted_iota(jnp.int32, (blk, 3 * blk), 0)
    kj = lax.broadcasted_iota(jnp.int32, (blk, 3 * blk), 1)
    band = jnp.abs(qi + blk - kj) <= WINDOW

    def scores(j, h):
        g = h // GROUP
        qh = q_ref[0, j * blk:(j + 1) * blk, h * HEAD_DIM:(h + 1) * HEAD_DIM]
        kh = kbuf[j * blk:(j + 3) * blk, g * HEAD_DIM:(g + 1) * HEAD_DIM]
        return lax.dot_general(qh, kh, (((1,), (1,)), ((), ())), preferred_element_type=F32)

    units = [(j, h) for j in range(tq // blk) for h in range(N_Q_HEADS)]
    pending = [scores(*u) for u in units[:ATTN_LOOKAHEAD]]
    outs = []
    for idx, (j, h) in enumerate(units):
        if idx + ATTN_LOOKAHEAD < len(units):
            pending.append(scores(*units[idx + ATTN_LOOKAHEAD]))
        s = pending.pop(0)
        n = i * (tq // blk) + j
        valid = band & ((kj >= blk) | (n > 0)) & ((kj < 2 * blk) | (n < nblk - 1))
        g = h // GROUP
        vh = vbuf[j * blk:(j + 3) * blk, g * HEAD_DIM:(g + 1) * HEAD_DIM]
        s = jnp.where(valid, s, NEG_BIG)
        sink = sink_ref[h]
        m = jnp.maximum(jnp.max(s, axis=-1, keepdims=True), sink)
        p = jnp.exp(s - m)
        denom = jnp.sum(p, axis=-1, keepdims=True) + jnp.exp(sink - m)
        o = jnp.dot(p.astype(BF16), vh, preferred_element_type=F32)
        outs.append(o / denom)
        if h == N_Q_HEADS - 1:
            o_ref[0, j * blk:(j + 1) * blk, :] = jnp.concatenate(outs, axis=1).astype(BF16)
            outs = []


def _attn(q, k, v, sinks, tq):
    b, s, _ = q.shape
    r = tq // WINDOW
    nblk = s // WINDOW
    cur = lambda w: pl.BlockSpec((1, tq, w), lambda bi, i: (bi, i, 0))
    prev = pl.BlockSpec((1, WINDOW, KV_COLS), lambda bi, i: (bi, jnp.maximum(i * r - 1, 0), 0))
    nxt = pl.BlockSpec((1, WINDOW, KV_COLS), lambda bi, i: (bi, jnp.minimum((i + 1) * r, nblk - 1), 0))
    return pl.pallas_call(
        functools.partial(_attn_body, tq=tq, nblk=nblk),
        grid=(b, s // tq),
        in_specs=[cur(D_MODEL), prev, cur(KV_COLS), nxt, prev, cur(KV_COLS), nxt,
                  pl.BlockSpec(memory_space=pltpu.SMEM)],
        out_specs=cur(D_MODEL),
        out_shape=jax.ShapeDtypeStruct((b, s, D_MODEL), BF16),
        scratch_shapes=[pltpu.VMEM((tq + 2 * WINDOW, KV_COLS), BF16),
                        pltpu.VMEM((tq + 2 * WINDOW, KV_COLS), BF16)],
        compiler_params=_cparams("arbitrary", "arbitrary"),
        name="attn",
    )(q, k, k, k, v, v, v, sinks)


CONV_ROWS = 64


def _mix_body(x_ref, up_ref, uc_ref, un_ref, at_ref, gc_ref, ga_ref,
              wdw_ref, bdw_ref, lng_ref, lnb_ref, wpw_ref, bpw_ref, wout_ref,
              gffn_ref, wrh_ref, wrl_ref, br_ref,
              x2_ref, h_ref, wt_ref, lposp_ref, cnt_ref,
              ubuf, sbuf, ybuf, lg_ref, *, tm, spt):
    i = pl.program_id(0)
    tile = jnp.minimum(i, pl.num_programs(0) - 2)
    first = (tile % spt) == 0
    last = (tile % spt) == spt - 1

    @pl.when(i == 0)
    def _():
        cnt_ref[...] = jnp.zeros_like(cnt_ref)
        lg_ref[...] = jnp.zeros_like(lg_ref)

    prev_logits = lg_ref[...]

    for c in range(D_MODEL // LANES):
        cs = slice(c * LANES, (c + 1) * LANES)
        ubuf[c, 0:HALO, :] = jnp.where(first, 0.0, up_ref[:, cs].astype(F32))
        ubuf[c, HALO:HALO + tm, :] = uc_ref[:, cs].astype(F32)
        ubuf[c, HALO + tm:, :] = jnp.where(last, 0.0, un_ref[:, cs].astype(F32))

    def shift_rows(r0, nrows):
        for c in range(D_MODEL // LANES):
            win = ubuf[c, pl.ds(r0, nrows + SUBLANES), :]
            for s in range(1, SUBLANES):
                sbuf[s - 1, c, pl.ds(r0, nrows), :] = win[s:s + nrows]

    def shift_chunk(r, carry):
        shift_rows(pl.multiple_of(r * CONV_ROWS, CONV_ROWS), CONV_ROWS)
        return carry

    lax.fori_loop(0, tm // CONV_ROWS, shift_chunk, 0)
    shift_rows(tm, 2 * HALO - SUBLANES)

    base = HALO - CONV_PAD

    for c in range(D_MODEL // LANES):
        cs = slice(c * LANES, (c + 1) * LANES)
        wts = [jnp.broadcast_to(wdw_ref[t:t + 1, cs], (SUBLANES, LANES)) for t in range(CONV_KERNEL)]
        bias = jnp.broadcast_to(bdw_ref[:, cs], (SUBLANES, LANES))

        def conv_rows(r, carry, c=c, cs=cs, wts=wts, bias=bias):
            r0 = pl.multiple_of(r * CONV_ROWS, CONV_ROWS)
            accs = [bias] * (CONV_ROWS // SUBLANES)
            for t in range(CONV_KERNEL):
                a, s = divmod(base + t, SUBLANES)
                rows = pl.ds(r0 + a * SUBLANES, CONV_ROWS)
                tap = ubuf[c, rows, :] if s == 0 else sbuf[s - 1, c, rows, :]
                accs = [acc + tap[k * SUBLANES:(k + 1) * SUBLANES] * wts[t] for k, acc in enumerate(accs)]
            ybuf[pl.ds(r0, CONV_ROWS), cs] = jnp.concatenate(accs, axis=0)
            return carry

        lax.fori_loop(0, tm // CONV_ROWS, conv_rows, 0)

    y = ybuf[...]
    mu = jnp.mean(y, axis=-1, keepdims=True)
    yc = y - mu
    var = jnp.mean(yc * yc, axis=-1, keepdims=True)
    yn = yc * lax.rsqrt(var + LN_EPS) * lng_ref[...] + lnb_ref[...]
    act = yn * jax.nn.sigmoid(yn)
    conv = jnp.dot(act.astype(BF16), wpw_ref[...], preferred_element_type=F32) + bpw_ref[...]

    merged = gc_ref[...] * conv.astype(BF16) + ga_ref[...] * at_ref[...]
    x2 = x_ref[...] + jnp.dot(merged, wout_ref[...], preferred_element_type=F32)
    x2_ref[...] = x2

    h = x2 * lax.rsqrt(jnp.mean(x2 * x2, axis=-1, keepdims=True) + RMS_EPS) * gffn_ref[...]
    h_hi = h.astype(BF16)
    h_ref[...] = h_hi

    h_lo = (h - h_hi.astype(F32)).astype(BF16)
    nt = lambda a, b: lax.dot_general(a, b, (((1,), (1,)), ((), ())), preferred_element_type=F32)
    lg_ref[...] = nt(wrh_ref[...], h_hi) + (nt(wrh_ref[...], h_lo) + nt(wrl_ref[...], h_hi)) + br_ref[...]

    eio = lax.broadcasted_iota(jnp.int32, (N_EXPERTS, tm), 0).astype(F32)
    work = prev_logits
    sels, tops = [], []
    for _ in range(TOP_K):
        m = jnp.max(work, axis=0, keepdims=True)
        idx = jnp.min(jnp.where(work == m, eio, float(N_EXPERTS)), axis=0, keepdims=True)
        sel = eio == idx
        work = jnp.where(sel, -jnp.inf, work)
        sels.append(sel)
        tops.append(m)
    es = [jnp.exp(t - tops[0]) for t in tops]
    tot = es[0] + es[1] + es[2] + es[3]
    wt_ref[...] = jnp.concatenate([e / tot for e in es], axis=0)

    multihot = (sels[0] | sels[1] | sels[2] | sels[3])
    ti = lax.broadcasted_iota(jnp.int32, (tm, tm), 0)
    tj = lax.broadcasted_iota(jnp.int32, (tm, tm), 1)
    before = (ti < tj).astype(BF16)
    prefix = jnp.dot(multihot.astype(BF16), before, preferred_element_type=F32)
    cnt = jnp.sum(multihot.astype(F32), axis=1, keepdims=True)
    cnt_slot = jnp.floor((cnt + (SUBLANES - 1)) * (1.0 / SUBLANES)) * SUBLANES
    ei = lax.broadcasted_iota(jnp.int32, (N_EXPERTS, N_EXPERTS), 0)
    ej = lax.broadcasted_iota(jnp.int32, (N_EXPERTS, N_EXPERTS), 1)
    earlier = (ej < ei).astype(BF16)

    run_start = jnp.dot(earlier, jnp.broadcast_to(cnt_slot, (N_EXPERTS, LANES)).astype(BF16),
                        preferred_element_type=F32)[:, :1]
    pos = prefix + run_start
    lposp_ref[...] = jnp.concatenate(
        [jnp.sum(jnp.where(s, pos, 0.0), axis=0, keepdims=True) for s in sels], axis=0).astype(jnp.int32)
    tile_lane = lax.broadcasted_iota(jnp.int32, cnt_ref.shape, 1)
    cnt_ref[...] = jnp.where(tile_lane == i - 1, cnt, cnt_ref[...])


def _mix(x, u, attn, gc, ga, w_dw, b_dw, ln_g, ln_b, w_pw, b_pw, w_out, g_ffn, w_r_hi, w_r_lo, b_r, seq, tm):
    n = x.shape[0]
    spt = seq // tm
    hb = tm // HALO
    nh = n // HALO
    n_tiles = n // tm
    tile = lambda i: jnp.minimum(i, n_tiles - 1)
    row = lambda w: pl.BlockSpec((tm, w), lambda i: (tile(i), 0))
    const = lambda r, c: pl.BlockSpec((r, c), lambda i: (0, 0))
    halo_prev = pl.BlockSpec((HALO, D_MODEL), lambda i: (jnp.maximum(tile(i) * hb - 1, 0), 0))
    halo_next = pl.BlockSpec((HALO, D_MODEL), lambda i: (jnp.minimum((tile(i) + 1) * hb, nh - 1), 0))
    tok = pl.BlockSpec((TOP_K, tm), lambda i: (0, jnp.maximum(i - 1, 0)))
    return pl.pallas_call(
        functools.partial(_mix_body, tm=tm, spt=spt),
        grid=(n_tiles + 1,),
        in_specs=[row(D_MODEL), halo_prev, row(D_MODEL), halo_next, row(D_MODEL), row(D_MODEL), row(D_MODEL),
                  const(CONV_KERNEL, D_MODEL), const(1, D_MODEL), const(1, D_MODEL), const(1, D_MODEL),
                  const(D_MODEL, D_MODEL), const(1, D_MODEL), const(D_MODEL, D_MODEL),
                  const(1, D_MODEL), const(N_EXPERTS, D_MODEL), const(N_EXPERTS, D_MODEL), const(N_EXPERTS, 1)],
        out_specs=[row(D_MODEL), row(D_MODEL), tok, tok, const(N_EXPERTS, n // tm)],
        out_shape=[jax.ShapeDtypeStruct((n, D_MODEL), F32),
                   jax.ShapeDtypeStruct((n, D_MODEL), BF16),
                   jax.ShapeDtypeStruct((TOP_K, n), F32),
                   jax.ShapeDtypeStruct((TOP_K, n), jnp.int32),
                   jax.ShapeDtypeStruct((N_EXPERTS, n // tm), F32)],
        scratch_shapes=[pltpu.VMEM((D_MODEL // LANES, tm + 2 * HALO, LANES), F32),
                        pltpu.VMEM((SUBLANES - 1, D_MODEL // LANES, tm + 2 * HALO - SUBLANES, LANES), F32),
                        pltpu.VMEM((tm, D_MODEL), F32),
                        pltpu.VMEM((N_EXPERTS, tm), F32)],
        compiler_params=_cparams("arbitrary"),
        name="mix",
    )(x, u, u, u, attn, gc, ga, w_dw, b_dw, ln_g, ln_b, w_pw, b_pw, w_out, g_ffn, w_r_hi, w_r_lo, b_r)


RUN_ROWS = SUBLANES
MM_ROWS = 256
TOKEN_TILE = 256


def _tile_slots(tm):
    return -(-(TOP_K * tm + N_EXPERTS * (RUN_ROWS - 1)) // MM_ROWS) * MM_ROWS


PIECE_ROWS = (32, RUN_ROWS)
WAIT_ROWS = (512, 64, RUN_ROWS)


def _rows_copy(src_ref, src_row, dst_ref, dst_row, n_rows, sem):
    rows = lambda r: pl.ds(pl.multiple_of(r, RUN_ROWS), n_rows)
    return pltpu.make_async_copy(src_ref.at[rows(src_row)], dst_ref.at[rows(dst_row)], sem)


def _repeat(n, fn):
    def body(j, c):
        fn(j)
        return c
    lax.fori_loop(0, n, body, 0)


def _start_run_copies(cnt_ref, tile, start):
    def per_expert(e, carry):
        run = (cnt_ref[tile * N_EXPERTS + e] + (RUN_ROWS - 1)) & -RUN_ROWS
        done = 0
        for n_rows in PIECE_ROWS:
            n = lax.shift_right_logical(run - done, n_rows.bit_length() - 1)
            _repeat(n, lambda j, n_rows=n_rows, done=done: start(e, done + j * n_rows, n_rows))
            done = done + n * n_rows
        return carry

    lax.fori_loop(0, N_EXPERTS, per_expert, 0)


def _wait_rows(total_rows, wait):
    done = 0
    for n_rows in WAIT_ROWS:
        n = lax.shift_right_logical(total_rows - done, n_rows.bit_length() - 1)
        _repeat(n, lambda j, n_rows=n_rows: wait(n_rows))
        done = done + n * n_rows


def _dispatch_body(cnt_ref, lstart_ref, gstart_ref, rows_ref, tail_ref, h_ref, lpos_ref, *rest, tm, chained,
                   zero_tails):
    xs_ref, obuf, zbuf, sem = rest[1:] if chained else rest
    i = pl.program_id(0)
    slot = i % 2
    buf, other = obuf.at[slot], obuf.at[1 - slot]

    lpos = lpos_ref[...]
    hb = h_ref[...]
    for c in range(obuf.shape[1] // MM_ROWS):
        rio = lax.broadcasted_iota(jnp.int32, (MM_ROWS, tm), 0) + c * MM_ROWS
        onehot = (rio == lpos[0:1]) | (rio == lpos[1:2]) | (rio == lpos[2:3]) | (rio == lpos[3:4])
        srt = jnp.dot(onehot.astype(BF16), hb, preferred_element_type=F32)
        lo = lax.bitcast_convert_type(srt[:, :HALF], jnp.uint32)
        hi = lax.bitcast_convert_type(srt[:, HALF:], jnp.uint32)
        buf[c * MM_ROWS:(c + 1) * MM_ROWS, :] = (lo >> 16) | hi

    _start_run_copies(cnt_ref, i, lambda e, off, n_rows: _rows_copy(
        buf, lstart_ref[i * N_EXPERTS + e] + off, xs_ref, gstart_ref[i * N_EXPERTS + e] + off, n_rows,
        sem.at[slot]).start())

    @pl.when(i > 0)
    def _():
        _wait_rows(rows_ref[i - 1], lambda n_rows: _rows_copy(other, 0, xs_ref, 0, n_rows, sem.at[1 - slot]).wait())

    @pl.when(i == pl.num_programs(0) - 1)
    def _():
        _wait_rows(rows_ref[i], lambda n_rows: _rows_copy(buf, 0, xs_ref, 0, n_rows, sem.at[slot]).wait())
        if not zero_tails:
            return

        zbuf[...] = jnp.zeros_like(zbuf)

        def per_expert(e, carry):
            n = tail_ref[N_EXPERTS + e]
            _repeat(n, lambda j: _rows_copy(zbuf, 0, xs_ref, tail_ref[e] + j * RUN_ROWS, RUN_ROWS,
                                            sem.at[slot]).start())
            _repeat(n, lambda j: _rows_copy(zbuf, 0, xs_ref, 0, RUN_ROWS, sem.at[slot]).wait())
            return carry

        lax.fori_loop(0, N_EXPERTS, per_expert, 0)


def _dispatch(cnt, lstart, gstart, tile_rows, tail, h, lpos_t, xs_in, n_rows_out, zero_tails, tm):
    n = h.shape[0]
    chained = xs_in is not None
    grid_spec = pltpu.PrefetchScalarGridSpec(
        num_scalar_prefetch=5,
        grid=(n // tm,),
        in_specs=[pl.BlockSpec((tm, D_MODEL), lambda i, *_: (i, 0)),
                  pl.BlockSpec((TOP_K, tm), lambda i, *_: (0, i))]
        + ([pl.BlockSpec(memory_space=pl.ANY)] if chained else []),
        out_specs=pl.BlockSpec(memory_space=pl.ANY),
        scratch_shapes=[pltpu.VMEM((2, _tile_slots(tm), HALF), jnp.uint32),
                        pltpu.VMEM((RUN_ROWS, HALF), jnp.uint32),
                        pltpu.SemaphoreType.DMA((2,))],
    )
    return pl.pallas_call(
        functools.partial(_dispatch_body, tm=tm, chained=chained, zero_tails=zero_tails),
        grid_spec=grid_spec,
        out_shape=jax.ShapeDtypeStruct((n_rows_out, HALF), jnp.uint32),
        input_output_aliases={7: 0} if chained else {},
        compiler_params=_cparams("arbitrary"),
        name="dispatch",
    )(cnt, lstart, gstart, tile_rows, tail, h, lpos_t, *([xs_in] if chained else []))


CAST_ROWS = 128


def _experts_body(be_ref, nused_ref, xs_ref, wgu_ref, bgu_ref, wd_ref, bd_ref, ys_ref, wgu_bf, wd_bf):
    b = pl.program_id(0)
    used = b < nused_ref[0]

    @pl.when(used & ((b == 0) | (be_ref[b] != be_ref[jnp.maximum(b - 1, 0)])))
    def _():
        def cast(r, carry):
            rows = pl.ds(pl.multiple_of(r * CAST_ROWS, CAST_ROWS), CAST_ROWS)
            wgu_bf[rows, :] = wgu_ref[0, rows, :].astype(BF16)
            wd_bf[rows, :] = wd_ref[0, rows, :].astype(BF16)
            return carry

        lax.fori_loop(0, D_MODEL // CAST_ROWS, cast, 0)

    @pl.when(used)
    def _():
        lo, hi = _unpack_rows(xs_ref[...])
        gu = (jnp.dot(lo.astype(BF16), wgu_bf[:HALF, :], preferred_element_type=F32)
              + jnp.dot(hi.astype(BF16), wgu_bf[HALF:, :], preferred_element_type=F32)
              + bgu_ref[0])
        gate = jnp.minimum(gu[:, :D_FF], SWIGLU_LIMIT)
        up = jnp.clip(gu[:, D_FF:], -SWIGLU_LIMIT, SWIGLU_LIMIT)
        glu = gate * jax.nn.sigmoid(gate * SWIGLU_ALPHA)
        act = ((up + 1.0) * glu).astype(BF16)
        y = jnp.dot(act, wd_bf[...], preferred_element_type=F32) + bd_ref[0]
        ys_ref[...] = _pack_rows(y)

    @pl.when(b >= nused_ref[0])
    def _():
        ys_ref[...] = jnp.zeros_like(ys_ref)


def _experts(block_e, n_used, xs, w_gu, b_gu, w_d, b_d):
    p = xs.shape[0]
    nb = p // MOE_BLOCK

    def xmap(b, be, nu):
        return (jnp.minimum(b, jnp.maximum(nu[0] - 1, 0)), 0)

    def wmap(b, be, nu):
        return (be[b], 0, 0)

    grid_spec = pltpu.PrefetchScalarGridSpec(
        num_scalar_prefetch=2,
        grid=(nb,),
        in_specs=[pl.BlockSpec((MOE_BLOCK, HALF), xmap),
                  pl.BlockSpec((1, D_MODEL, 2 * D_FF), wmap),
                  pl.BlockSpec((1, 1, 2 * D_FF), wmap),
                  pl.BlockSpec((1, D_FF, D_MODEL), wmap),
                  pl.BlockSpec((1, 1, D_MODEL), wmap)],
        out_specs=pl.BlockSpec((MOE_BLOCK, HALF), lambda b, be, nu: (b, 0)),
        scratch_shapes=[pltpu.VMEM((D_MODEL, 2 * D_FF), BF16), pltpu.VMEM((D_FF, D_MODEL), BF16)],
    )
    return pl.pallas_call(
        _experts_body,
        grid_spec=grid_spec,
        out_shape=jax.ShapeDtypeStruct((p, HALF), jnp.uint32),
        compiler_params=_cparams("arbitrary"),
        name="experts",
    )(block_e, n_used, xs, w_gu, b_gu, w_d, b_d)


def _combine_body(cnt_ref, lstartp_ref, gstart_ref, rows_ref, x2_ref, lposp_ref, wt_ref, g_ref, ys_ref, o_ref,
                  gbuf, sem, *, tm, slots):
    i = pl.program_id(0)
    slot = i % 2

    def fetch(tile, s):
        _start_run_copies(cnt_ref, tile, lambda e, off, n_rows: _rows_copy(
            ys_ref, gstart_ref[tile * N_EXPERTS + e] + off, gbuf.at[s], lstartp_ref[tile * N_EXPERTS + e] + off,
            n_rows, sem.at[s]).start())

    @pl.when(i == 0)
    def _():
        gbuf[...] = jnp.zeros_like(gbuf)
        fetch(i, slot)

    @pl.when(i + 1 < pl.num_programs(0))
    def _():
        fetch(i + 1, 1 - slot)

    _wait_rows(rows_ref[i], lambda n_rows: _rows_copy(ys_ref, 0, gbuf.at[slot], 0, n_rows, sem.at[slot]).wait())

    lposp = lposp_ref[...]
    wt = wt_ref[...]
    sio = lax.broadcasted_iota(jnp.int32, (tm, slots), 1)
    comb = jnp.zeros((tm, slots), F32)
    for k in range(TOP_K):
        comb = jnp.where(sio == lposp[:, k:k + 1], wt[:, k:k + 1], comb)
    comb = comb.astype(BF16)
    lo, hi = _unpack_rows(gbuf[slot])
    moe_lo = jnp.dot(comb, lo.astype(BF16), preferred_element_type=F32)
    moe_hi = jnp.dot(comb, hi.astype(BF16), preferred_element_type=F32)

    x2 = x2_ref[...]
    y_lo = x2[:, :HALF] + moe_lo
    y_hi = x2[:, HALF:] + moe_hi
    ms = (jnp.sum(y_lo * y_lo, axis=-1, keepdims=True) + jnp.sum(y_hi * y_hi, axis=-1, keepdims=True)) / D_MODEL
    inv = lax.rsqrt(ms + RMS_EPS)
    g = g_ref[...]
    o_ref[:, :HALF] = y_lo * inv * g[:, :HALF]
    o_ref[:, HALF:] = y_hi * inv * g[:, HALF:]


def _combine(cnt, lstartp, gstart, tile_rows, x2, lposp_tok, wt_tok, g_final, ys, tm):
    n = x2.shape[0]
    slots = _tile_slots(tm)
    grid_spec = pltpu.PrefetchScalarGridSpec(
        num_scalar_prefetch=4,
        grid=(n // tm,),
        in_specs=[pl.BlockSpec((tm, D_MODEL), lambda i, *_: (i, 0)),
                  pl.BlockSpec((tm, TOP_K), lambda i, *_: (i, 0)),
                  pl.BlockSpec((tm, TOP_K), lambda i, *_: (i, 0)),
                  pl.BlockSpec((1, D_MODEL), lambda i, *_: (0, 0)),
                  pl.BlockSpec(memory_space=pl.ANY)],
        out_specs=pl.BlockSpec((tm, D_MODEL), lambda i, *_: (i, 0)),
        scratch_shapes=[pltpu.VMEM((2, slots, HALF), jnp.uint32), pltpu.SemaphoreType.DMA((2,))],
    )
    return pl.pallas_call(
        functools.partial(_combine_body, tm=tm, slots=slots),
        grid_spec=grid_spec,
        out_shape=jax.ShapeDtypeStruct((n, D_MODEL), F32),
        compiler_params=_cparams("arbitrary"),
        name="combine",
    )(cnt, lstartp, gstart, tile_rows, x2, lposp_tok, wt_tok, g_final, ys)


def _rotary_tables(seq):
    half = ROT_DIM // 2
    inv_freq = ROPE_THETA ** (-jnp.arange(half, dtype=F32) * 2.0 / ROT_DIM)
    ang = jnp.arange(seq, dtype=F32)[:, None] * inv_freq[None, :]
    cos, sin = jnp.cos(ang), jnp.sin(ang)
    ones = jnp.ones((seq, HEAD_DIM - ROT_DIM), F32)
    zeros = jnp.zeros((seq, HEAD_DIM - ROT_DIM), F32)
    zh = jnp.zeros((seq, half), F32)
    cos_h = jnp.concatenate([cos, cos, ones], axis=1)
    s1_h = jnp.concatenate([zh, sin, zeros], axis=1)
    s2_h = jnp.concatenate([-sin, zh, zeros], axis=1)
    rep = LANES // HEAD_DIM
    return jnp.tile(cos_h, (1, rep)), jnp.tile(s1_h, (1, rep)), jnp.tile(s2_h, (1, rep))


def _prep_weights(g_mix, w_in, b_in, sinks, w_dw, b_dw, ln_g, ln_b, w_pw, b_pw, w_out,
                  g_ffn, w_router, b_router, w_gate_up, b_gate_up, w_down, b_down, g_final):
    r1 = lambda a: a.reshape(1, -1).astype(F32)
    w_r_t = w_router[0].T.astype(F32)
    w_r_hi = w_r_t.astype(BF16)
    return dict(
        g_mix=r1(g_mix[0]), w_in=w_in[0].astype(BF16), b_in=r1(b_in[0]), sinks=sinks[0].astype(F32),
        w_dw=w_dw[0].reshape(CONV_KERNEL, D_MODEL).astype(F32), b_dw=r1(b_dw[0]),
        ln_g=r1(ln_g[0]), ln_b=r1(ln_b[0]), w_pw=w_pw[0].astype(BF16), b_pw=r1(b_pw[0]),
        w_out=w_out[0].astype(BF16), g_ffn=r1(g_ffn[0]),
        w_r_hi=w_r_hi, w_r_lo=(w_r_t - w_r_hi.astype(F32)).astype(BF16),
        b_r=b_router[0].reshape(N_EXPERTS, 1).astype(F32),
        w_gu=w_gate_up[0].astype(F32), b_gu=b_gate_up[0].reshape(N_EXPERTS, 1, 2 * D_FF).astype(F32),
        w_d=w_down[0].astype(F32), b_d=b_down[0].reshape(N_EXPERTS, 1, D_MODEL).astype(F32),
        g_final=r1(g_final),
    )


def _mixer_stage(x, w, tables, tm_proj, tq, tm_mix):
    b, s, d = x.shape
    n = b * s
    xf = x.reshape(n, d)
    u, q, k, v, gc, ga = _inproj(xf, w["g_mix"], w["w_in"], w["b_in"], *tables, seq=s, tm=tm_proj)
    attn = _attn(q.reshape(b, s, d), k.reshape(b, s, KV_COLS), v.reshape(b, s, KV_COLS), w["sinks"], tq)
    return _mix(xf, u, attn.reshape(n, d), gc, ga, w["w_dw"], w["b_dw"], w["ln_g"], w["ln_b"], w["w_pw"], w["b_pw"],
                w["w_out"], w["g_ffn"], w["w_r_hi"], w["w_r_lo"], w["b_r"], seq=s, tm=tm_mix)


def _moe_stage(mixed, w, tm):
    tiles = [m[0].shape[0] // tm for m in mixed]
    n_tiles = sum(tiles)

    cnt_te = jnp.concatenate([m[4] for m in mixed], axis=1).T.astype(jnp.int32)
    run_te = (cnt_te + RUN_ROWS - 1) // RUN_ROWS * RUN_ROWS
    total = jnp.sum(run_te, axis=0)
    padded = (total + MOE_BLOCK - 1) // MOE_BLOCK * MOE_BLOCK
    pend = jnp.cumsum(padded)
    pstart = pend - padded
    gstart = pstart[None, :] + jnp.cumsum(run_te, axis=0) - run_te
    lstart = jnp.cumsum(run_te, axis=1) - run_te
    tile_rows = jnp.sum(run_te, axis=1)
    flat = lambda a: a.reshape(-1).astype(jnp.int32)
    max_rows = n_tiles * (tm * TOP_K + N_EXPERTS * (RUN_ROWS - 1)) + N_EXPERTS * (MOE_BLOCK - 1)
    n_blocks = -(-max_rows // MOE_BLOCK)
    block_row = jnp.arange(n_blocks, dtype=jnp.int32) * MOE_BLOCK
    block_e = jnp.minimum(jnp.sum((pend[None, :] <= block_row[:, None]).astype(jnp.int32), axis=1), N_EXPERTS - 1)
    n_used = (pend[-1:] // MOE_BLOCK).astype(jnp.int32)
    tail = flat(jnp.concatenate([pstart + total, (padded - total) // RUN_ROWS]))

    starts = [sum(tiles[:g]) for g in range(len(tiles))]
    tables = [[flat(a[t0:t0 + nt]) for a in (cnt_te, lstart, gstart, tile_rows)] for t0, nt in zip(starts, tiles)]
    xs = None
    for g, (m, tab) in enumerate(zip(mixed, tables)):
        xs = _dispatch(*tab, tail, m[1], m[3], xs, n_blocks * MOE_BLOCK, g == len(mixed) - 1, tm)
    ys = _experts(block_e, n_used, xs, w["w_gu"], w["b_gu"], w["w_d"], w["b_d"])
    return [_combine(*tab, m[0], m[3].T, m[2].T, w["g_final"], ys, tm) for m, tab in zip(mixed, tables)]


def kernel(x_prompt, x_sample, g_mix, w_in, b_in, sinks, w_dw, b_dw, ln_g, ln_b, w_pw, b_pw, w_out, g_ffn,
           w_router, b_router, w_gate_up, b_gate_up, w_down, b_down, g_final):
    w = _prep_weights(g_mix, w_in, b_in, sinks, w_dw, b_dw, ln_g, ln_b, w_pw, b_pw, w_out, g_ffn,
                      w_router, b_router, w_gate_up, b_gate_up, w_down, b_down, g_final)
    xs = (x_prompt, x_sample)
    tm = min(TOKEN_TILE, *(x.shape[1] for x in xs))
    mixed = [_mixer_stage(x, w, _rotary_tables(x.shape[1]), tm_proj=min(2 * TOKEN_TILE, x.shape[1]),
                          tq=min(2 * TOKEN_TILE, x.shape[1]), tm_mix=tm) for x in xs]
    outs = _moe_stage(mixed, w, tm)
    return tuple(o.reshape(x.shape) for o, x in zip(outs, xs))
```

```python
import functools

import jax
import jax.numpy as jnp
from jax import lax
from jax.experimental import pallas as pl
from jax.experimental.pallas import tpu as pltpu

D_MODEL = 1024
N_Q_HEADS = 16
N_KV_HEADS = 2
HEAD_DIM = 64
GROUP = N_Q_HEADS // N_KV_HEADS
ROT_DIM = HEAD_DIM // 4
ROPE_THETA = 500000.0
WINDOW = 128
ATTN_SCALE = HEAD_DIM ** -0.5
CONV_KERNEL = 31
CONV_PAD = CONV_KERNEL // 2
N_EXPERTS = 32
TOP_K = 4
D_FF = D_MODEL
SWIGLU_LIMIT = 7.0
SWIGLU_ALPHA = 1.702
MOE_BLOCK = 512
RMS_EPS = 1e-5
LN_EPS = 1e-5

KV_COLS = N_KV_HEADS * HEAD_DIM
LANES = 128
SUBLANES = 8
HALF = D_MODEL // 2
HALO = 16
NEG_BIG = -1e30
ATTN_LOOKAHEAD = 2

VMEM_LIMIT = 56 * 1024 * 1024

F32 = jnp.float32
BF16 = jnp.bfloat16


def _cparams(*sem):
    return pltpu.CompilerParams(dimension_semantics=sem, vmem_limit_bytes=VMEM_LIMIT)


def _pack_rows(x):
    lo = lax.bitcast_convert_type(x[:, :HALF].astype(BF16).astype(F32), jnp.uint32)
    hi = lax.bitcast_convert_type(x[:, HALF:].astype(BF16).astype(F32), jnp.uint32)
    return (lo >> 16) | (hi & jnp.uint32(0xFFFF0000))


def _unpack_rows(p):
    lo = lax.bitcast_convert_type(p << 16, F32)
    hi = lax.bitcast_convert_type(p & jnp.uint32(0xFFFF0000), F32)
    return lo, hi


def _inproj_body(x_ref, g_ref, w_ref, b_ref, cos_ref, s1_ref, s2_ref,
                 u_ref, q_ref, k_ref, v_ref, gc_ref, ga_ref):
    x = x_ref[...]
    xn = x * lax.rsqrt(jnp.mean(x * x, axis=-1, keepdims=True) + RMS_EPS) * g_ref[...]
    xb = xn.astype(BF16)

    def proj(c0, c1):
        return jnp.dot(xb, w_ref[:, c0:c1], preferred_element_type=F32) + b_ref[:, c0:c1]

    cos = cos_ref[...]
    s1 = s1_ref[...]
    s2 = s2_ref[...]

    def rotary(t):
        half = ROT_DIM // 2
        outs = []
        for c in range(t.shape[1] // LANES):
            tc = t[:, c * LANES:(c + 1) * LANES]
            outs.append(tc * cos + pltpu.roll(tc, half, 1) * s1 + pltpu.roll(tc, LANES - half, 1) * s2)
        return outs[0] if len(outs) == 1 else jnp.concatenate(outs, axis=1)

    d = D_MODEL
    a = proj(0, d)
    gate = proj(d, 2 * d)
    u_ref[...] = (a * jax.nn.sigmoid(gate)).astype(BF16)
    q = proj(2 * d, 3 * d)
    q_ref[...] = (rotary(q) * ATTN_SCALE).astype(BF16)
    kv = proj(3 * d, 3 * d + 2 * KV_COLS)
    k_ref[...] = rotary(kv[:, :KV_COLS]).astype(BF16)
    v_ref[...] = kv[:, KV_COLS:].astype(BF16)
    c0 = 3 * d + 2 * KV_COLS
    gc_ref[...] = jax.nn.sigmoid(proj(c0, c0 + d)).astype(BF16)
    ga_ref[...] = jax.nn.sigmoid(proj(c0 + d, c0 + 2 * d)).astype(BF16)


def _inproj(x, g_mix, w_in, b_in, cos_t, s1_t, s2_t, seq, tm):
    n = x.shape[0]
    in_cols = w_in.shape[1]
    spt = seq // tm
    row = lambda w: pl.BlockSpec((tm, w), lambda i: (i, 0))
    const = lambda r, c: pl.BlockSpec((r, c), lambda i: (0, 0))
    tab = pl.BlockSpec((tm, LANES), lambda i: (i % spt, 0))
    big = jax.ShapeDtypeStruct((n, D_MODEL), BF16)
    small = jax.ShapeDtypeStruct((n, KV_COLS), BF16)
    return pl.pallas_call(
        _inproj_body,
        grid=(n // tm,),
        in_specs=[row(D_MODEL), const(1, D_MODEL), const(D_MODEL, in_cols), const(1, in_cols), tab, tab, tab],
        out_specs=[row(D_MODEL), row(D_MODEL), row(KV_COLS), row(KV_COLS), row(D_MODEL), row(D_MODEL)],
        out_shape=[big, big, small, small, big, big],
        compiler_params=_cparams("arbitrary"),
        name="inproj",
    )(x, g_mix, w_in, b_in, cos_t, s1_t, s2_t)


def _attn_body(q_ref, kp_ref, kc_ref, kn_ref, vp_ref, vc_ref, vn_ref, sink_ref, o_ref, kbuf, vbuf,
               *, tq, nblk):
    i = pl.program_id(1)
    blk = WINDOW
    kbuf[0:blk, :] = kp_ref[0]
    kbuf[blk:blk + tq, :] = kc_ref[0]
    kbuf[blk + tq:, :] = kn_ref[0]
    vbuf[0:blk, :] = vp_ref[0]
    vbuf[blk:blk + tq, :] = vc_ref[0]
    vbuf[blk + tq:, :] = vn_ref[0]

    qi = lax.broadcasted_iota(jnp.int32, (blk, 3 * blk), 0)
    kj = lax.broadcasted_iota(jnp.int32, (blk, 3 * blk), 1)
    band = jnp.abs(qi + blk - kj) <= WINDOW

    def scores(j, h):
        g = h // GROUP
        qh = q_ref[0, j * blk:(j + 1) * blk, h * HEAD_DIM:(h + 1) * HEAD_DIM]
        kh = kbuf[j * blk:(j + 3) * blk, g * HEAD_DIM:(g + 1) * HEAD_DIM]
        return lax.dot_general(qh, kh, (((1,), (1,)), ((), ())), preferred_element_type=F32)

    units = [(j, h) for j in range(tq // blk) for h in range(N_Q_HEADS)]
    pending = [scores(*u) for u in units[:ATTN_LOOKAHEAD]]
    outs = []
    for idx, (j, h) in enumerate(units):
        if idx + ATTN_LOOKAHEAD < len(units):
            pending.append(scores(*units[idx + ATTN_LOOKAHEAD]))
        s = pending.pop(0)
        n = i * (tq // blk) + j
        valid = band & ((kj >= blk) | (n > 0)) & ((kj < 2 * blk) | (n < nblk - 1))
        g = h // GROUP
        vh = vbuf[j * blk:(j + 3) * blk, g * HEAD_DIM:(g + 1) * HEAD_DIM]
        s = jnp.where(valid, s, NEG_BIG)
        sink = sink_ref[h]
        m = jnp.maximum(jnp.max(s, axis=-1, keepdims=True), sink)
        p = jnp.exp(s - m)
        denom = jnp.sum(p, axis=-1, keepdims=True) + jnp.exp(sink - m)
        o = jnp.dot(p.astype(BF16), vh, preferred_element_type=F32)
        outs.append(o / denom)
        if h == N_Q_HEADS - 1:
            o_ref[0, j * blk:(j + 1) * blk, :] = jnp.concatenate(outs, axis=1).astype(BF16)
            outs = []


def _attn(q, k, v, sinks, tq):
    b, s, _ = q.shape
    r = tq // WINDOW
    nblk = s // WINDOW
    cur = lambda w: pl.BlockSpec((1, tq, w), lambda bi, i: (bi, i, 0))
    prev = pl.BlockSpec((1, WINDOW, KV_COLS), lambda bi, i: (bi, jnp.maximum(i * r - 1, 0), 0))
    nxt = pl.BlockSpec((1, WINDOW, KV_COLS), lambda bi, i: (bi, jnp.minimum((i + 1) * r, nblk - 1), 0))
    return pl.pallas_call(
        functools.partial(_attn_body, tq=tq, nblk=nblk),
        grid=(b, s // tq),
        in_specs=[cur(D_MODEL), prev, cur(KV_COLS), nxt, prev, cur(KV_COLS), nxt,
                  pl.BlockSpec(memory_space=pltpu.SMEM)],
        out_specs=cur(D_MODEL),
        out_shape=jax.ShapeDtypeStruct((b, s, D_MODEL), BF16),
        scratch_shapes=[pltpu.VMEM((tq + 2 * WINDOW, KV_COLS), BF16),
                        pltpu.VMEM((tq + 2 * WINDOW, KV_COLS), BF16)],
        compiler_params=_cparams("arbitrary", "arbitrary"),
        name="attn",
    )(q, k, k, k, v, v, v, sinks)


CONV_ROWS = 64


def _mix_body(x_ref, up_ref, uc_ref, un_ref, at_ref, gc_ref, ga_ref,
              wdw_ref, bdw_ref, lng_ref, lnb_ref, wpw_ref, bpw_ref, wout_ref,
              gffn_ref, wrh_ref, wrl_ref, br_ref,
              x2_ref, h_ref, wt_ref, lposp_ref, cnt_ref,
              ubuf, sbuf, ybuf, lg_ref, *, tm, spt):
    i = pl.program_id(0)
    tile = jnp.minimum(i, pl.num_programs(0) - 2)
    first = (tile % spt) == 0
    last = (tile % spt) == spt - 1

    @pl.when(i == 0)
    def _():
        cnt_ref[...] = jnp.zeros_like(cnt_ref)
        lg_ref[...] = jnp.zeros_like(lg_ref)

    prev_logits = lg_ref[...]

    for c in range(D_MODEL // LANES):
        cs = slice(c * LANES, (c + 1) * LANES)
        ubuf[c, 0:HALO, :] = jnp.where(first, 0.0, up_ref[:, cs].astype(F32))
        ubuf[c, HALO:HALO + tm, :] = uc_ref[:, cs].astype(F32)
        ubuf[c, HALO + tm:, :] = jnp.where(last, 0.0, un_ref[:, cs].astype(F32))

    def shift_rows(r0, nrows):
        for c in range(D_MODEL // LANES):
            win = ubuf[c, pl.ds(r0, nrows + SUBLANES), :]
            for s in range(1, SUBLANES):
                sbuf[s - 1, c, pl.ds(r0, nrows), :] = win[s:s + nrows]

    def shift_chunk(r, carry):
        shift_rows(pl.multiple_of(r * CONV_ROWS, CONV_ROWS), CONV_ROWS)
        return carry

    lax.fori_loop(0, tm // CONV_ROWS, shift_chunk, 0)
    shift_rows(tm, 2 * HALO - SUBLANES)

    base = HALO - CONV_PAD

    for c in range(D_MODEL // LANES):
        cs = slice(c * LANES, (c + 1) * LANES)
        wts = [jnp.broadcast_to(wdw_ref[t:t + 1, cs], (SUBLANES, LANES)) for t in range(CONV_KERNEL)]
        bias = jnp.broadcast_to(bdw_ref[:, cs], (SUBLANES, LANES))

        def conv_rows(r, carry, c=c, cs=cs, wts=wts, bias=bias):
            r0 = pl.multiple_of(r * CONV_ROWS, CONV_ROWS)
            accs = [bias] * (CONV_ROWS // SUBLANES)
            for t in range(CONV_KERNEL):
                a, s = divmod(base + t, SUBLANES)
                rows = pl.ds(r0 + a * SUBLANES, CONV_ROWS)
                tap = ubuf[c, rows, :] if s == 0 else sbuf[s - 1, c, rows, :]
                accs = [acc + tap[k * SUBLANES:(k + 1) * SUBLANES] * wts[t] for k, acc in enumerate(accs)]
            ybuf[pl.ds(r0, CONV_ROWS), cs] = jnp.concatenate(accs, axis=0)
            return carry

        lax.fori_loop(0, tm // CONV_ROWS, conv_rows, 0)

    y = ybuf[...]
    mu = jnp.mean(y, axis=-1, keepdims=True)
    yc = y - mu
    var = jnp.mean(yc * yc, axis=-1, keepdims=True)
    yn = yc * lax.rsqrt(var + LN_EPS) * lng_ref[...] + lnb_ref[...]
    act = yn * jax.nn.sigmoid(yn)
    conv = jnp.dot(act.astype(BF16), wpw_ref[...], preferred_element_type=F32) + bpw_ref[...]

    merged = gc_ref[...] * conv.astype(BF16) + ga_ref[...] * at_ref[...]
    x2 = x_ref[...] + jnp.dot(merged, wout_ref[...], preferred_element_type=F32)
    x2_ref[...] = x2

    h = x2 * lax.rsqrt(jnp.mean(x2 * x2, axis=-1, keepdims=True) + RMS_EPS) * gffn_ref[...]
    h_hi = h.astype(BF16)
    h_ref[...] = h_hi

    h_lo = (h - h_hi.astype(F32)).astype(BF16)
    nt = lambda a, b: lax.dot_general(a, b, (((1,), (1,)), ((), ())), preferred_element_type=F32)
    lg_ref[...] = nt(wrh_ref[...], h_hi) + (nt(wrh_ref[...], h_lo) + nt(wrl_ref[...], h_hi)) + br_ref[...]

    eio = lax.broadcasted_iota(jnp.int32, (N_EXPERTS, tm), 0).astype(F32)
    work = prev_logits
    sels, tops = [], []
    for _ in range(TOP_K):
        m = jnp.max(work, axis=0, keepdims=True)
        idx = jnp.min(jnp.where(work == m, eio, float(N_EXPERTS)), axis=0, keepdims=True)
        sel = eio == idx
        work = jnp.where(sel, -jnp.inf, work)
        sels.append(sel)
        tops.append(m)
    es = [jnp.exp(t - tops[0]) for t in tops]
    tot = es[0] + es[1] + es[2] + es[3]
    wt_ref[...] = jnp.concatenate([e / tot for e in es], axis=0)

    multihot = (sels[0] | sels[1] | sels[2] | sels[3])
    ti = lax.broadcasted_iota(jnp.int32, (tm, tm), 0)
    tj = lax.broadcasted_iota(jnp.int32, (tm, tm), 1)
    before = (ti < tj).astype(BF16)
    prefix = jnp.dot(multihot.astype(BF16), before, preferred_element_type=F32)
    cnt = jnp.sum(multihot.astype(F32), axis=1, keepdims=True)
    cnt_slot = jnp.floor((cnt + (SUBLANES - 1)) * (1.0 / SUBLANES)) * SUBLANES
    ei = lax.broadcasted_iota(jnp.int32, (N_EXPERTS, N_EXPERTS), 0)
    ej = lax.broadcasted_iota(jnp.int32, (N_EXPERTS, N_EXPERTS), 1)
    earlier = (ej < ei).astype(BF16)

    run_start = jnp.dot(earlier, jnp.broadcast_to(cnt_slot, (N_EXPERTS, LANES)).astype(BF16),
                        preferred_element_type=F32)[:, :1]
    pos = prefix + run_start
    lposp_ref[...] = jnp.concatenate(
        [jnp.sum(jnp.where(s, pos, 0.0), axis=0, keepdims=True) for s in sels], axis=0).astype(jnp.int32)
    tile_lane = lax.broadcasted_iota(jnp.int32, cnt_ref.shape, 1)
    cnt_ref[...] = jnp.where(tile_lane == i - 1, cnt, cnt_ref[...])


def _mix(x, u, attn, gc, ga, w_dw, b_dw, ln_g, ln_b, w_pw, b_pw, w_out, g_ffn, w_r_hi, w_r_lo, b_r, seq, tm):
    n = x.shape[0]
    spt = seq // tm
    hb = tm // HALO
    nh = n // HALO
    n_tiles = n // tm
    tile = lambda i: jnp.minimum(i, n_tiles - 1)
    row = lambda w: pl.BlockSpec((tm, w), lambda i: (tile(i), 0))
    const = lambda r, c: pl.BlockSpec((r, c), lambda i: (0, 0))
    halo_prev = pl.BlockSpec((HALO, D_MODEL), lambda i: (jnp.maximum(tile(i) * hb - 1, 0), 0))
    halo_next = pl.BlockSpec((HALO, D_MODEL), lambda i: (jnp.minimum((tile(i) + 1) * hb, nh - 1), 0))
    tok = pl.BlockSpec((TOP_K, tm), lambda i: (0, jnp.maximum(i - 1, 0)))
    return pl.pallas_call(
        functools.partial(_mix_body, tm=tm, spt=spt),
        grid=(n_tiles + 1,),
        in_specs=[row(D_MODEL), halo_prev, row(D_MODEL), halo_next, row(D_MODEL), row(D_MODEL), row(D_MODEL),
                  const(CONV_KERNEL, D_MODEL), const(1, D_MODEL), const(1, D_MODEL), const(1, D_MODEL),
                  const(D_MODEL, D_MODEL), const(1, D_MODEL), const(D_MODEL, D_MODEL),
                  const(1, D_MODEL), const(N_EXPERTS, D_MODEL), const(N_EXPERTS, D_MODEL), const(N_EXPERTS, 1)],
        out_specs=[row(D_MODEL), row(D_MODEL), tok, tok, const(N_EXPERTS, n // tm)],
        out_shape=[jax.ShapeDtypeStruct((n, D_MODEL), F32),
                   jax.ShapeDtypeStruct((n, D_MODEL), BF16),
                   jax.ShapeDtypeStruct((TOP_K, n), F32),
                   jax.ShapeDtypeStruct((TOP_K, n), jnp.int32),
                   jax.ShapeDtypeStruct((N_EXPERTS, n // tm), F32)],
        scratch_shapes=[pltpu.VMEM((D_MODEL // LANES, tm + 2 * HALO, LANES), F32),
                        pltpu.VMEM((SUBLANES - 1, D_MODEL // LANES, tm + 2 * HALO - SUBLANES, LANES), F32),
                        pltpu.VMEM((tm, D_MODEL), F32),
                        pltpu.VMEM((N_EXPERTS, tm), F32)],
        compiler_params=_cparams("arbitrary"),
        name="mix",
    )(x, u, u, u, attn, gc, ga, w_dw, b_dw, ln_g, ln_b, w_pw, b_pw, w_out, g_ffn, w_r_hi, w_r_lo, b_r)


RUN_ROWS = SUBLANES
MM_ROWS = 256
TOKEN_TILE = 256


def _tile_slots(tm):
    return -(-(TOP_K * tm + N_EXPERTS * (RUN_ROWS - 1)) // MM_ROWS) * MM_ROWS


PIECE_ROWS = (32, RUN_ROWS)
WAIT_ROWS = (512, 64, RUN_ROWS)


def _rows_copy(src_ref, src_row, dst_ref, dst_row, n_rows, sem):
    rows = lambda r: pl.ds(pl.multiple_of(r, RUN_ROWS), n_rows)
    return pltpu.make_async_copy(src_ref.at[rows(src_row)], dst_ref.at[rows(dst_row)], sem)


def _repeat(n, fn):
    def body(j, c):
        fn(j)
        return c
    lax.fori_loop(0, n, body, 0)


def _max_pieces(tm):
    return _tile_slots(tm) // PIECE_ROWS[0], N_EXPERTS * (PIECE_ROWS[0] // PIECE_ROWS[1] - 1)


def _start_pieces(piece_refs, tile, tm, start):
    for n_rows, cap, (n_ref, loc_ref, glb_ref) in zip(PIECE_ROWS, _max_pieces(tm), piece_refs):
        _repeat(n_ref[tile], lambda p, n_rows=n_rows, cap=cap, loc_ref=loc_ref, glb_ref=glb_ref: start(
            loc_ref[tile * cap + p], glb_ref[tile * cap + p], n_rows))


def _wait_rows(total_rows, wait):
    done = 0
    for n_rows in WAIT_ROWS:
        n = lax.shift_right_logical(total_rows - done, n_rows.bit_length() - 1)
        _repeat(n, lambda j, n_rows=n_rows: wait(n_rows))
        done = done + n * n_rows


def _dispatch_body(*refs, tm, chained, zero_tails):
    pieces, (rows_ref, tail_ref, h_ref, lpos_ref), rest = (refs[0:3], refs[3:6]), refs[6:10], refs[10:]
    xs_ref, obuf, zbuf, sem = rest[1:] if chained else rest
    i = pl.program_id(0)
    slot = i % 2
    buf, other = obuf.at[slot], obuf.at[1 - slot]

    lpos = lpos_ref[...]
    hb = h_ref[...]
    for c in range(obuf.shape[1] // MM_ROWS):
        rio = lax.broadcasted_iota(jnp.int32, (MM_ROWS, tm), 0) + c * MM_ROWS
        onehot = (rio == lpos[0:1]) | (rio == lpos[1:2]) | (rio == lpos[2:3]) | (rio == lpos[3:4])
        srt = jnp.dot(onehot.astype(BF16), hb, preferred_element_type=F32)
        lo = lax.bitcast_convert_type(srt[:, :HALF], jnp.uint32)
        hi = lax.bitcast_convert_type(srt[:, HALF:], jnp.uint32)
        buf[c * MM_ROWS:(c + 1) * MM_ROWS, :] = (lo >> 16) | hi

    _start_pieces(pieces, i, tm, lambda loc, glb, n_rows: _rows_copy(
        buf, loc, xs_ref, glb, n_rows, sem.at[slot]).start())

    @pl.when(i > 0)
    def _():
        _wait_rows(rows_ref[i - 1], lambda n_rows: _rows_copy(other, 0, xs_ref, 0, n_rows, sem.at[1 - slot]).wait())

    @pl.when(i == pl.num_programs(0) - 1)
    def _():
        _wait_rows(rows_ref[i], lambda n_rows: _rows_copy(buf, 0, xs_ref, 0, n_rows, sem.at[slot]).wait())
        if not zero_tails:
            return

        zbuf[...] = jnp.zeros_like(zbuf)

        def per_expert(e, carry):
            n = tail_ref[N_EXPERTS + e]
            _repeat(n, lambda j: _rows_copy(zbuf, 0, xs_ref, tail_ref[e] + j * RUN_ROWS, RUN_ROWS,
                                            sem.at[slot]).start())
            _repeat(n, lambda j: _rows_copy(zbuf, 0, xs_ref, 0, RUN_ROWS, sem.at[slot]).wait())
            return carry

        lax.fori_loop(0, N_EXPERTS, per_expert, 0)


def _dispatch(pieces, tile_rows, tail, h, lpos_t, xs_in, n_rows_out, zero_tails, tm):
    n = h.shape[0]
    chained = xs_in is not None
    grid_spec = pltpu.PrefetchScalarGridSpec(
        num_scalar_prefetch=len(pieces) + 2,
        grid=(n // tm,),
        in_specs=[pl.BlockSpec((tm, D_MODEL), lambda i, *_: (i, 0)),
                  pl.BlockSpec((TOP_K, tm), lambda i, *_: (0, i))]
        + ([pl.BlockSpec(memory_space=pl.ANY)] if chained else []),
        out_specs=pl.BlockSpec(memory_space=pl.ANY),
        scratch_shapes=[pltpu.VMEM((2, _tile_slots(tm), HALF), jnp.uint32),
                        pltpu.VMEM((RUN_ROWS, HALF), jnp.uint32),
                        pltpu.SemaphoreType.DMA((2,))],
    )
    return pl.pallas_call(
        functools.partial(_dispatch_body, tm=tm, chained=chained, zero_tails=zero_tails),
        grid_spec=grid_spec,
        out_shape=jax.ShapeDtypeStruct((n_rows_out, HALF), jnp.uint32),
        input_output_aliases={len(pieces) + 4: 0} if chained else {},
        compiler_params=_cparams("arbitrary"),
        name="dispatch",
    )(*pieces, tile_rows, tail, h, lpos_t, *([xs_in] if chained else []))


CAST_ROWS = 128


def _experts_body(be_ref, nused_ref, xs_ref, wgu_ref, bgu_ref, wd_ref, bd_ref, ys_ref, wgu_bf, wd_bf):
    b = pl.program_id(0)
    used = b < nused_ref[0]

    @pl.when(used & ((b == 0) | (be_ref[b] != be_ref[jnp.maximum(b - 1, 0)])))
    def _():
        def cast(r, carry):
            rows = pl.ds(pl.multiple_of(r * CAST_ROWS, CAST_ROWS), CAST_ROWS)
            wgu_bf[rows, :] = wgu_ref[0, rows, :].astype(BF16)
            wd_bf[rows, :] = wd_ref[0, rows, :].astype(BF16)
            return carry

        lax.fori_loop(0, D_MODEL // CAST_ROWS, cast, 0)

    @pl.when(used)
    def _():
        lo, hi = _unpack_rows(xs_ref[...])
        gu = (jnp.dot(lo.astype(BF16), wgu_bf[:HALF, :], preferred_element_type=F32)
              + jnp.dot(hi.astype(BF16), wgu_bf[HALF:, :], preferred_element_type=F32)
              + bgu_ref[0])
        gate = jnp.minimum(gu[:, :D_FF], SWIGLU_LIMIT)
        up = jnp.clip(gu[:, D_FF:], -SWIGLU_LIMIT, SWIGLU_LIMIT)
        glu = gate * jax.nn.sigmoid(gate * SWIGLU_ALPHA)
        act = ((up + 1.0) * glu).astype(BF16)
        y = jnp.dot(act, wd_bf[...], preferred_element_type=F32) + bd_ref[0]
        ys_ref[...] = _pack_rows(y)

    @pl.when(b >= nused_ref[0])
    def _():
        ys_ref[...] = jnp.zeros_like(ys_ref)


def _experts(block_e, n_used, xs, w_gu, b_gu, w_d, b_d):
    p = xs.shape[0]
    nb = p // MOE_BLOCK

    def xmap(b, be, nu):
        return (jnp.minimum(b, jnp.maximum(nu[0] - 1, 0)), 0)

    def wmap(b, be, nu):
        return (be[b], 0, 0)

    grid_spec = pltpu.PrefetchScalarGridSpec(
        num_scalar_prefetch=2,
        grid=(nb,),
        in_specs=[pl.BlockSpec((MOE_BLOCK, HALF), xmap),
                  pl.BlockSpec((1, D_MODEL, 2 * D_FF), wmap),
                  pl.BlockSpec((1, 1, 2 * D_FF), wmap),
                  pl.BlockSpec((1, D_FF, D_MODEL), wmap),
                  pl.BlockSpec((1, 1, D_MODEL), wmap)],
        out_specs=pl.BlockSpec((MOE_BLOCK, HALF), lambda b, be, nu: (b, 0)),
        scratch_shapes=[pltpu.VMEM((D_MODEL, 2 * D_FF), BF16), pltpu.VMEM((D_FF, D_MODEL), BF16)],
    )
    return pl.pallas_call(
        _experts_body,
        grid_spec=grid_spec,
        out_shape=jax.ShapeDtypeStruct((p, HALF), jnp.uint32),
        compiler_params=_cparams("arbitrary"),
        name="experts",
    )(block_e, n_used, xs, w_gu, b_gu, w_d, b_d)


def _combine_body(*refs, tm, slots):
    pieces = (refs[0:3], refs[3:6])
    rows_ref, x2_ref, lposp_ref, wt_ref, g_ref, ys_ref, o_ref, gbuf, sem = refs[6:]
    i = pl.program_id(0)
    slot = i % 2

    def fetch(tile, s):
        _start_pieces(pieces, tile, tm, lambda loc, glb, n_rows: _rows_copy(
            ys_ref, glb, gbuf.at[s], loc, n_rows, sem.at[s]).start())

    @pl.when(i == 0)
    def _():
        gbuf[...] = jnp.zeros_like(gbuf)
        fetch(i, slot)

    @pl.when(i + 1 < pl.num_programs(0))
    def _():
        fetch(i + 1, 1 - slot)

    _wait_rows(rows_ref[i], lambda n_rows: _rows_copy(ys_ref, 0, gbuf.at[slot], 0, n_rows, sem.at[slot]).wait())

    lposp = lposp_ref[...]
    wt = wt_ref[...]
    sio = lax.broadcasted_iota(jnp.int32, (tm, slots), 1)
    comb = jnp.zeros((tm, slots), F32)
    for k in range(TOP_K):
        comb = jnp.where(sio == lposp[:, k:k + 1], wt[:, k:k + 1], comb)
    comb = comb.astype(BF16)
    lo, hi = _unpack_rows(gbuf[slot])
    moe_lo = jnp.dot(comb, lo.astype(BF16), preferred_element_type=F32)
    moe_hi = jnp.dot(comb, hi.astype(BF16), preferred_element_type=F32)

    x2 = x2_ref[...]
    y_lo = x2[:, :HALF] + moe_lo
    y_hi = x2[:, HALF:] + moe_hi
    ms = (jnp.sum(y_lo * y_lo, axis=-1, keepdims=True) + jnp.sum(y_hi * y_hi, axis=-1, keepdims=True)) / D_MODEL
    inv = lax.rsqrt(ms + RMS_EPS)
    g = g_ref[...]
    o_ref[:, :HALF] = y_lo * inv * g[:, :HALF]
    o_ref[:, HALF:] = y_hi * inv * g[:, HALF:]


def _combine(pieces, tile_rows, x2, lposp_tok, wt_tok, g_final, ys, tm):
    n = x2.shape[0]
    slots = _tile_slots(tm)
    grid_spec = pltpu.PrefetchScalarGridSpec(
        num_scalar_prefetch=len(pieces) + 1,
        grid=(n // tm,),
        in_specs=[pl.BlockSpec((tm, D_MODEL), lambda i, *_: (i, 0)),
                  pl.BlockSpec((tm, TOP_K), lambda i, *_: (i, 0)),
                  pl.BlockSpec((tm, TOP_K), lambda i, *_: (i, 0)),
                  pl.BlockSpec((1, D_MODEL), lambda i, *_: (0, 0)),
                  pl.BlockSpec(memory_space=pl.ANY)],
        out_specs=pl.BlockSpec((tm, D_MODEL), lambda i, *_: (i, 0)),
        scratch_shapes=[pltpu.VMEM((2, slots, HALF), jnp.uint32), pltpu.SemaphoreType.DMA((2,))],
    )
    return pl.pallas_call(
        functools.partial(_combine_body, tm=tm, slots=slots),
        grid_spec=grid_spec,
        out_shape=jax.ShapeDtypeStruct((n, D_MODEL), F32),
        compiler_params=_cparams("arbitrary"),
        name="combine",
    )(*pieces, tile_rows, x2, lposp_tok, wt_tok, g_final, ys)


def _rotary_tables(seq):
    half = ROT_DIM // 2
    inv_freq = ROPE_THETA ** (-jnp.arange(half, dtype=F32) * 2.0 / ROT_DIM)
    ang = jnp.arange(seq, dtype=F32)[:, None] * inv_freq[None, :]
    cos, sin = jnp.cos(ang), jnp.sin(ang)
    ones = jnp.ones((seq, HEAD_DIM - ROT_DIM), F32)
    zeros = jnp.zeros((seq, HEAD_DIM - ROT_DIM), F32)
    zh = jnp.zeros((seq, half), F32)
    cos_h = jnp.concatenate([cos, cos, ones], axis=1)
    s1_h = jnp.concatenate([zh, sin, zeros], axis=1)
    s2_h = jnp.concatenate([-sin, zh, zeros], axis=1)
    rep = LANES // HEAD_DIM
    return jnp.tile(cos_h, (1, rep)), jnp.tile(s1_h, (1, rep)), jnp.tile(s2_h, (1, rep))


def _prep_weights(g_mix, w_in, b_in, sinks, w_dw, b_dw, ln_g, ln_b, w_pw, b_pw, w_out,
                  g_ffn, w_router, b_router, w_gate_up, b_gate_up, w_down, b_down, g_final):
    r1 = lambda a: a.reshape(1, -1).astype(F32)
    w_r_t = w_router[0].T.astype(F32)
    w_r_hi = w_r_t.astype(BF16)
    return dict(
        g_mix=r1(g_mix[0]), w_in=w_in[0].astype(BF16), b_in=r1(b_in[0]), sinks=sinks[0].astype(F32),
        w_dw=w_dw[0].reshape(CONV_KERNEL, D_MODEL).astype(F32), b_dw=r1(b_dw[0]),
        ln_g=r1(ln_g[0]), ln_b=r1(ln_b[0]), w_pw=w_pw[0].astype(BF16), b_pw=r1(b_pw[0]),
        w_out=w_out[0].astype(BF16), g_ffn=r1(g_ffn[0]),
        w_r_hi=w_r_hi, w_r_lo=(w_r_t - w_r_hi.astype(F32)).astype(BF16),
        b_r=b_router[0].reshape(N_EXPERTS, 1).astype(F32),
        w_gu=w_gate_up[0].astype(F32), b_gu=b_gate_up[0].reshape(N_EXPERTS, 1, 2 * D_FF).astype(F32),
        w_d=w_down[0].astype(F32), b_d=b_down[0].reshape(N_EXPERTS, 1, D_MODEL).astype(F32),
        g_final=r1(g_final),
    )


def _mixer_stage(x, w, tables, tm_proj, tq, tm_mix):
    b, s, d = x.shape
    n = b * s
    xf = x.reshape(n, d)
    u, q, k, v, gc, ga = _inproj(xf, w["g_mix"], w["w_in"], w["b_in"], *tables, seq=s, tm=tm_proj)
    attn = _attn(q.reshape(b, s, d), k.reshape(b, s, KV_COLS), v.reshape(b, s, KV_COLS), w["sinks"], tq)
    return _mix(xf, u, attn.reshape(n, d), gc, ga, w["w_dw"], w["b_dw"], w["ln_g"], w["ln_b"], w["w_pw"], w["b_pw"],
                w["w_out"], w["g_ffn"], w["w_r_hi"], w["w_r_lo"], w["b_r"], seq=s, tm=tm_mix)


def _moe_stage(mixed, w, tm):
    tiles = [m[0].shape[0] // tm for m in mixed]
    n_tiles = sum(tiles)

    cnt_te = jnp.concatenate([m[4] for m in mixed], axis=1).T.astype(jnp.int32)
    run_te = (cnt_te + RUN_ROWS - 1) // RUN_ROWS * RUN_ROWS
    total = jnp.sum(run_te, axis=0)
    padded = (total + MOE_BLOCK - 1) // MOE_BLOCK * MOE_BLOCK
    pend = jnp.cumsum(padded)
    pstart = pend - padded
    gstart = pstart[None, :] + jnp.cumsum(run_te, axis=0) - run_te
    lstart = jnp.cumsum(run_te, axis=1) - run_te
    tile_rows = jnp.sum(run_te, axis=1)
    flat = lambda a: a.reshape(-1).astype(jnp.int32)
    max_rows = n_tiles * (tm * TOP_K + N_EXPERTS * (RUN_ROWS - 1)) + N_EXPERTS * (MOE_BLOCK - 1)
    n_blocks = -(-max_rows // MOE_BLOCK)
    block_row = jnp.arange(n_blocks, dtype=jnp.int32) * MOE_BLOCK
    block_e = jnp.minimum(jnp.sum((pend[None, :] <= block_row[:, None]).astype(jnp.int32), axis=1), N_EXPERTS - 1)
    n_used = (pend[-1:] // MOE_BLOCK).astype(jnp.int32)
    tail = flat(jnp.concatenate([pstart + total, (padded - total) // RUN_ROWS]))

    piece_lists = []
    done = jnp.zeros_like(run_te)
    for n_rows, cap in zip(PIECE_ROWS, _max_pieces(tm)):
        per_run = (run_te - done) // n_rows
        last = jnp.cumsum(per_run, axis=1)
        slot_id = jnp.arange(cap, dtype=jnp.int32)[None, :, None]
        owner = jnp.sum((last[:, None, :] <= slot_id).astype(jnp.int32), axis=2, keepdims=True)
        mine = owner == jnp.arange(N_EXPERTS, dtype=jnp.int32)[None, None, :]
        pick = lambda a: jnp.sum(jnp.where(mine, a[:, None, :], 0), axis=2)
        offset = pick(done) + (slot_id[:, :, 0] - pick(last - per_run)) * n_rows
        piece_lists.append((last[:, -1], pick(lstart) + offset, pick(gstart) + offset))
        done = done + per_run * n_rows

    starts = [sum(tiles[:g]) for g in range(len(tiles))]
    sliced = lambda t0, nt: tuple(flat(a[t0:t0 + nt]) for lst in piece_lists for a in lst)
    tables = [(sliced(t0, nt), flat(tile_rows[t0:t0 + nt])) for t0, nt in zip(starts, tiles)]
    xs = None
    for g, (m, (pieces, rows)) in enumerate(zip(mixed, tables)):
        xs = _dispatch(pieces, rows, tail, m[1], m[3], xs, n_blocks * MOE_BLOCK, g == len(mixed) - 1, tm)
    ys = _experts(block_e, n_used, xs, w["w_gu"], w["b_gu"], w["w_d"], w["b_d"])
    return [_combine(pieces, rows, m[0], m[3].T, m[2].T, w["g_final"], ys, tm)
            for m, (pieces, rows) in zip(mixed, tables)]


def kernel(x_prompt, x_sample, g_mix, w_in, b_in, sinks, w_dw, b_dw, ln_g, ln_b, w_pw, b_pw, w_out, g_ffn,
           w_router, b_router, w_gate_up, b_gate_up, w_down, b_down, g_final):
    w = _prep_weights(g_mix, w_in, b_in, sinks, w_dw, b_dw, ln_g, ln_b, w_pw, b_pw, w_out, g_ffn,
                      w_router, b_router, w_gate_up, b_gate_up, w_down, b_down, g_final)
    xs = (x_prompt, x_sample)
    tm = min(TOKEN_TILE, *(x.shape[1] for x in xs))
    mixed = [_mixer_stage(x, w, _rotary_tables(x.shape[1]), tm_proj=min(2 * TOKEN_TILE, x.shape[1]),
                          tq=min(2 * TOKEN_TILE, x.shape[1]), tm_mix=tm) for x in xs]
    outs = _moe_stage(mixed, w, tm)
    return tuple(o.reshape(x.shape) for o, x in zip(outs, xs))
```

```python
import functools

import jax
import jax.numpy as jnp
from jax import lax
from jax.experimental import pallas as pl
from jax.experimental.pallas import tpu as pltpu

D_MODEL = 1024
N_Q_HEADS = 16
N_KV_HEADS = 2
HEAD_DIM = 64
GROUP = N_Q_HEADS // N_KV_HEADS
ROT_DIM = HEAD_DIM // 4
ROPE_THETA = 500000.0
WINDOW = 128
ATTN_SCALE = HEAD_DIM ** -0.5
CONV_KERNEL = 31
CONV_PAD = CONV_KERNEL // 2
N_EXPERTS = 32
TOP_K = 4
D_FF = D_MODEL
SWIGLU_LIMIT = 7.0
SWIGLU_ALPHA = 1.702
MOE_BLOCK = 512
RMS_EPS = 1e-5
LN_EPS = 1e-5

KV_COLS = N_KV_HEADS * HEAD_DIM
LANES = 128
SUBLANES = 8
HALF = D_MODEL // 2
HALO = 16
NEG_BIG = -1e30
ATTN_LOOKAHEAD = 2

VMEM_LIMIT = 56 * 1024 * 1024

F32 = jnp.float32
BF16 = jnp.bfloat16


def _cparams(*sem):
    return pltpu.CompilerParams(dimension_semantics=sem, vmem_limit_bytes=VMEM_LIMIT)


def _pack_rows(x):
    lo = lax.bitcast_convert_type(x[:, :HALF].astype(BF16).astype(F32), jnp.uint32)
    hi = lax.bitcast_convert_type(x[:, HALF:].astype(BF16).astype(F32), jnp.uint32)
    return (lo >> 16) | (hi & jnp.uint32(0xFFFF0000))


def _unpack_rows(p):
    lo = lax.bitcast_convert_type(p << 16, F32)
    hi = lax.bitcast_convert_type(p & jnp.uint32(0xFFFF0000), F32)
    return lo, hi


def _inproj_body(x_ref, g_ref, w_ref, b_ref, cos_ref, s1_ref, s2_ref,
                 u_ref, q_ref, k_ref, v_ref, gc_ref, ga_ref):
    x = x_ref[...]
    xn = x * lax.rsqrt(jnp.mean(x * x, axis=-1, keepdims=True) + RMS_EPS) * g_ref[...]
    xb = xn.astype(BF16)

    def proj(c0, c1):
        return jnp.dot(xb, w_ref[:, c0:c1], preferred_element_type=F32) + b_ref[:, c0:c1]

    cos = cos_ref[...]
    s1 = s1_ref[...]
    s2 = s2_ref[...]

    def rotary(t):
        half = ROT_DIM // 2
        outs = []
        for c in range(t.shape[1] // LANES):
            tc = t[:, c * LANES:(c + 1) * LANES]
            outs.append(tc * cos + pltpu.roll(tc, half, 1) * s1 + pltpu.roll(tc, LANES - half, 1) * s2)
        return outs[0] if len(outs) == 1 else jnp.concatenate(outs, axis=1)

    d = D_MODEL
    a = proj(0, d)
    gate = proj(d, 2 * d)
    u_ref[...] = (a * jax.nn.sigmoid(gate)).astype(BF16)
    q = proj(2 * d, 3 * d)
    q_ref[...] = (rotary(q) * ATTN_SCALE).astype(BF16)
    kv = proj(3 * d, 3 * d + 2 * KV_COLS)
    k_ref[...] = rotary(kv[:, :KV_COLS]).astype(BF16)
    v_ref[...] = kv[:, KV_COLS:].astype(BF16)
    c0 = 3 * d + 2 * KV_COLS
    gc_ref[...] = jax.nn.sigmoid(proj(c0, c0 + d)).astype(BF16)
    ga_ref[...] = jax.nn.sigmoid(proj(c0 + d, c0 + 2 * d)).astype(BF16)


def _inproj(x, g_mix, w_in, b_in, cos_t, s1_t, s2_t, seq, tm):
    n = x.shape[0]
    in_cols = w_in.shape[1]
    spt = seq // tm
    row = lambda w: pl.BlockSpec((tm, w), lambda i: (i, 0))
    const = lambda r, c: pl.BlockSpec((r, c), lambda i: (0, 0))
    tab = pl.BlockSpec((tm, LANES), lambda i: (i % spt, 0))
    big = jax.ShapeDtypeStruct((n, D_MODEL), BF16)
    small = jax.ShapeDtypeStruct((n, KV_COLS), BF16)
    return pl.pallas_call(
        _inproj_body,
        grid=(n // tm,),
        in_specs=[row(D_MODEL), const(1, D_MODEL), const(D_MODEL, in_cols), const(1, in_cols), tab, tab, tab],
        out_specs=[row(D_MODEL), row(D_MODEL), row(KV_COLS), row(KV_COLS), row(D_MODEL), row(D_MODEL)],
        out_shape=[big, big, small, small, big, big],
        compiler_params=_cparams("arbitrary"),
        name="inproj",
    )(x, g_mix, w_in, b_in, cos_t, s1_t, s2_t)


def _attn_body(q_ref, kp_ref, kc_ref, kn_ref, vp_ref, vc_ref, vn_ref, sink_ref, o_ref, kbuf, vbuf,
               *, tq, nblk):
    i = pl.program_id(1)
    blk = WINDOW
    kbuf[0:blk, :] = kp_ref[0]
    kbuf[blk:blk + tq, :] = kc_ref[0]
    kbuf[blk + tq:, :] = kn_ref[0]
    vbuf[0:blk, :] = vp_ref[0]
    vbuf[blk:blk + tq, :] = vc_ref[0]
    vbuf[blk + tq:, :] = vn_ref[0]

    qi = lax.broadcasted_iota(jnp.int32, (blk, 3 * blk), 0)
    kj = lax.broadcasted_iota(jnp.int32, (blk, 3 * blk), 1)
    band = jnp.abs(qi + blk - kj) <= WINDOW

    def scores(j, h):
        g = h // GROUP
        qh = q_ref[0, j * blk:(j + 1) * blk, h * HEAD_DIM:(h + 1) * HEAD_DIM]
        kh = kbuf[j * blk:(j + 3) * blk, g * HEAD_DIM:(g + 1) * HEAD_DIM]
        return lax.dot_general(qh, kh, (((1,), (1,)), ((), ())), preferred_element_type=F32)

    units = [(j, h) for j in range(tq // blk) for h in range(N_Q_HEADS)]
    pending = [scores(*u) for u in units[:ATTN_LOOKAHEAD]]
    outs = []
    for idx, (j, h) in enumerate(units):
        if idx + ATTN_LOOKAHEAD < len(units):
            pending.append(scores(*units[idx + ATTN_LOOKAHEAD]))
        s = pending.pop(0)
        n = i * (tq // blk) + j
        valid = band & ((kj >= blk) | (n > 0)) & ((kj < 2 * blk) | (n < nblk - 1))
        g = h // GROUP
        vh = vbuf[j * blk:(j + 3) * blk, g * HEAD_DIM:(g + 1) * HEAD_DIM]
        s = jnp.where(valid, s, NEG_BIG)
        sink = sink_ref[h]
        m = jnp.maximum(jnp.max(s, axis=-1, keepdims=True), sink)
        p = jnp.exp(s - m)
        denom = jnp.sum(p, axis=-1, keepdims=True) + jnp.exp(sink - m)
        o = jnp.dot(p.astype(BF16), vh, preferred_element_type=F32)
        outs.append(o / denom)
        if h == N_Q_HEADS - 1:
            o_ref[0, j * blk:(j + 1) * blk, :] = jnp.concatenate(outs, axis=1).astype(BF16)
            outs = []


def _attn(q, k, v, sinks, tq):
    b, s, _ = q.shape
    r = tq // WINDOW
    nblk = s // WINDOW
    cur = lambda w: pl.BlockSpec((1, tq, w), lambda bi, i: (bi, i, 0))
    prev = pl.BlockSpec((1, WINDOW, KV_COLS), lambda bi, i: (bi, jnp.maximum(i * r - 1, 0), 0))
    nxt = pl.BlockSpec((1, WINDOW, KV_COLS), lambda bi, i: (bi, jnp.minimum((i + 1) * r, nblk - 1), 0))
    return pl.pallas_call(
        functools.partial(_attn_body, tq=tq, nblk=nblk),
        grid=(b, s // tq),
        in_specs=[cur(D_MODEL), prev, cur(KV_COLS), nxt, prev, cur(KV_COLS), nxt,
                  pl.BlockSpec(memory_space=pltpu.SMEM)],
        out_specs=cur(D_MODEL),
        out_shape=jax.ShapeDtypeStruct((b, s, D_MODEL), BF16),
        scratch_shapes=[pltpu.VMEM((tq + 2 * WINDOW, KV_COLS), BF16),
                        pltpu.VMEM((tq + 2 * WINDOW, KV_COLS), BF16)],
        compiler_params=_cparams("arbitrary", "arbitrary"),
        name="attn",
    )(q, k, k, k, v, v, v, sinks)


CONV_ROWS = 64


def _mix_body(x_ref, up_ref, uc_ref, un_ref, at_ref, gc_ref, ga_ref,
              wdw_ref, bdw_ref, lng_ref, lnb_ref, wpw_ref, bpw_ref, wout_ref,
              gffn_ref, wrh_ref, wrl_ref, br_ref,
              x2_ref, h_ref, wt_ref, lposp_ref, cnt_ref,
              ubuf, sbuf, ybuf, hh_ref, hl_ref, *, tm, spt):
    i = pl.program_id(0)
    tile = jnp.minimum(i, pl.num_programs(0) - 2)
    first = (tile % spt) == 0
    last = (tile % spt) == spt - 1

    @pl.when(i == 0)
    def _():
        cnt_ref[...] = jnp.zeros_like(cnt_ref)
        hh_ref[...] = jnp.zeros_like(hh_ref)
        hl_ref[...] = jnp.zeros_like(hl_ref)

    for c in range(D_MODEL // LANES):
        cs = slice(c * LANES, (c + 1) * LANES)
        ubuf[c, 0:HALO, :] = jnp.where(first, 0.0, up_ref[:, cs].astype(F32))
        ubuf[c, HALO:HALO + tm, :] = uc_ref[:, cs].astype(F32)
        ubuf[c, HALO + tm:, :] = jnp.where(last, 0.0, un_ref[:, cs].astype(F32))

    def shift_rows(r0, nrows):
        for c in range(D_MODEL // LANES):
            win = ubuf[c, pl.ds(r0, nrows + SUBLANES), :]
            for s in range(1, SUBLANES):
                sbuf[s - 1, c, pl.ds(r0, nrows), :] = win[s:s + nrows]

    def shift_chunk(r, carry):
        shift_rows(pl.multiple_of(r * CONV_ROWS, CONV_ROWS), CONV_ROWS)
        return carry

    lax.fori_loop(0, tm // CONV_ROWS, shift_chunk, 0)
    shift_rows(tm, 2 * HALO - SUBLANES)

    base = HALO - CONV_PAD

    for c in range(D_MODEL // LANES):
        cs = slice(c * LANES, (c + 1) * LANES)
        wts = [jnp.broadcast_to(wdw_ref[t:t + 1, cs], (SUBLANES, LANES)) for t in range(CONV_KERNEL)]
        bias = jnp.broadcast_to(bdw_ref[:, cs], (SUBLANES, LANES))

        def conv_rows(r, carry, c=c, cs=cs, wts=wts, bias=bias):
            r0 = pl.multiple_of(r * CONV_ROWS, CONV_ROWS)
            accs = [bias] * (CONV_ROWS // SUBLANES)
            for t in range(CONV_KERNEL):
                a, s = divmod(base + t, SUBLANES)
                rows = pl.ds(r0 + a * SUBLANES, CONV_ROWS)
                tap = ubuf[c, rows, :] if s == 0 else sbuf[s - 1, c, rows, :]
                accs = [acc + tap[k * SUBLANES:(k + 1) * SUBLANES] * wts[t] for k, acc in enumerate(accs)]
            ybuf[pl.ds(r0, CONV_ROWS), cs] = jnp.concatenate(accs, axis=0)
            return carry

        lax.fori_loop(0, tm // CONV_ROWS, conv_rows, 0)

    p_hi, p_lo = hh_ref[...], hl_ref[...]
    nt = lambda a, b: lax.dot_general(a, b, (((1,), (1,)), ((), ())), preferred_element_type=F32)
    logits = nt(wrh_ref[...], p_hi) + (nt(wrh_ref[...], p_lo) + nt(wrl_ref[...], p_hi)) + br_ref[...]
    eio = lax.broadcasted_iota(jnp.int32, (N_EXPERTS, tm), 0).astype(F32)
    work = logits
    sels, tops = [], []
    for _ in range(TOP_K):
        m = jnp.max(work, axis=0, keepdims=True)
        idx = jnp.min(jnp.where(work == m, eio, float(N_EXPERTS)), axis=0, keepdims=True)
        sel = eio == idx
        work = jnp.where(sel, -jnp.inf, work)
        sels.append(sel)
        tops.append(m)
    es = [jnp.exp(t - tops[0]) for t in tops]
    tot = es[0] + es[1] + es[2] + es[3]
    wt_ref[...] = jnp.concatenate([e / tot for e in es], axis=0)

    y = ybuf[...]
    mu = jnp.mean(y, axis=-1, keepdims=True)
    yc = y - mu
    var = jnp.mean(yc * yc, axis=-1, keepdims=True)
    yn = yc * lax.rsqrt(var + LN_EPS) * lng_ref[...] + lnb_ref[...]
    act = yn * jax.nn.sigmoid(yn)
    conv = jnp.dot(act.astype(BF16), wpw_ref[...], preferred_element_type=F32) + bpw_ref[...]

    merged = gc_ref[...] * conv.astype(BF16) + ga_ref[...] * at_ref[...]
    x2 = x_ref[...] + jnp.dot(merged, wout_ref[...], preferred_element_type=F32)
    x2_ref[...] = x2

    h = x2 * lax.rsqrt(jnp.mean(x2 * x2, axis=-1, keepdims=True) + RMS_EPS) * gffn_ref[...]
    h_hi = h.astype(BF16)
    h_ref[...] = h_hi
    hh_ref[...] = h_hi
    hl_ref[...] = (h - h_hi.astype(F32)).astype(BF16)

    multihot = (sels[0] | sels[1] | sels[2] | sels[3])
    ti = lax.broadcasted_iota(jnp.int32, (tm, tm), 0)
    tj = lax.broadcasted_iota(jnp.int32, (tm, tm), 1)
    before = (ti < tj).astype(BF16)
    prefix = jnp.dot(multihot.astype(BF16), before, preferred_element_type=F32)
    cnt = jnp.sum(multihot.astype(F32), axis=1, keepdims=True)
    cnt_slot = jnp.floor((cnt + (SUBLANES - 1)) * (1.0 / SUBLANES)) * SUBLANES
    ei = lax.broadcasted_iota(jnp.int32, (N_EXPERTS, N_EXPERTS), 0)
    ej = lax.broadcasted_iota(jnp.int32, (N_EXPERTS, N_EXPERTS), 1)
    earlier = (ej < ei).astype(BF16)

    run_start = jnp.dot(earlier, jnp.broadcast_to(cnt_slot, (N_EXPERTS, LANES)).astype(BF16),
                        preferred_element_type=F32)[:, :1]
    pos = prefix + run_start
    lposp_ref[...] = jnp.concatenate(
        [jnp.sum(jnp.where(s, pos, 0.0), axis=0, keepdims=True) for s in sels], axis=0).astype(jnp.int32)
    tile_lane = lax.broadcasted_iota(jnp.int32, cnt_ref.shape, 1)
    cnt_ref[...] = jnp.where(tile_lane == i - 1, cnt, cnt_ref[...])


def _mix(x, u, attn, gc, ga, w_dw, b_dw, ln_g, ln_b, w_pw, b_pw, w_out, g_ffn, w_r_hi, w_r_lo, b_r, seq, tm):
    n = x.shape[0]
    spt = seq // tm
    hb = tm // HALO
    nh = n // HALO
    n_tiles = n // tm
    tile = lambda i: jnp.minimum(i, n_tiles - 1)
    row = lambda w: pl.BlockSpec((tm, w), lambda i: (tile(i), 0))
    const = lambda r, c: pl.BlockSpec((r, c), lambda i: (0, 0))
    halo_prev = pl.BlockSpec((HALO, D_MODEL), lambda i: (jnp.maximum(tile(i) * hb - 1, 0), 0))
    halo_next = pl.BlockSpec((HALO, D_MODEL), lambda i: (jnp.minimum((tile(i) + 1) * hb, nh - 1), 0))
    tok = pl.BlockSpec((TOP_K, tm), lambda i: (0, jnp.maximum(i - 1, 0)))
    return pl.pallas_call(
        functools.partial(_mix_body, tm=tm, spt=spt),
        grid=(n_tiles + 1,),
        in_specs=[row(D_MODEL), halo_prev, row(D_MODEL), halo_next, row(D_MODEL), row(D_MODEL), row(D_MODEL),
                  const(CONV_KERNEL, D_MODEL), const(1, D_MODEL), const(1, D_MODEL), const(1, D_MODEL),
                  const(D_MODEL, D_MODEL), const(1, D_MODEL), const(D_MODEL, D_MODEL),
                  const(1, D_MODEL), const(N_EXPERTS, D_MODEL), const(N_EXPERTS, D_MODEL), const(N_EXPERTS, 1)],
        out_specs=[row(D_MODEL), row(D_MODEL), tok, tok, const(N_EXPERTS, n // tm)],
        out_shape=[jax.ShapeDtypeStruct((n, D_MODEL), F32),
                   jax.ShapeDtypeStruct((n, D_MODEL), BF16),
                   jax.ShapeDtypeStruct((TOP_K, n), F32),
                   jax.ShapeDtypeStruct((TOP_K, n), jnp.int32),
                   jax.ShapeDtypeStruct((N_EXPERTS, n // tm), F32)],
        scratch_shapes=[pltpu.VMEM((D_MODEL // LANES, tm + 2 * HALO, LANES), F32),
                        pltpu.VMEM((SUBLANES - 1, D_MODEL // LANES, tm + 2 * HALO - SUBLANES, LANES), F32),
                        pltpu.VMEM((tm, D_MODEL), F32),
                        pltpu.VMEM((tm, D_MODEL), BF16),
                        pltpu.VMEM((tm, D_MODEL), BF16)],
        compiler_params=_cparams("arbitrary"),
        name="mix",
    )(x, u, u, u, attn, gc, ga, w_dw, b_dw, ln_g, ln_b, w_pw, b_pw, w_out, g_ffn, w_r_hi, w_r_lo, b_r)


RUN_ROWS = SUBLANES
MM_ROWS = 256
TOKEN_TILE = 256


def _tile_slots(tm):
    return -(-(TOP_K * tm + N_EXPERTS * (RUN_ROWS - 1)) // MM_ROWS) * MM_ROWS


PIECE_ROWS = (32, RUN_ROWS)
WAIT_ROWS = (512, 64, RUN_ROWS)


def _rows_copy(src_ref, src_row, dst_ref, dst_row, n_rows, sem):
    rows = lambda r: pl.ds(pl.multiple_of(r, RUN_ROWS), n_rows)
    return pltpu.make_async_copy(src_ref.at[rows(src_row)], dst_ref.at[rows(dst_row)], sem)


def _repeat(n, fn):
    def body(j, c):
        fn(j)
        return c
    lax.fori_loop(0, n, body, 0)


def _max_pieces(tm):
    return _tile_slots(tm) // PIECE_ROWS[0], N_EXPERTS * (PIECE_ROWS[0] // PIECE_ROWS[1] - 1)


def _start_pieces(piece_refs, tile, tm, start):
    for n_rows, cap, (n_ref, loc_ref, glb_ref) in zip(PIECE_ROWS, _max_pieces(tm), piece_refs):
        _repeat(n_ref[tile], lambda p, n_rows=n_rows, cap=cap, loc_ref=loc_ref, glb_ref=glb_ref: start(
            loc_ref[tile * cap + p], glb_ref[tile * cap + p], n_rows))


def _wait_rows(total_rows, wait):
    done = 0
    for n_rows in WAIT_ROWS:
        n = lax.shift_right_logical(total_rows - done, n_rows.bit_length() - 1)
        _repeat(n, lambda j, n_rows=n_rows: wait(n_rows))
        done = done + n * n_rows


def _dispatch_body(*refs, tm, chained, zero_tails):
    pieces, (rows_ref, tail_ref, h_ref, lpos_ref), rest = (refs[0:3], refs[3:6]), refs[6:10], refs[10:]
    xs_ref, obuf, zbuf, sem = rest[1:] if chained else rest
    i = pl.program_id(0)
    slot = i % 2
    buf, other = obuf.at[slot], obuf.at[1 - slot]

    lpos = lpos_ref[...]
    hb = h_ref[...]
    for c in range(obuf.shape[1] // MM_ROWS):
        rio = lax.broadcasted_iota(jnp.int32, (MM_ROWS, tm), 0) + c * MM_ROWS
        onehot = (rio == lpos[0:1]) | (rio == lpos[1:2]) | (rio == lpos[2:3]) | (rio == lpos[3:4])
        srt = jnp.dot(onehot.astype(BF16), hb, preferred_element_type=F32)
        lo = lax.bitcast_convert_type(srt[:, :HALF], jnp.uint32)
        hi = lax.bitcast_convert_type(srt[:, HALF:], jnp.uint32)
        buf[c * MM_ROWS:(c + 1) * MM_ROWS, :] = (lo >> 16) | hi

    _start_pieces(pieces, i, tm, lambda loc, glb, n_rows: _rows_copy(
        buf, loc, xs_ref, glb, n_rows, sem.at[slot]).start())

    @pl.when(i > 0)
    def _():
        _wait_rows(rows_ref[i - 1], lambda n_rows: _rows_copy(other, 0, xs_ref, 0, n_rows, sem.at[1 - slot]).wait())

    @pl.when(i == pl.num_programs(0) - 1)
    def _():
        _wait_rows(rows_ref[i], lambda n_rows: _rows_copy(buf, 0, xs_ref, 0, n_rows, sem.at[slot]).wait())
        if not zero_tails:
            return

        zbuf[...] = jnp.zeros_like(zbuf)

        def per_expert(e, carry):
            n = tail_ref[N_EXPERTS + e]
            _repeat(n, lambda j: _rows_copy(zbuf, 0, xs_ref, tail_ref[e] + j * RUN_ROWS, RUN_ROWS,
                                            sem.at[slot]).start())
            _repeat(n, lambda j: _rows_copy(zbuf, 0, xs_ref, 0, RUN_ROWS, sem.at[slot]).wait())
            return carry

        lax.fori_loop(0, N_EXPERTS, per_expert, 0)


def _dispatch(pieces, tile_rows, tail, h, lpos_t, xs_in, n_rows_out, zero_tails, tm):
    n = h.shape[0]
    chained = xs_in is not None
    grid_spec = pltpu.PrefetchScalarGridSpec(
        num_scalar_prefetch=len(pieces) + 2,
        grid=(n // tm,),
        in_specs=[pl.BlockSpec((tm, D_MODEL), lambda i, *_: (i, 0)),
                  pl.BlockSpec((TOP_K, tm), lambda i, *_: (0, i))]
        + ([pl.BlockSpec(memory_space=pl.ANY)] if chained else []),
        out_specs=pl.BlockSpec(memory_space=pl.ANY),
        scratch_shapes=[pltpu.VMEM((2, _tile_slots(tm), HALF), jnp.uint32),
                        pltpu.VMEM((RUN_ROWS, HALF), jnp.uint32),
                        pltpu.SemaphoreType.DMA((2,))],
    )
    return pl.pallas_call(
        functools.partial(_dispatch_body, tm=tm, chained=chained, zero_tails=zero_tails),
        grid_spec=grid_spec,
        out_shape=jax.ShapeDtypeStruct((n_rows_out, HALF), jnp.uint32),
        input_output_aliases={len(pieces) + 4: 0} if chained else {},
        compiler_params=_cparams("arbitrary"),
        name="dispatch",
    )(*pieces, tile_rows, tail, h, lpos_t, *([xs_in] if chained else []))


CAST_ROWS = 128


def _experts_body(be_ref, nused_ref, xs_ref, wgu_ref, bgu_ref, wd_ref, bd_ref, ys_ref, wgu_bf, wd_bf):
    b = pl.program_id(0)
    used = b < nused_ref[0]

    @pl.when(used & ((b == 0) | (be_ref[b] != be_ref[jnp.maximum(b - 1, 0)])))
    def _():
        def cast(r, carry):
            rows = pl.ds(pl.multiple_of(r * CAST_ROWS, CAST_ROWS), CAST_ROWS)
            wgu_bf[rows, :] = wgu_ref[0, rows, :].astype(BF16)
            wd_bf[rows, :] = wd_ref[0, rows, :].astype(BF16)
            return carry

        lax.fori_loop(0, D_MODEL // CAST_ROWS, cast, 0)

    @pl.when(used)
    def _():
        lo, hi = _unpack_rows(xs_ref[...])
        gu = (jnp.dot(lo.astype(BF16), wgu_bf[:HALF, :], preferred_element_type=F32)
              + jnp.dot(hi.astype(BF16), wgu_bf[HALF:, :], preferred_element_type=F32)
              + bgu_ref[0])
        gate = jnp.minimum(gu[:, :D_FF], SWIGLU_LIMIT)
        up = jnp.clip(gu[:, D_FF:], -SWIGLU_LIMIT, SWIGLU_LIMIT)
        glu = gate * jax.nn.sigmoid(gate * SWIGLU_ALPHA)
        act = ((up + 1.0) * glu).astype(BF16)
        y = jnp.dot(act, wd_bf[...], preferred_element_type=F32) + bd_ref[0]
        ys_ref[...] = _pack_rows(y)

    @pl.when(b >= nused_ref[0])
    def _():
        ys_ref[...] = jnp.zeros_like(ys_ref)


def _experts(block_e, n_used, xs, w_gu, b_gu, w_d, b_d):
    p = xs.shape[0]
    nb = p // MOE_BLOCK

    def xmap(b, be, nu):
        return (jnp.minimum(b, jnp.maximum(nu[0] - 1, 0)), 0)

    def wmap(b, be, nu):
        return (be[b], 0, 0)

    grid_spec = pltpu.PrefetchScalarGridSpec(
        num_scalar_prefetch=2,
        grid=(nb,),
        in_specs=[pl.BlockSpec((MOE_BLOCK, HALF), xmap),
                  pl.BlockSpec((1, D_MODEL, 2 * D_FF), wmap),
                  pl.BlockSpec((1, 1, 2 * D_FF), wmap),
                  pl.BlockSpec((1, D_FF, D_MODEL), wmap),
                  pl.BlockSpec((1, 1, D_MODEL), wmap)],
        out_specs=pl.BlockSpec((MOE_BLOCK, HALF), lambda b, be, nu: (b, 0)),
        scratch_shapes=[pltpu.VMEM((D_MODEL, 2 * D_FF), BF16), pltpu.VMEM((D_FF, D_MODEL), BF16)],
    )
    return pl.pallas_call(
        _experts_body,
        grid_spec=grid_spec,
        out_shape=jax.ShapeDtypeStruct((p, HALF), jnp.uint32),
        compiler_params=_cparams("arbitrary"),
        name="experts",
    )(block_e, n_used, xs, w_gu, b_gu, w_d, b_d)


def _combine_body(*refs, tm, slots):
    pieces = (refs[0:3], refs[3:6])
    rows_ref, x2_ref, lposp_ref, wt_ref, g_ref, ys_ref, o_ref, gbuf, sem = refs[6:]
    i = pl.program_id(0)
    slot = i % 2

    def fetch(tile, s):
        _start_pieces(pieces, tile, tm, lambda loc, glb, n_rows: _rows_copy(
            ys_ref, glb, gbuf.at[s], loc, n_rows, sem.at[s]).start())

    @pl.when(i == 0)
    def _():
        gbuf[...] = jnp.zeros_like(gbuf)
        fetch(i, slot)

    @pl.when(i + 1 < pl.num_programs(0))
    def _():
        fetch(i + 1, 1 - slot)

    _wait_rows(rows_ref[i], lambda n_rows: _rows_copy(ys_ref, 0, gbuf.at[slot], 0, n_rows, sem.at[slot]).wait())

    lposp = lposp_ref[...]
    wt = wt_ref[...]
    sio = lax.broadcasted_iota(jnp.int32, (tm, slots), 1)
    comb = jnp.zeros((tm, slots), F32)
    for k in range(TOP_K):
        comb = jnp.where(sio == lposp[:, k:k + 1], wt[:, k:k + 1], comb)
    comb = comb.astype(BF16)
    lo, hi = _unpack_rows(gbuf[slot])
    moe_lo = jnp.dot(comb, lo.astype(BF16), preferred_element_type=F32)
    moe_hi = jnp.dot(comb, hi.astype(BF16), preferred_element_type=F32)

    x2 = x2_ref[...]
    y_lo = x2[:, :HALF] + moe_lo
    y_hi = x2[:, HALF:] + moe_hi
    ms = (jnp.sum(y_lo * y_lo, axis=-1, keepdims=True) + jnp.sum(y_hi * y_hi, axis=-1, keepdims=True)) / D_MODEL
    inv = lax.rsqrt(ms + RMS_EPS)
    g = g_ref[...]
    o_ref[:, :HALF] = y_lo * inv * g[:, :HALF]
    o_ref[:, HALF:] = y_hi * inv * g[:, HALF:]


def _combine(pieces, tile_rows, x2, lposp_tok, wt_tok, g_final, ys, tm):
    n = x2.shape[0]
    slots = _tile_slots(tm)
    grid_spec = pltpu.PrefetchScalarGridSpec(
        num_scalar_prefetch=len(pieces) + 1,
        grid=(n // tm,),
        in_specs=[pl.BlockSpec((tm, D_MODEL), lambda i, *_: (i, 0)),
                  pl.BlockSpec((tm, TOP_K), lambda i, *_: (i, 0)),
                  pl.BlockSpec((tm, TOP_K), lambda i, *_: (i, 0)),
                  pl.BlockSpec((1, D_MODEL), lambda i, *_: (0, 0)),
                  pl.BlockSpec(memory_space=pl.ANY)],
        out_specs=pl.BlockSpec((tm, D_MODEL), lambda i, *_: (i, 0)),
        scratch_shapes=[pltpu.VMEM((2, slots, HALF), jnp.uint32), pltpu.SemaphoreType.DMA((2,))],
    )
    return pl.pallas_call(
        functools.partial(_combine_body, tm=tm, slots=slots),
        grid_spec=grid_spec,
        out_shape=jax.ShapeDtypeStruct((n, D_MODEL), F32),
        compiler_params=_cparams("arbitrary"),
        name="combine",
    )(*pieces, tile_rows, x2, lposp_tok, wt_tok, g_final, ys)


def _rotary_tables(seq):
    half = ROT_DIM // 2
    inv_freq = ROPE_THETA ** (-jnp.arange(half, dtype=F32) * 2.0 / ROT_DIM)
    ang = jnp.arange(seq, dtype=F32)[:, None] * inv_freq[None, :]
    cos, sin = jnp.cos(ang), jnp.sin(ang)
    ones = jnp.ones((seq, HEAD_DIM - ROT_DIM), F32)
    zeros = jnp.zeros((seq, HEAD_DIM - ROT_DIM), F32)
    zh = jnp.zeros((seq, half), F32)
    cos_h = jnp.concatenate([cos, cos, ones], axis=1)
    s1_h = jnp.concatenate([zh, sin, zeros], axis=1)
    s2_h = jnp.concatenate([-sin, zh, zeros], axis=1)
    rep = LANES // HEAD_DIM
    return jnp.tile(cos_h, (1, rep)), jnp.tile(s1_h, (1, rep)), jnp.tile(s2_h, (1, rep))


def _prep_weights(g_mix, w_in, b_in, sinks, w_dw, b_dw, ln_g, ln_b, w_pw, b_pw, w_out,
                  g_ffn, w_router, b_router, w_gate_up, b_gate_up, w_down, b_down, g_final):
    r1 = lambda a: a.reshape(1, -1).astype(F32)
    w_r_t = w_router[0].T.astype(F32)
    w_r_hi = w_r_t.astype(BF16)
    return dict(
        g_mix=r1(g_mix[0]), w_in=w_in[0].astype(BF16), b_in=r1(b_in[0]), sinks=sinks[0].astype(F32),
        w_dw=w_dw[0].reshape(CONV_KERNEL, D_MODEL).astype(F32), b_dw=r1(b_dw[0]),
        ln_g=r1(ln_g[0]), ln_b=r1(ln_b[0]), w_pw=w_pw[0].astype(BF16), b_pw=r1(b_pw[0]),
        w_out=w_out[0].astype(BF16), g_ffn=r1(g_ffn[0]),
        w_r_hi=w_r_hi, w_r_lo=(w_r_t - w_r_hi.astype(F32)).astype(BF16),
        b_r=b_router[0].reshape(N_EXPERTS, 1).astype(F32),
        w_gu=w_gate_up[0].astype(F32), b_gu=b_gate_up[0].reshape(N_EXPERTS, 1, 2 * D_FF).astype(F32),
        w_d=w_down[0].astype(F32), b_d=b_down[0].reshape(N_EXPERTS, 1, D_MODEL).astype(F32),
        g_final=r1(g_final),
    )


def _mixer_stage(x, w, tables, tm_proj, tq, tm_mix):
    b, s, d = x.shape
    n = b * s
    xf = x.reshape(n, d)
    u, q, k, v, gc, ga = _inproj(xf, w["g_mix"], w["w_in"], w["b_in"], *tables, seq=s, tm=tm_proj)
    attn = _attn(q.reshape(b, s, d), k.reshape(b, s, KV_COLS), v.reshape(b, s, KV_COLS), w["sinks"], tq)
    return _mix(xf, u, attn.reshape(n, d), gc, ga, w["w_dw"], w["b_dw"], w["ln_g"], w["ln_b"], w["w_pw"], w["b_pw"],
                w["w_out"], w["g_ffn"], w["w_r_hi"], w["w_r_lo"], w["b_r"], seq=s, tm=tm_mix)


def _moe_stage(mixed, w, tm):
    tiles = [m[0].shape[0] // tm for m in mixed]
    n_tiles = sum(tiles)

    cnt_te = jnp.concatenate([m[4] for m in mixed], axis=1).T.astype(jnp.int32)
    run_te = (cnt_te + RUN_ROWS - 1) // RUN_ROWS * RUN_ROWS
    total = jnp.sum(run_te, axis=0)
    padded = (total + MOE_BLOCK - 1) // MOE_BLOCK * MOE_BLOCK
    pend = jnp.cumsum(padded)
    pstart = pend - padded
    gstart = pstart[None, :] + jnp.cumsum(run_te, axis=0) - run_te
    lstart = jnp.cumsum(run_te, axis=1) - run_te
    tile_rows = jnp.sum(run_te, axis=1)
    flat = lambda a: a.reshape(-1).astype(jnp.int32)
    max_rows = n_tiles * (tm * TOP_K + N_EXPERTS * (RUN_ROWS - 1)) + N_EXPERTS * (MOE_BLOCK - 1)
    n_blocks = -(-max_rows // MOE_BLOCK)
    block_row = jnp.arange(n_blocks, dtype=jnp.int32) * MOE_BLOCK
    block_e = jnp.minimum(jnp.sum((pend[None, :] <= block_row[:, None]).astype(jnp.int32), axis=1), N_EXPERTS - 1)
    n_used = (pend[-1:] // MOE_BLOCK).astype(jnp.int32)
    tail = flat(jnp.concatenate([pstart + total, (padded - total) // RUN_ROWS]))

    piece_lists = []
    done = jnp.zeros_like(run_te)
    for n_rows, cap in zip(PIECE_ROWS, _max_pieces(tm)):
        per_run = (run_te - done) // n_rows
        last = jnp.cumsum(per_run, axis=1)
        slot_id = jnp.arange(cap, dtype=jnp.int32)[None, :, None]
        owner = jnp.sum((last[:, None, :] <= slot_id).astype(jnp.int32), axis=2, keepdims=True)
        mine = owner == jnp.arange(N_EXPERTS, dtype=jnp.int32)[None, None, :]
        pick = lambda a: jnp.sum(jnp.where(mine, a[:, None, :], 0), axis=2)
        offset = pick(done) + (slot_id[:, :, 0] - pick(last - per_run)) * n_rows
        piece_lists.append((last[:, -1], pick(lstart) + offset, pick(gstart) + offset))
        done = done + per_run * n_rows

    starts = [sum(tiles[:g]) for g in range(len(tiles))]
    sliced = lambda t0, nt: tuple(flat(a[t0:t0 + nt]) for lst in piece_lists for a in lst)
    tables = [(sliced(t0, nt), flat(tile_rows[t0:t0 + nt])) for t0, nt in zip(starts, tiles)]
    xs = None
    for g, (m, (pieces, rows)) in enumerate(zip(mixed, tables)):
        xs = _dispatch(pieces, rows, tail, m[1], m[3], xs, n_blocks * MOE_BLOCK, g == len(mixed) - 1, tm)
    ys = _experts(block_e, n_used, xs, w["w_gu"], w["b_gu"], w["w_d"], w["b_d"])
    return [_combine(pieces, rows, m[0], m[3].T, m[2].T, w["g_final"], ys, tm)
            for m, (pieces, rows) in zip(mixed, tables)]


def kernel(x_prompt, x_sample, g_mix, w_in, b_in, sinks, w_dw, b_dw, ln_g, ln_b, w_pw, b_pw, w_out, g_ffn,
           w_router, b_router, w_gate_up, b_gate_up, w_down, b_down, g_final):
    w = _prep_weights(g_mix, w_in, b_in, sinks, w_dw, b_dw, ln_g, ln_b, w_pw, b_pw, w_out, g_ffn,
                      w_router, b_router, w_gate_up, b_gate_up, w_down, b_down, g_final)
    xs = (x_prompt, x_sample)
    tm = min(TOKEN_TILE, *(x.shape[1] for x in xs))
    mixed = [_mixer_stage(x, w, _rotary_tables(x.shape[1]), tm_proj=min(2 * TOKEN_TILE, x.shape[1]),
                          tq=min(2 * TOKEN_TILE, x.shape[1]), tm_mix=tm) for x in xs]
    outs = _moe_stage(mixed, w, tm)
    return tuple(o.reshape(x.shape) for o, x in zip(outs, xs))
```

```python
import functools

import jax
import jax.numpy as jnp
from jax import lax
from jax.experimental import pallas as pl
from jax.experimental.pallas import tpu as pltpu

D_MODEL = 1024
N_Q_HEADS = 16
N_KV_HEADS = 2
HEAD_DIM = 64
GROUP = N_Q_HEADS // N_KV_HEADS
ROT_DIM = HEAD_DIM // 4
ROPE_THETA = 500000.0
WINDOW = 128
ATTN_SCALE = HEAD_DIM ** -0.5
CONV_KERNEL = 31
CONV_PAD = CONV_KERNEL // 2
N_EXPERTS = 32
TOP_K = 4
D_FF = D_MODEL
SWIGLU_LIMIT = 7.0
SWIGLU_ALPHA = 1.702
MOE_BLOCK = 512
RMS_EPS = 1e-5
LN_EPS = 1e-5

KV_COLS = N_KV_HEADS * HEAD_DIM
LANES = 128
SUBLANES = 8
HALF = D_MODEL // 2
HALO = 16
NEG_BIG = -1e30
ATTN_LOOKAHEAD = 2

VMEM_LIMIT = 56 * 1024 * 1024

F32 = jnp.float32
BF16 = jnp.bfloat16


def _cparams(*sem):
    return pltpu.CompilerParams(dimension_semantics=sem, vmem_limit_bytes=VMEM_LIMIT)


def _pack_rows(x):
    lo = lax.bitcast_convert_type(x[:, :HALF].astype(BF16).astype(F32), jnp.uint32)
    hi = lax.bitcast_convert_type(x[:, HALF:].astype(BF16).astype(F32), jnp.uint32)
    return (lo >> 16) | (hi & jnp.uint32(0xFFFF0000))


def _unpack_rows(p):
    lo = lax.bitcast_convert_type(p << 16, F32)
    hi = lax.bitcast_convert_type(p & jnp.uint32(0xFFFF0000), F32)
    return lo, hi


def _inproj_body(x_ref, g_ref, w_ref, b_ref, cos_ref, s1_ref, s2_ref,
                 u_ref, q_ref, k_ref, v_ref, gc_ref, ga_ref):
    x = x_ref[...]
    xn = x * lax.rsqrt(jnp.mean(x * x, axis=-1, keepdims=True) + RMS_EPS) * g_ref[...]
    xb = xn.astype(BF16)

    def proj(c0, c1):
        return jnp.dot(xb, w_ref[:, c0:c1], preferred_element_type=F32) + b_ref[:, c0:c1]

    cos = cos_ref[...]
    s1 = s1_ref[...]
    s2 = s2_ref[...]

    def rotary(t):
        half = ROT_DIM // 2
        outs = []
        for c in range(t.shape[1] // LANES):
            tc = t[:, c * LANES:(c + 1) * LANES]
            outs.append(tc * cos + pltpu.roll(tc, half, 1) * s1 + pltpu.roll(tc, LANES - half, 1) * s2)
        return outs[0] if len(outs) == 1 else jnp.concatenate(outs, axis=1)

    d = D_MODEL
    a = proj(0, d)
    gate = proj(d, 2 * d)
    u_ref[...] = (a * jax.nn.sigmoid(gate)).astype(BF16)
    q = proj(2 * d, 3 * d)
    q_ref[...] = (rotary(q) * ATTN_SCALE).astype(BF16)
    kv = proj(3 * d, 3 * d + 2 * KV_COLS)
    k_ref[...] = rotary(kv[:, :KV_COLS]).astype(BF16)
    v_ref[...] = kv[:, KV_COLS:].astype(BF16)
    c0 = 3 * d + 2 * KV_COLS
    gc_ref[...] = jax.nn.sigmoid(proj(c0, c0 + d)).astype(BF16)
    ga_ref[...] = jax.nn.sigmoid(proj(c0 + d, c0 + 2 * d)).astype(BF16)


def _inproj(x, g_mix, w_in, b_in, cos_t, s1_t, s2_t, seq, tm):
    n = x.shape[0]
    in_cols = w_in.shape[1]
    spt = seq // tm
    row = lambda w: pl.BlockSpec((tm, w), lambda i: (i, 0))
    const = lambda r, c: pl.BlockSpec((r, c), lambda i: (0, 0))
    tab = pl.BlockSpec((tm, LANES), lambda i: (i % spt, 0))
    big = jax.ShapeDtypeStruct((n, D_MODEL), BF16)
    small = jax.ShapeDtypeStruct((n, KV_COLS), BF16)
    return pl.pallas_call(
        _inproj_body,
        grid=(n // tm,),
        in_specs=[row(D_MODEL), const(1, D_MODEL), const(D_MODEL, in_cols), const(1, in_cols), tab, tab, tab],
        out_specs=[row(D_MODEL), row(D_MODEL), row(KV_COLS), row(KV_COLS), row(D_MODEL), row(D_MODEL)],
        out_shape=[big, big, small, small, big, big],
        compiler_params=_cparams("arbitrary"),
        name="inproj",
    )(x, g_mix, w_in, b_in, cos_t, s1_t, s2_t)


def _attn_body(q_ref, kp_ref, kc_ref, kn_ref, vp_ref, vc_ref, vn_ref, sink_ref, o_ref, kbuf, vbuf,
               *, tq, nblk):
    i = pl.program_id(1)
    blk = WINDOW
    kbuf[0:blk, :] = kp_ref[0]
    kbuf[blk:blk + tq, :] = kc_ref[0]
    kbuf[blk + tq:, :] = kn_ref[0]
    vbuf[0:blk, :] = vp_ref[0]
    vbuf[blk:blk + tq, :] = vc_ref[0]
    vbuf[blk + tq:, :] = vn_ref[0]

    qi = lax.broadcasted_iota(jnp.int32, (blk, 3 * blk), 0)
    kj = lax.broadcasted_iota(jnp.int32, (blk, 3 * blk), 1)
    band = jnp.abs(qi + blk - kj) <= WINDOW

    def scores(j, h):
        g = h // GROUP
        qh = q_ref[0, j * blk:(j + 1) * blk, h * HEAD_DIM:(h + 1) * HEAD_DIM]
        kh = kbuf[j * blk:(j + 3) * blk, g * HEAD_DIM:(g + 1) * HEAD_DIM]
        return lax.dot_general(qh, kh, (((1,), (1,)), ((), ())), preferred_element_type=F32)

    units = [(j, h) for j in range(tq // blk) for h in range(N_Q_HEADS)]
    pending = [scores(*u) for u in units[:ATTN_LOOKAHEAD]]
    outs = []
    for idx, (j, h) in enumerate(units):
        if idx + ATTN_LOOKAHEAD < len(units):
            pending.append(scores(*units[idx + ATTN_LOOKAHEAD]))
        s = pending.pop(0)
        n = i * (tq // blk) + j
        valid = band & ((kj >= blk) | (n > 0)) & ((kj < 2 * blk) | (n < nblk - 1))
        g = h // GROUP
        vh = vbuf[j * blk:(j + 3) * blk, g * HEAD_DIM:(g + 1) * HEAD_DIM]
        s = jnp.where(valid, s, NEG_BIG)
        sink = sink_ref[h]
        m = jnp.maximum(jnp.max(s, axis=-1, keepdims=True), sink)
        p = jnp.exp(s - m)
        denom = jnp.sum(p, axis=-1, keepdims=True) + jnp.exp(sink - m)
        o = jnp.dot(p.astype(BF16), vh, preferred_element_type=F32)
        outs.append(o / denom)
        if h == N_Q_HEADS - 1:
            o_ref[0, j * blk:(j + 1) * blk, :] = jnp.concatenate(outs, axis=1).astype(BF16)
            outs = []


def _attn(q, k, v, sinks, tq):
    b, s, _ = q.shape
    r = tq // WINDOW
    nblk = s // WINDOW
    cur = lambda w: pl.BlockSpec((1, tq, w), lambda bi, i: (bi, i, 0))
    prev = pl.BlockSpec((1, WINDOW, KV_COLS), lambda bi, i: (bi, jnp.maximum(i * r - 1, 0), 0))
    nxt = pl.BlockSpec((1, WINDOW, KV_COLS), lambda bi, i: (bi, jnp.minimum((i + 1) * r, nblk - 1), 0))
    return pl.pallas_call(
        functools.partial(_attn_body, tq=tq, nblk=nblk),
        grid=(b, s // tq),
        in_specs=[cur(D_MODEL), prev, cur(KV_COLS), nxt, prev, cur(KV_COLS), nxt,
                  pl.BlockSpec(memory_space=pltpu.SMEM)],
        out_specs=cur(D_MODEL),
        out_shape=jax.ShapeDtypeStruct((b, s, D_MODEL), BF16),
        scratch_shapes=[pltpu.VMEM((tq + 2 * WINDOW, KV_COLS), BF16),
                        pltpu.VMEM((tq + 2 * WINDOW, KV_COLS), BF16)],
        compiler_params=_cparams("arbitrary", "arbitrary"),
        name="attn",
    )(q, k, k, k, v, v, v, sinks)


CONV_ROWS = 64


def _mix_body(x_ref, up_ref, uc_ref, un_ref, at_ref, gc_ref, ga_ref,
              wdw_ref, bdw_ref, lng_ref, lnb_ref, wpw_ref, bpw_ref, wout_ref,
              gffn_ref, wrh_ref, wrl_ref, br_ref,
              x2_ref, h_ref, wt_ref, lposp_ref, cnt_ref,
              ubuf, sbuf, ybuf, hh_ref, hl_ref, *, tm, spt):
    i = pl.program_id(0)
    tile = jnp.minimum(i, pl.num_programs(0) - 2)
    first = (tile % spt) == 0
    last = (tile % spt) == spt - 1

    @pl.when(i == 0)
    def _():
        cnt_ref[...] = jnp.zeros_like(cnt_ref)
        hh_ref[...] = jnp.zeros_like(hh_ref)
        hl_ref[...] = jnp.zeros_like(hl_ref)

    for c in range(D_MODEL // LANES):
        cs = slice(c * LANES, (c + 1) * LANES)
        ubuf[c, 0:HALO, :] = jnp.where(first, 0.0, up_ref[:, cs].astype(F32))
        ubuf[c, HALO:HALO + tm, :] = uc_ref[:, cs].astype(F32)
        ubuf[c, HALO + tm:, :] = jnp.where(last, 0.0, un_ref[:, cs].astype(F32))

    def shift_rows(r0, nrows):
        for c in range(D_MODEL // LANES):
            win = ubuf[c, pl.ds(r0, nrows + SUBLANES), :]
            for s in range(1, SUBLANES):
                sbuf[s - 1, c, pl.ds(r0, nrows), :] = win[s:s + nrows]

    def shift_chunk(r, carry):
        shift_rows(pl.multiple_of(r * CONV_ROWS, CONV_ROWS), CONV_ROWS)
        return carry

    lax.fori_loop(0, tm // CONV_ROWS, shift_chunk, 0)
    shift_rows(tm, 2 * HALO - SUBLANES)

    base = HALO - CONV_PAD

    for c in range(D_MODEL // LANES):
        cs = slice(c * LANES, (c + 1) * LANES)
        wts = [jnp.broadcast_to(wdw_ref[t:t + 1, cs], (SUBLANES, LANES)) for t in range(CONV_KERNEL)]
        bias = jnp.broadcast_to(bdw_ref[:, cs], (SUBLANES, LANES))

        def conv_rows(r, carry, c=c, cs=cs, wts=wts, bias=bias):
            r0 = pl.multiple_of(r * CONV_ROWS, CONV_ROWS)
            accs = [bias] * (CONV_ROWS // SUBLANES)
            for t in range(CONV_KERNEL):
                a, s = divmod(base + t, SUBLANES)
                rows = pl.ds(r0 + a * SUBLANES, CONV_ROWS)
                tap = ubuf[c, rows, :] if s == 0 else sbuf[s - 1, c, rows, :]
                accs = [acc + tap[k * SUBLANES:(k + 1) * SUBLANES] * wts[t] for k, acc in enumerate(accs)]
            ybuf[pl.ds(r0, CONV_ROWS), cs] = jnp.concatenate(accs, axis=0)
            return carry

        lax.fori_loop(0, tm // CONV_ROWS, conv_rows, 0)

    p_hi, p_lo = hh_ref[...], hl_ref[...]
    nt = lambda a, b: lax.dot_general(a, b, (((1,), (1,)), ((), ())), preferred_element_type=F32)
    logits = nt(wrh_ref[...], p_hi) + (nt(wrh_ref[...], p_lo) + nt(wrl_ref[...], p_hi)) + br_ref[...]
    eio = lax.broadcasted_iota(jnp.int32, (N_EXPERTS, tm), 0).astype(F32)
    work = logits
    sels, tops = [], []
    for _ in range(TOP_K):
        m = jnp.max(work, axis=0, keepdims=True)
        idx = jnp.min(jnp.where(work == m, eio, float(N_EXPERTS)), axis=0, keepdims=True)
        sel = eio == idx
        work = jnp.where(sel, -jnp.inf, work)
        sels.append(sel)
        tops.append(m)
    es = [jnp.exp(t - tops[0]) for t in tops]
    tot = es[0] + es[1] + es[2] + es[3]
    wt_ref[...] = jnp.concatenate([e / tot for e in es], axis=0)

    y = ybuf[...]
    mu = jnp.mean(y, axis=-1, keepdims=True)
    yc = y - mu
    var = jnp.mean(yc * yc, axis=-1, keepdims=True)
    yn = yc * lax.rsqrt(var + LN_EPS) * lng_ref[...] + lnb_ref[...]
    act = yn * jax.nn.sigmoid(yn)
    conv = jnp.dot(act.astype(BF16), wpw_ref[...], preferred_element_type=F32) + bpw_ref[...]

    merged = gc_ref[...] * conv.astype(BF16) + ga_ref[...] * at_ref[...]
    x2 = x_ref[...] + jnp.dot(merged, wout_ref[...], preferred_element_type=F32)
    x2_ref[...] = x2

    h = x2 * lax.rsqrt(jnp.mean(x2 * x2, axis=-1, keepdims=True) + RMS_EPS) * gffn_ref[...]
    h_hi = h.astype(BF16)
    h_ref[...] = h_hi
    hh_ref[...] = h_hi
    hl_ref[...] = (h - h_hi.astype(F32)).astype(BF16)

    multihot = (sels[0] | sels[1] | sels[2] | sels[3])
    ti = lax.broadcasted_iota(jnp.int32, (tm, tm), 0)
    tj = lax.broadcasted_iota(jnp.int32, (tm, tm), 1)
    before = (ti < tj).astype(BF16)
    prefix = jnp.dot(multihot.astype(BF16), before, preferred_element_type=F32)
    cnt = jnp.sum(multihot.astype(F32), axis=1, keepdims=True)
    cnt_slot = jnp.floor((cnt + (SUBLANES - 1)) * (1.0 / SUBLANES)) * SUBLANES
    ei = lax.broadcasted_iota(jnp.int32, (N_EXPERTS, N_EXPERTS), 0)
    ej = lax.broadcasted_iota(jnp.int32, (N_EXPERTS, N_EXPERTS), 1)
    earlier = (ej < ei).astype(BF16)

    run_start = jnp.dot(earlier, jnp.broadcast_to(cnt_slot, (N_EXPERTS, LANES)).astype(BF16),
                        preferred_element_type=F32)[:, :1]
    pos = prefix + run_start
    lposp_ref[...] = jnp.concatenate(
        [jnp.sum(jnp.where(s, pos, 0.0), axis=0, keepdims=True) for s in sels], axis=0).astype(jnp.int32)
    tile_lane = lax.broadcasted_iota(jnp.int32, cnt_ref.shape, 1)
    cnt_ref[...] = jnp.where(tile_lane == i - 1, cnt, cnt_ref[...])


def _mix(x, u, attn, gc, ga, w_dw, b_dw, ln_g, ln_b, w_pw, b_pw, w_out, g_ffn, w_r_hi, w_r_lo, b_r, seq, tm):
    n = x.shape[0]
    spt = seq // tm
    hb = tm // HALO
    nh = n // HALO
    n_tiles = n // tm
    tile = lambda i: jnp.minimum(i, n_tiles - 1)
    row = lambda w: pl.BlockSpec((tm, w), lambda i: (tile(i), 0))
    const = lambda r, c: pl.BlockSpec((r, c), lambda i: (0, 0))
    halo_prev = pl.BlockSpec((HALO, D_MODEL), lambda i: (jnp.maximum(tile(i) * hb - 1, 0), 0))
    halo_next = pl.BlockSpec((HALO, D_MODEL), lambda i: (jnp.minimum((tile(i) + 1) * hb, nh - 1), 0))
    tok = pl.BlockSpec((TOP_K, tm), lambda i: (0, jnp.maximum(i - 1, 0)))
    return pl.pallas_call(
        functools.partial(_mix_body, tm=tm, spt=spt),
        grid=(n_tiles + 1,),
        in_specs=[row(D_MODEL), halo_prev, row(D_MODEL), halo_next, row(D_MODEL), row(D_MODEL), row(D_MODEL),
                  const(CONV_KERNEL, D_MODEL), const(1, D_MODEL), const(1, D_MODEL), const(1, D_MODEL),
                  const(D_MODEL, D_MODEL), const(1, D_MODEL), const(D_MODEL, D_MODEL),
                  const(1, D_MODEL), const(N_EXPERTS, D_MODEL), const(N_EXPERTS, D_MODEL), const(N_EXPERTS, 1)],
        out_specs=[row(D_MODEL), row(D_MODEL), tok, tok, const(N_EXPERTS, n // tm)],
        out_shape=[jax.ShapeDtypeStruct((n, D_MODEL), F32),
                   jax.ShapeDtypeStruct((n, D_MODEL), BF16),
                   jax.ShapeDtypeStruct((TOP_K, n), F32),
                   jax.ShapeDtypeStruct((TOP_K, n), jnp.int32),
                   jax.ShapeDtypeStruct((N_EXPERTS, n // tm), F32)],
        scratch_shapes=[pltpu.VMEM((D_MODEL // LANES, tm + 2 * HALO, LANES), F32),
                        pltpu.VMEM((SUBLANES - 1, D_MODEL // LANES, tm + 2 * HALO - SUBLANES, LANES), F32),
                        pltpu.VMEM((tm, D_MODEL), F32),
                        pltpu.VMEM((tm, D_MODEL), BF16),
                        pltpu.VMEM((tm, D_MODEL), BF16)],
        compiler_params=_cparams("arbitrary"),
        name="mix",
    )(x, u, u, u, attn, gc, ga, w_dw, b_dw, ln_g, ln_b, w_pw, b_pw, w_out, g_ffn, w_r_hi, w_r_lo, b_r)


RUN_ROWS = SUBLANES
MM_ROWS = 256
TOKEN_TILE = 256


def _tile_slots(tm):
    return -(-(TOP_K * tm + N_EXPERTS * (RUN_ROWS - 1)) // MM_ROWS) * MM_ROWS


PIECE_ROWS = (32, RUN_ROWS)
WAIT_ROWS = (512, 64, RUN_ROWS)


def _rows_copy(src_ref, src_row, dst_ref, dst_row, n_rows, sem):
    rows = lambda r: pl.ds(pl.multiple_of(r, RUN_ROWS), n_rows)
    return pltpu.make_async_copy(src_ref.at[rows(src_row)], dst_ref.at[rows(dst_row)], sem)


def _repeat(n, fn):
    def body(j, c):
        fn(j)
        return c
    lax.fori_loop(0, n, body, 0)


def _max_pieces(tm):
    return _tile_slots(tm) // PIECE_ROWS[0], N_EXPERTS * (PIECE_ROWS[0] // PIECE_ROWS[1] - 1)


def _start_pieces(piece_refs, tile, tm, start):
    for n_rows, cap, (n_ref, loc_ref, glb_ref) in zip(PIECE_ROWS, _max_pieces(tm), piece_refs):
        _repeat(n_ref[tile], lambda p, n_rows=n_rows, cap=cap, loc_ref=loc_ref, glb_ref=glb_ref: start(
            loc_ref[tile * cap + p], glb_ref[tile * cap + p], n_rows))


def _wait_rows(total_rows, wait):
    done = 0
    for n_rows in WAIT_ROWS:
        n = lax.shift_right_logical(total_rows - done, n_rows.bit_length() - 1)
        _repeat(n, lambda j, n_rows=n_rows: wait(n_rows))
        done = done + n * n_rows


def _dispatch_body(*refs, tm, chained, zero_tails):
    pieces, (rows_ref, tail_ref, h_ref, lpos_ref), rest = (refs[0:3], refs[3:6]), refs[6:10], refs[10:]
    xs_ref, obuf, zbuf, sem = rest[1:] if chained else rest
    i = pl.program_id(0)
    slot = i % 2
    buf, other = obuf.at[slot], obuf.at[1 - slot]

    lpos = lpos_ref[...]
    hb = h_ref[...]
    for c in range(obuf.shape[1] // MM_ROWS):
        rio = lax.broadcasted_iota(jnp.int32, (MM_ROWS, tm), 0) + c * MM_ROWS
        onehot = (rio == lpos[0:1]) | (rio == lpos[1:2]) | (rio == lpos[2:3]) | (rio == lpos[3:4])
        srt = jnp.dot(onehot.astype(BF16), hb, preferred_element_type=F32)
        lo = lax.bitcast_convert_type(srt[:, :HALF], jnp.uint32)
        hi = lax.bitcast_convert_type(srt[:, HALF:], jnp.uint32)
        buf[c * MM_ROWS:(c + 1) * MM_ROWS, :] = (lo >> 16) | hi

    _start_pieces(pieces, i, tm, lambda loc, glb, n_rows: _rows_copy(
        buf, loc, xs_ref, glb, n_rows, sem.at[slot]).start())

    @pl.when(i > 0)
    def _():
        _wait_rows(rows_ref[i - 1], lambda n_rows: _rows_copy(other, 0, xs_ref, 0, n_rows, sem.at[1 - slot]).wait())

    @pl.when(i == pl.num_programs(0) - 1)
    def _():
        _wait_rows(rows_ref[i], lambda n_rows: _rows_copy(buf, 0, xs_ref, 0, n_rows, sem.at[slot]).wait())
        if not zero_tails:
            return

        zbuf[...] = jnp.zeros_like(zbuf)

        def per_expert(e, carry):
            n = tail_ref[N_EXPERTS + e]
            _repeat(n, lambda j: _rows_copy(zbuf, 0, xs_ref, tail_ref[e] + j * RUN_ROWS, RUN_ROWS,
                                            sem.at[slot]).start())
            _repeat(n, lambda j: _rows_copy(zbuf, 0, xs_ref, 0, RUN_ROWS, sem.at[slot]).wait())
            return carry

        lax.fori_loop(0, N_EXPERTS, per_expert, 0)


def _dispatch(pieces, tile_rows, tail, h, lpos_t, xs_in, n_rows_out, zero_tails, tm):
    n = h.shape[0]
    chained = xs_in is not None
    grid_spec = pltpu.PrefetchScalarGridSpec(
        num_scalar_prefetch=len(pieces) + 2,
        grid=(n // tm,),
        in_specs=[pl.BlockSpec((tm, D_MODEL), lambda i, *_: (i, 0)),
                  pl.BlockSpec((TOP_K, tm), lambda i, *_: (0, i))]
        + ([pl.BlockSpec(memory_space=pl.ANY)] if chained else []),
        out_specs=pl.BlockSpec(memory_space=pl.ANY),
        scratch_shapes=[pltpu.VMEM((2, _tile_slots(tm), HALF), jnp.uint32),
                        pltpu.VMEM((RUN_ROWS, HALF), jnp.uint32),
                        pltpu.SemaphoreType.DMA((2,))],
    )
    return pl.pallas_call(
        functools.partial(_dispatch_body, tm=tm, chained=chained, zero_tails=zero_tails),
        grid_spec=grid_spec,
        out_shape=jax.ShapeDtypeStruct((n_rows_out, HALF), jnp.uint32),
        input_output_aliases={len(pieces) + 4: 0} if chained else {},
        compiler_params=_cparams("arbitrary"),
        name="dispatch",
    )(*pieces, tile_rows, tail, h, lpos_t, *([xs_in] if chained else []))


CAST_ROWS = 128
FF_CHUNK = 256


def _experts_body(be_ref, nused_ref, xs_ref, wgu_ref, bgu_ref, wd_ref, bd_ref, ys_ref, wgu_bf, wd_bf):
    b = pl.program_id(0)
    used = b < nused_ref[0]

    @pl.when(used & ((b == 0) | (be_ref[b] != be_ref[jnp.maximum(b - 1, 0)])))
    def _():
        def cast(r, carry):
            rows = pl.ds(pl.multiple_of(r * CAST_ROWS, CAST_ROWS), CAST_ROWS)
            wgu_bf[rows, :] = wgu_ref[0, rows, :].astype(BF16)
            wd_bf[rows, :] = wd_ref[0, rows, :].astype(BF16)
            return carry

        lax.fori_loop(0, D_MODEL // CAST_ROWS, cast, 0)

    @pl.when(used)
    def _():
        lo, hi = _unpack_rows(xs_ref[...])
        lo, hi = lo.astype(BF16), hi.astype(BF16)

        def proj(c0):
            cols = slice(c0, c0 + FF_CHUNK)
            return (jnp.dot(lo, wgu_bf[:HALF, cols], preferred_element_type=F32)
                    + jnp.dot(hi, wgu_bf[HALF:, cols], preferred_element_type=F32) + bgu_ref[0, :, cols])

        chunks = range(D_FF // FF_CHUNK)
        gus = [(proj(0), proj(D_FF))]
        y = bd_ref[0]
        for c in chunks:
            if c + 1 < len(chunks):
                gus.append((proj((c + 1) * FF_CHUNK), proj(D_FF + (c + 1) * FF_CHUNK)))
            gate, up = gus[c]
            gate = jnp.minimum(gate, SWIGLU_LIMIT)
            up = jnp.clip(up, -SWIGLU_LIMIT, SWIGLU_LIMIT)
            glu = gate * jax.nn.sigmoid(gate * SWIGLU_ALPHA)
            act = ((up + 1.0) * glu).astype(BF16)
            y = y + jnp.dot(act, wd_bf[c * FF_CHUNK:(c + 1) * FF_CHUNK, :], preferred_element_type=F32)
        ys_ref[...] = _pack_rows(y)

    @pl.when(b >= nused_ref[0])
    def _():
        ys_ref[...] = jnp.zeros_like(ys_ref)


def _experts(block_e, n_used, xs, w_gu, b_gu, w_d, b_d):
    p = xs.shape[0]
    nb = p // MOE_BLOCK

    def xmap(b, be, nu):
        return (jnp.minimum(b, jnp.maximum(nu[0] - 1, 0)), 0)

    def wmap(b, be, nu):
        return (be[b], 0, 0)

    grid_spec = pltpu.PrefetchScalarGridSpec(
        num_scalar_prefetch=2,
        grid=(nb,),
        in_specs=[pl.BlockSpec((MOE_BLOCK, HALF), xmap),
                  pl.BlockSpec((1, D_MODEL, 2 * D_FF), wmap),
                  pl.BlockSpec((1, 1, 2 * D_FF), wmap),
                  pl.BlockSpec((1, D_FF, D_MODEL), wmap),
                  pl.BlockSpec((1, 1, D_MODEL), wmap)],
        out_specs=pl.BlockSpec((MOE_BLOCK, HALF), lambda b, be, nu: (b, 0)),
        scratch_shapes=[pltpu.VMEM((D_MODEL, 2 * D_FF), BF16), pltpu.VMEM((D_FF, D_MODEL), BF16)],
    )
    return pl.pallas_call(
        _experts_body,
        grid_spec=grid_spec,
        out_shape=jax.ShapeDtypeStruct((p, HALF), jnp.uint32),
        compiler_params=_cparams("arbitrary"),
        name="experts",
    )(block_e, n_used, xs, w_gu, b_gu, w_d, b_d)


def _combine_body(*refs, tm, slots):
    pieces = (refs[0:3], refs[3:6])
    rows_ref, x2_ref, lposp_ref, wt_ref, g_ref, ys_ref, o_ref, gbuf, sem = refs[6:]
    i = pl.program_id(0)
    slot = i % 2

    def fetch(tile, s):
        _start_pieces(pieces, tile, tm, lambda loc, glb, n_rows: _rows_copy(
            ys_ref, glb, gbuf.at[s], loc, n_rows, sem.at[s]).start())

    @pl.when(i == 0)
    def _():
        gbuf[...] = jnp.zeros_like(gbuf)
        fetch(i, slot)

    @pl.when(i + 1 < pl.num_programs(0))
    def _():
        fetch(i + 1, 1 - slot)

    _wait_rows(rows_ref[i], lambda n_rows: _rows_copy(ys_ref, 0, gbuf.at[slot], 0, n_rows, sem.at[slot]).wait())

    lposp = lposp_ref[...]
    wt = wt_ref[...]
    sio = lax.broadcasted_iota(jnp.int32, (tm, slots), 1)
    comb = jnp.zeros((tm, slots), F32)
    for k in range(TOP_K):
        comb = jnp.where(sio == lposp[:, k:k + 1], wt[:, k:k + 1], comb)
    comb = comb.astype(BF16)
    lo, hi = _unpack_rows(gbuf[slot])
    moe_lo = jnp.dot(comb, lo.astype(BF16), preferred_element_type=F32)
    moe_hi = jnp.dot(comb, hi.astype(BF16), preferred_element_type=F32)

    x2 = x2_ref[...]
    y_lo = x2[:, :HALF] + moe_lo
    y_hi = x2[:, HALF:] + moe_hi
    ms = (jnp.sum(y_lo * y_lo, axis=-1, keepdims=True) + jnp.sum(y_hi * y_hi, axis=-1, keepdims=True)) / D_MODEL
    inv = lax.rsqrt(ms + RMS_EPS)
    g = g_ref[...]
    o_ref[:, :HALF] = y_lo * inv * g[:, :HALF]
    o_ref[:, HALF:] = y_hi * inv * g[:, HALF:]


def _combine(pieces, tile_rows, x2, lposp_tok, wt_tok, g_final, ys, tm):
    n = x2.shape[0]
    slots = _tile_slots(tm)
    grid_spec = pltpu.PrefetchScalarGridSpec(
        num_scalar_prefetch=len(pieces) + 1,
        grid=(n // tm,),
        in_specs=[pl.BlockSpec((tm, D_MODEL), lambda i, *_: (i, 0)),
                  pl.BlockSpec((tm, TOP_K), lambda i, *_: (i, 0)),
                  pl.BlockSpec((tm, TOP_K), lambda i, *_: (i, 0)),
                  pl.BlockSpec((1, D_MODEL), lambda i, *_: (0, 0)),
                  pl.BlockSpec(memory_space=pl.ANY)],
        out_specs=pl.BlockSpec((tm, D_MODEL), lambda i, *_: (i, 0)),
        scratch_shapes=[pltpu.VMEM((2, slots, HALF), jnp.uint32), pltpu.SemaphoreType.DMA((2,))],
    )
    return pl.pallas_call(
        functools.partial(_combine_body, tm=tm, slots=slots),
        grid_spec=grid_spec,
        out_shape=jax.ShapeDtypeStruct((n, D_MODEL), F32),
        compiler_params=_cparams("arbitrary"),
        name="combine",
    )(*pieces, tile_rows, x2, lposp_tok, wt_tok, g_final, ys)


def _rotary_tables(seq):
    half = ROT_DIM // 2
    inv_freq = ROPE_THETA ** (-jnp.arange(half, dtype=F32) * 2.0 / ROT_DIM)
    ang = jnp.arange(seq, dtype=F32)[:, None] * inv_freq[None, :]
    cos, sin = jnp.cos(ang), jnp.sin(ang)
    ones = jnp.ones((seq, HEAD_DIM - ROT_DIM), F32)
    zeros = jnp.zeros((seq, HEAD_DIM - ROT_DIM), F32)
    zh = jnp.zeros((seq, half), F32)
    cos_h = jnp.concatenate([cos, cos, ones], axis=1)
    s1_h = jnp.concatenate([zh, sin, zeros], axis=1)
    s2_h = jnp.concatenate([-sin, zh, zeros], axis=1)
    rep = LANES // HEAD_DIM
    return jnp.tile(cos_h, (1, rep)), jnp.tile(s1_h, (1, rep)), jnp.tile(s2_h, (1, rep))


def _prep_weights(g_mix, w_in, b_in, sinks, w_dw, b_dw, ln_g, ln_b, w_pw, b_pw, w_out,
                  g_ffn, w_router, b_router, w_gate_up, b_gate_up, w_down, b_down, g_final):
    r1 = lambda a: a.reshape(1, -1).astype(F32)
    w_r_t = w_router[0].T.astype(F32)
    w_r_hi = w_r_t.astype(BF16)
    return dict(
        g_mix=r1(g_mix[0]), w_in=w_in[0].astype(BF16), b_in=r1(b_in[0]), sinks=sinks[0].astype(F32),
        w_dw=w_dw[0].reshape(CONV_KERNEL, D_MODEL).astype(F32), b_dw=r1(b_dw[0]),
        ln_g=r1(ln_g[0]), ln_b=r1(ln_b[0]), w_pw=w_pw[0].astype(BF16), b_pw=r1(b_pw[0]),
        w_out=w_out[0].astype(BF16), g_ffn=r1(g_ffn[0]),
        w_r_hi=w_r_hi, w_r_lo=(w_r_t - w_r_hi.astype(F32)).astype(BF16),
        b_r=b_router[0].reshape(N_EXPERTS, 1).astype(F32),
        w_gu=w_gate_up[0].astype(F32), b_gu=b_gate_up[0].reshape(N_EXPERTS, 1, 2 * D_FF).astype(F32),
        w_d=w_down[0].astype(F32), b_d=b_down[0].reshape(N_EXPERTS, 1, D_MODEL).astype(F32),
        g_final=r1(g_final),
    )


def _mixer_stage(x, w, tables, tm_proj, tq, tm_mix):
    b, s, d = x.shape
    n = b * s
    xf = x.reshape(n, d)
    u, q, k, v, gc, ga = _inproj(xf, w["g_mix"], w["w_in"], w["b_in"], *tables, seq=s, tm=tm_proj)
    attn = _attn(q.reshape(b, s, d), k.reshape(b, s, KV_COLS), v.reshape(b, s, KV_COLS), w["sinks"], tq)
    return _mix(xf, u, attn.reshape(n, d), gc, ga, w["w_dw"], w["b_dw"], w["ln_g"], w["ln_b"], w["w_pw"], w["b_pw"],
                w["w_out"], w["g_ffn"], w["w_r_hi"], w["w_r_lo"], w["b_r"], seq=s, tm=tm_mix)


def _moe_stage(mixed, w, tm):
    tiles = [m[0].shape[0] // tm for m in mixed]
    n_tiles = sum(tiles)

    cnt_te = jnp.concatenate([m[4] for m in mixed], axis=1).T.astype(jnp.int32)
    run_te = (cnt_te + RUN_ROWS - 1) // RUN_ROWS * RUN_ROWS
    total = jnp.sum(run_te, axis=0)
    padded = (total + MOE_BLOCK - 1) // MOE_BLOCK * MOE_BLOCK
    pend = jnp.cumsum(padded)
    pstart = pend - padded
    gstart = pstart[None, :] + jnp.cumsum(run_te, axis=0) - run_te
    lstart = jnp.cumsum(run_te, axis=1) - run_te
    tile_rows = jnp.sum(run_te, axis=1)
    flat = lambda a: a.reshape(-1).astype(jnp.int32)
    max_rows = n_tiles * (tm * TOP_K + N_EXPERTS * (RUN_ROWS - 1)) + N_EXPERTS * (MOE_BLOCK - 1)
    n_blocks = -(-max_rows // MOE_BLOCK)
    block_row = jnp.arange(n_blocks, dtype=jnp.int32) * MOE_BLOCK
    block_e = jnp.minimum(jnp.sum((pend[None, :] <= block_row[:, None]).astype(jnp.int32), axis=1), N_EXPERTS - 1)
    n_used = (pend[-1:] // MOE_BLOCK).astype(jnp.int32)
    tail = flat(jnp.concatenate([pstart + total, (padded - total) // RUN_ROWS]))

    piece_lists = []
    done = jnp.zeros_like(run_te)
    for n_rows, cap in zip(PIECE_ROWS, _max_pieces(tm)):
        per_run = (run_te - done) // n_rows
        last = jnp.cumsum(per_run, axis=1)
        slot_id = jnp.arange(cap, dtype=jnp.int32)[None, :, None]
        owner = jnp.sum((last[:, None, :] <= slot_id).astype(jnp.int32), axis=2, keepdims=True)
        mine = owner == jnp.arange(N_EXPERTS, dtype=jnp.int32)[None, None, :]
        pick = lambda a: jnp.sum(jnp.where(mine, a[:, None, :], 0), axis=2)
        offset = pick(done) + (slot_id[:, :, 0] - pick(last - per_run)) * n_rows
        piece_lists.append((last[:, -1], pick(lstart) + offset, pick(gstart) + offset))
        done = done + per_run * n_rows

    starts = [sum(tiles[:g]) for g in range(len(tiles))]
    sliced = lambda t0, nt: tuple(flat(a[t0:t0 + nt]) for lst in piece_lists for a in lst)
    tables = [(sliced(t0, nt), flat(tile_rows[t0:t0 + nt])) for t0, nt in zip(starts, tiles)]
    xs = None
    for g, (m, (pieces, rows)) in enumerate(zip(mixed, tables)):
        xs = _dispatch(pieces, rows, tail, m[1], m[3], xs, n_blocks * MOE_BLOCK, g == len(mixed) - 1, tm)
    ys = _experts(block_e, n_used, xs, w["w_gu"], w["b_gu"], w["w_d"], w["b_d"])
    return [_combine(pieces, rows, m[0], m[3].T, m[2].T, w["g_final"], ys, tm)
            for m, (pieces, rows) in zip(mixed, tables)]


def kernel(x_prompt, x_sample, g_mix, w_in, b_in, sinks, w_dw, b_dw, ln_g, ln_b, w_pw, b_pw, w_out, g_ffn,
           w_router, b_router, w_gate_up, b_gate_up, w_down, b_down, g_final):
    w = _prep_weights(g_mix, w_in, b_in, sinks, w_dw, b_dw, ln_g, ln_b, w_pw, b_pw, w_out, g_ffn,
                      w_router, b_router, w_gate_up, b_gate_up, w_down, b_down, g_final)
    xs = (x_prompt, x_sample)
    tm = min(TOKEN_TILE, *(x.shape[1] for x in xs))
    mixed = [_mixer_stage(x, w, _rotary_tables(x.shape[1]), tm_proj=min(2 * TOKEN_TILE, x.shape[1]),
                          tq=min(2 * TOKEN_TILE, x.shape[1]), tm_mix=tm) for x in xs]
    outs = _moe_stage(mixed, w, tm)
    return tuple(o.reshape(x.shape) for o, x in zip(outs, xs))
```

```python
import functools

import jax
import jax.numpy as jnp
from jax import lax
from jax.experimental import pallas as pl
from jax.experimental.pallas import tpu as pltpu

D_MODEL = 1024
N_Q_HEADS = 16
N_KV_HEADS = 2
HEAD_DIM = 64
GROUP = N_Q_HEADS // N_KV_HEADS
ROT_DIM = HEAD_DIM // 4
ROPE_THETA = 500000.0
WINDOW = 128
ATTN_SCALE = HEAD_DIM ** -0.5
CONV_KERNEL = 31
CONV_PAD = CONV_KERNEL // 2
N_EXPERTS = 32
TOP_K = 4
D_FF = D_MODEL
SWIGLU_LIMIT = 7.0
SWIGLU_ALPHA = 1.702
MOE_BLOCK = 512
RMS_EPS = 1e-5
LN_EPS = 1e-5

KV_COLS = N_KV_HEADS * HEAD_DIM
LANES = 128
SUBLANES = 8
HALF = D_MODEL // 2
HALO = 16
NEG_BIG = -1e30
ATTN_LOOKAHEAD = 2

VMEM_LIMIT = 56 * 1024 * 1024

F32 = jnp.float32
BF16 = jnp.bfloat16


def _cparams(*sem):
    return pltpu.CompilerParams(dimension_semantics=sem, vmem_limit_bytes=VMEM_LIMIT)


def _pack_rows(x):
    lo = lax.bitcast_convert_type(x[:, :HALF].astype(BF16).astype(F32), jnp.uint32)
    hi = lax.bitcast_convert_type(x[:, HALF:].astype(BF16).astype(F32), jnp.uint32)
    return (lo >> 16) | (hi & jnp.uint32(0xFFFF0000))


def _unpack_rows(p):
    lo = lax.bitcast_convert_type(p << 16, F32)
    hi = lax.bitcast_convert_type(p & jnp.uint32(0xFFFF0000), F32)
    return lo, hi


def _inproj_body(x_ref, g_ref, w_ref, b_ref, cos_ref, s1_ref, s2_ref,
                 u_ref, q_ref, k_ref, v_ref, gc_ref, ga_ref):
    x = x_ref[...]
    xn = x * lax.rsqrt(jnp.mean(x * x, axis=-1, keepdims=True) + RMS_EPS) * g_ref[...]
    xb = xn.astype(BF16)

    def proj(c0, c1):
        return jnp.dot(xb, w_ref[:, c0:c1], preferred_element_type=F32) + b_ref[:, c0:c1]

    cos = cos_ref[...]
    s1 = s1_ref[...]
    s2 = s2_ref[...]

    def rotary(t):
        half = ROT_DIM // 2
        outs = []
        for c in range(t.shape[1] // LANES):
            tc = t[:, c * LANES:(c + 1) * LANES]
            outs.append(tc * cos + pltpu.roll(tc, half, 1) * s1 + pltpu.roll(tc, LANES - half, 1) * s2)
        return outs[0] if len(outs) == 1 else jnp.concatenate(outs, axis=1)

    d = D_MODEL
    a = proj(0, d)
    gate = proj(d, 2 * d)
    u_ref[...] = (a * jax.nn.sigmoid(gate)).astype(BF16)
    q = proj(2 * d, 3 * d)
    q_ref[...] = (rotary(q) * ATTN_SCALE).astype(BF16)
    kv = proj(3 * d, 3 * d + 2 * KV_COLS)
    k_ref[...] = rotary(kv[:, :KV_COLS]).astype(BF16)
    v_ref[...] = kv[:, KV_COLS:].astype(BF16)
    c0 = 3 * d + 2 * KV_COLS
    gc_ref[...] = jax.nn.sigmoid(proj(c0, c0 + d)).astype(BF16)
    ga_ref[...] = jax.nn.sigmoid(proj(c0 + d, c0 + 2 * d)).astype(BF16)


def _inproj(x, g_mix, w_in, b_in, cos_t, s1_t, s2_t, seq, tm):
    n = x.shape[0]
    in_cols = w_in.shape[1]
    spt = seq // tm
    row = lambda w: pl.BlockSpec((tm, w), lambda i: (i, 0))
    const = lambda r, c: pl.BlockSpec((r, c), lambda i: (0, 0))
    tab = pl.BlockSpec((tm, LANES), lambda i: (i % spt, 0))
    big = jax.ShapeDtypeStruct((n, D_MODEL), BF16)
    small = jax.ShapeDtypeStruct((n, KV_COLS), BF16)
    return pl.pallas_call(
        _inproj_body,
        grid=(n // tm,),
        in_specs=[row(D_MODEL), const(1, D_MODEL), const(D_MODEL, in_cols), const(1, in_cols), tab, tab, tab],
        out_specs=[row(D_MODEL), row(D_MODEL), row(KV_COLS), row(KV_COLS), row(D_MODEL), row(D_MODEL)],
        out_shape=[big, big, small, small, big, big],
        compiler_params=_cparams("arbitrary"),
        name="inproj",
    )(x, g_mix, w_in, b_in, cos_t, s1_t, s2_t)


def _attn_body(q_ref, kp_ref, kc_ref, kn_ref, vp_ref, vc_ref, vn_ref, sink_ref, o_ref, kbuf, vbuf,
               *, tq, nblk):
    i = pl.program_id(1)
    blk = WINDOW
    kbuf[0:blk, :] = kp_ref[0]
    kbuf[blk:blk + tq, :] = kc_ref[0]
    kbuf[blk + tq:, :] = kn_ref[0]
    vbuf[0:blk, :] = vp_ref[0]
    vbuf[blk:blk + tq, :] = vc_ref[0]
    vbuf[blk + tq:, :] = vn_ref[0]

    qi = lax.broadcasted_iota(jnp.int32, (blk, 3 * blk), 0)
    kj = lax.broadcasted_iota(jnp.int32, (blk, 3 * blk), 1)
    band = jnp.abs(qi + blk - kj) <= WINDOW

    def scores(j, h):
        g = h // GROUP
        qh = q_ref[0, j * blk:(j + 1) * blk, h * HEAD_DIM:(h + 1) * HEAD_DIM]
        kh = kbuf[j * blk:(j + 3) * blk, g * HEAD_DIM:(g + 1) * HEAD_DIM]
        return lax.dot_general(qh, kh, (((1,), (1,)), ((), ())), preferred_element_type=F32)

    units = [(j, h) for j in range(tq // blk) for h in range(N_Q_HEADS)]
    pending = [scores(*u) for u in units[:ATTN_LOOKAHEAD]]
    outs = []
    for idx, (j, h) in enumerate(units):
        if idx + ATTN_LOOKAHEAD < len(units):
            pending.append(scores(*units[idx + ATTN_LOOKAHEAD]))
        s = pending.pop(0)
        n = i * (tq // blk) + j
        valid = band & ((kj >= blk) | (n > 0)) & ((kj < 2 * blk) | (n < nblk - 1))
        g = h // GROUP
        vh = vbuf[j * blk:(j + 3) * blk, g * HEAD_DIM:(g + 1) * HEAD_DIM]
        s = jnp.where(valid, s, NEG_BIG)
        sink = sink_ref[h]
        m = jnp.maximum(jnp.max(s, axis=-1, keepdims=True), sink)
        p = jnp.exp(s - m)
        denom = jnp.sum(p, axis=-1, keepdims=True) + jnp.exp(sink - m)
        o = jnp.dot(p.astype(BF16), vh, preferred_element_type=F32)
        outs.append(o / denom)
        if h == N_Q_HEADS - 1:
            o_ref[0, j * blk:(j + 1) * blk, :] = jnp.concatenate(outs, axis=1).astype(BF16)
            outs = []


def _attn(q, k, v, sinks, tq):
    b, s, _ = q.shape
    r = tq // WINDOW
    nblk = s // WINDOW
    cur = lambda w: pl.BlockSpec((1, tq, w), lambda bi, i: (bi, i, 0))
    prev = pl.BlockSpec((1, WINDOW, KV_COLS), lambda bi, i: (bi, jnp.maximum(i * r - 1, 0), 0))
    nxt = pl.BlockSpec((1, WINDOW, KV_COLS), lambda bi, i: (bi, jnp.minimum((i + 1) * r, nblk - 1), 0))
    return pl.pallas_call(
        functools.partial(_attn_body, tq=tq, nblk=nblk),
        grid=(b, s // tq),
        in_specs=[cur(D_MODEL), prev, cur(KV_COLS), nxt, prev, cur(KV_COLS), nxt,
                  pl.BlockSpec(memory_space=pltpu.SMEM)],
        out_specs=cur(D_MODEL),
        out_shape=jax.ShapeDtypeStruct((b, s, D_MODEL), BF16),
        scratch_shapes=[pltpu.VMEM((tq + 2 * WINDOW, KV_COLS), BF16),
                        pltpu.VMEM((tq + 2 * WINDOW, KV_COLS), BF16)],
        compiler_params=_cparams("arbitrary", "arbitrary"),
        name="attn",
    )(q, k, k, k, v, v, v, sinks)


CONV_ROWS = 128


def _mix_body(x_ref, up_ref, uc_ref, un_ref, at_ref, gc_ref, ga_ref,
              wdw_ref, bdw_ref, lng_ref, lnb_ref, wpw_ref, bpw_ref, wout_ref,
              gffn_ref, wrh_ref, wrl_ref, br_ref,
              x2_ref, h_ref, wt_ref, lposp_ref, cnt_ref,
              ubuf, sbuf, ybuf, hh_ref, hl_ref, *, tm, spt):
    i = pl.program_id(0)
    tile = jnp.minimum(i, pl.num_programs(0) - 2)
    first = (tile % spt) == 0
    last = (tile % spt) == spt - 1

    @pl.when(i == 0)
    def _():
        cnt_ref[...] = jnp.zeros_like(cnt_ref)
        hh_ref[...] = jnp.zeros_like(hh_ref)
        hl_ref[...] = jnp.zeros_like(hl_ref)

    for c in range(D_MODEL // LANES):
        cs = slice(c * LANES, (c + 1) * LANES)
        ubuf[c, 0:HALO, :] = jnp.where(first, 0.0, up_ref[:, cs].astype(F32))
        ubuf[c, HALO:HALO + tm, :] = uc_ref[:, cs].astype(F32)
        ubuf[c, HALO + tm:, :] = jnp.where(last, 0.0, un_ref[:, cs].astype(F32))

    def shift_rows(r0, nrows):
        for c in range(D_MODEL // LANES):
            win = ubuf[c, pl.ds(r0, nrows + SUBLANES), :]
            for s in range(1, SUBLANES):
                sbuf[s - 1, c, pl.ds(r0, nrows), :] = win[s:s + nrows]

    def shift_chunk(r, carry):
        shift_rows(pl.multiple_of(r * CONV_ROWS, CONV_ROWS), CONV_ROWS)
        return carry

    lax.fori_loop(0, tm // CONV_ROWS, shift_chunk, 0)
    shift_rows(tm, 2 * HALO - SUBLANES)

    base = HALO - CONV_PAD

    for c in range(D_MODEL // LANES):
        cs = slice(c * LANES, (c + 1) * LANES)
        wts = [jnp.broadcast_to(wdw_ref[t:t + 1, cs], (SUBLANES, LANES)) for t in range(CONV_KERNEL)]
        bias = jnp.broadcast_to(bdw_ref[:, cs], (SUBLANES, LANES))

        def conv_rows(r, carry, c=c, cs=cs, wts=wts, bias=bias):
            r0 = pl.multiple_of(r * CONV_ROWS, CONV_ROWS)
            accs = [bias] * (CONV_ROWS // SUBLANES)
            for t in range(CONV_KERNEL):
                a, s = divmod(base + t, SUBLANES)
                rows = pl.ds(r0 + a * SUBLANES, CONV_ROWS)
                tap = ubuf[c, rows, :] if s == 0 else sbuf[s - 1, c, rows, :]
                accs = [acc + tap[k * SUBLANES:(k + 1) * SUBLANES] * wts[t] for k, acc in enumerate(accs)]
            ybuf[pl.ds(r0, CONV_ROWS), cs] = jnp.concatenate(accs, axis=0)
            return carry

        lax.fori_loop(0, tm // CONV_ROWS, conv_rows, 0)

    p_hi, p_lo = hh_ref[...], hl_ref[...]
    nt = lambda a, b: lax.dot_general(a, b, (((1,), (1,)), ((), ())), preferred_element_type=F32)
    logits = nt(wrh_ref[...], p_hi) + (nt(wrh_ref[...], p_lo) + nt(wrl_ref[...], p_hi)) + br_ref[...]
    eio = lax.broadcasted_iota(jnp.int32, (N_EXPERTS, tm), 0).astype(F32)
    work = logits
    sels, tops = [], []
    for _ in range(TOP_K):
        m = jnp.max(work, axis=0, keepdims=True)
        idx = jnp.min(jnp.where(work == m, eio, float(N_EXPERTS)), axis=0, keepdims=True)
        sel = eio == idx
        work = jnp.where(sel, -jnp.inf, work)
        sels.append(sel)
        tops.append(m)
    es = [jnp.exp(t - tops[0]) for t in tops]
    tot = es[0] + es[1] + es[2] + es[3]
    wt_ref[...] = jnp.concatenate([e / tot for e in es], axis=0)

    y = ybuf[...]
    mu = jnp.mean(y, axis=-1, keepdims=True)
    yc = y - mu
    var = jnp.mean(yc * yc, axis=-1, keepdims=True)
    yn = yc * lax.rsqrt(var + LN_EPS) * lng_ref[...] + lnb_ref[...]
    act = yn * jax.nn.sigmoid(yn)
    conv = jnp.dot(act.astype(BF16), wpw_ref[...], preferred_element_type=F32) + bpw_ref[...]

    merged = gc_ref[...] * conv.astype(BF16) + ga_ref[...] * at_ref[...]
    x2 = x_ref[...] + jnp.dot(merged, wout_ref[...], preferred_element_type=F32)
    x2_ref[...] = x2

    h = x2 * lax.rsqrt(jnp.mean(x2 * x2, axis=-1, keepdims=True) + RMS_EPS) * gffn_ref[...]
    h_hi = h.astype(BF16)
    h_ref[...] = h_hi
    hh_ref[...] = h_hi
    hl_ref[...] = (h - h_hi.astype(F32)).astype(BF16)

    multihot = (sels[0] | sels[1] | sels[2] | sels[3])
    ti = lax.broadcasted_iota(jnp.int32, (tm, tm), 0)
    tj = lax.broadcasted_iota(jnp.int32, (tm, tm), 1)
    before = (ti < tj).astype(BF16)
    prefix = jnp.dot(multihot.astype(BF16), before, preferred_element_type=F32)
    cnt = jnp.sum(multihot.astype(F32), axis=1, keepdims=True)
    cnt_slot = jnp.floor((cnt + (SUBLANES - 1)) * (1.0 / SUBLANES)) * SUBLANES
    ei = lax.broadcasted_iota(jnp.int32, (N_EXPERTS, N_EXPERTS), 0)
    ej = lax.broadcasted_iota(jnp.int32, (N_EXPERTS, N_EXPERTS), 1)
    earlier = (ej < ei).astype(BF16)

    run_start = jnp.dot(earlier, jnp.broadcast_to(cnt_slot, (N_EXPERTS, LANES)).astype(BF16),
                        preferred_element_type=F32)[:, :1]
    pos = prefix + run_start
    lposp_ref[...] = jnp.concatenate(
        [jnp.sum(jnp.where(s, pos, 0.0), axis=0, keepdims=True) for s in sels], axis=0).astype(jnp.int32)
    tile_lane = lax.broadcasted_iota(jnp.int32, cnt_ref.shape, 1)
    cnt_ref[...] = jnp.where(tile_lane == i - 1, cnt, cnt_ref[...])


def _mix(x, u, attn, gc, ga, w_dw, b_dw, ln_g, ln_b, w_pw, b_pw, w_out, g_ffn, w_r_hi, w_r_lo, b_r, seq, tm):
    n = x.shape[0]
    spt = seq // tm
    hb = tm // HALO
    nh = n // HALO
    n_tiles = n // tm
    tile = lambda i: jnp.minimum(i, n_tiles - 1)
    row = lambda w: pl.BlockSpec((tm, w), lambda i: (tile(i), 0))
    const = lambda r, c: pl.BlockSpec((r, c), lambda i: (0, 0))
    halo_prev = pl.BlockSpec((HALO, D_MODEL), lambda i: (jnp.maximum(tile(i) * hb - 1, 0), 0))
    halo_next = pl.BlockSpec((HALO, D_MODEL), lambda i: (jnp.minimum((tile(i) + 1) * hb, nh - 1), 0))
    tok = pl.BlockSpec((TOP_K, tm), lambda i: (0, jnp.maximum(i - 1, 0)))
    return pl.pallas_call(
        functools.partial(_mix_body, tm=tm, spt=spt),
        grid=(n_tiles + 1,),
        in_specs=[row(D_MODEL), halo_prev, row(D_MODEL), halo_next, row(D_MODEL), row(D_MODEL), row(D_MODEL),
                  const(CONV_KERNEL, D_MODEL), const(1, D_MODEL), const(1, D_MODEL), const(1, D_MODEL),
                  const(D_MODEL, D_MODEL), const(1, D_MODEL), const(D_MODEL, D_MODEL),
                  const(1, D_MODEL), const(N_EXPERTS, D_MODEL), const(N_EXPERTS, D_MODEL), const(N_EXPERTS, 1)],
        out_specs=[row(D_MODEL), row(D_MODEL), tok, tok, const(N_EXPERTS, n // tm)],
        out_shape=[jax.ShapeDtypeStruct((n, D_MODEL), F32),
                   jax.ShapeDtypeStruct((n, D_MODEL), BF16),
                   jax.ShapeDtypeStruct((TOP_K, n), F32),
                   jax.ShapeDtypeStruct((TOP_K, n), jnp.int32),
                   jax.ShapeDtypeStruct((N_EXPERTS, n // tm), F32)],
        scratch_shapes=[pltpu.VMEM((D_MODEL // LANES, tm + 2 * HALO, LANES), F32),
                        pltpu.VMEM((SUBLANES - 1, D_MODEL // LANES, tm + 2 * HALO - SUBLANES, LANES), F32),
                        pltpu.VMEM((tm, D_MODEL), F32),
                        pltpu.VMEM((tm, D_MODEL), BF16),
                        pltpu.VMEM((tm, D_MODEL), BF16)],
        compiler_params=_cparams("arbitrary"),
        name="mix",
    )(x, u, u, u, attn, gc, ga, w_dw, b_dw, ln_g, ln_b, w_pw, b_pw, w_out, g_ffn, w_r_hi, w_r_lo, b_r)


RUN_ROWS = SUBLANES
MM_ROWS = 256
TOKEN_TILE = 256


def _tile_slots(tm):
    return -(-(TOP_K * tm + N_EXPERTS * (RUN_ROWS - 1)) // MM_ROWS) * MM_ROWS


PIECE_ROWS = (32, RUN_ROWS)
WAIT_ROWS = (512, 64, RUN_ROWS)


def _rows_copy(src_ref, src_row, dst_ref, dst_row, n_rows, sem):
    rows = lambda r: pl.ds(pl.multiple_of(r, RUN_ROWS), n_rows)
    return pltpu.make_async_copy(src_ref.at[rows(src_row)], dst_ref.at[rows(dst_row)], sem)


def _repeat(n, fn):
    def body(j, c):
        fn(j)
        return c
    lax.fori_loop(0, n, body, 0)


def _max_pieces(tm):
    return _tile_slots(tm) // PIECE_ROWS[0], N_EXPERTS * (PIECE_ROWS[0] // PIECE_ROWS[1] - 1)


def _start_pieces(piece_refs, tile, tm, start):
    for n_rows, cap, (n_ref, loc_ref, glb_ref) in zip(PIECE_ROWS, _max_pieces(tm), piece_refs):
        _repeat(n_ref[tile], lambda p, n_rows=n_rows, cap=cap, loc_ref=loc_ref, glb_ref=glb_ref: start(
            loc_ref[tile * cap + p], glb_ref[tile * cap + p], n_rows))


def _wait_rows(total_rows, wait):
    done = 0
    for n_rows in WAIT_ROWS:
        n = lax.shift_right_logical(total_rows - done, n_rows.bit_length() - 1)
        _repeat(n, lambda j, n_rows=n_rows: wait(n_rows))
        done = done + n * n_rows


def _dispatch_body(*refs, tm, chained, zero_tails):
    pieces, (rows_ref, tail_ref, h_ref, lpos_ref), rest = (refs[0:3], refs[3:6]), refs[6:10], refs[10:]
    xs_ref, obuf, zbuf, sem = rest[1:] if chained else rest
    i = pl.program_id(0)
    slot = i % 2
    buf, other = obuf.at[slot], obuf.at[1 - slot]

    lpos = lpos_ref[...]
    hb = h_ref[...]
    for c in range(obuf.shape[1] // MM_ROWS):
        rio = lax.broadcasted_iota(jnp.int32, (MM_ROWS, tm), 0) + c * MM_ROWS
        onehot = (rio == lpos[0:1]) | (rio == lpos[1:2]) | (rio == lpos[2:3]) | (rio == lpos[3:4])
        srt = jnp.dot(onehot.astype(BF16), hb, preferred_element_type=F32)
        lo = lax.bitcast_convert_type(srt[:, :HALF], jnp.uint32)
        hi = lax.bitcast_convert_type(srt[:, HALF:], jnp.uint32)
        buf[c * MM_ROWS:(c + 1) * MM_ROWS, :] = (lo >> 16) | hi

    _start_pieces(pieces, i, tm, lambda loc, glb, n_rows: _rows_copy(
        buf, loc, xs_ref, glb, n_rows, sem.at[slot]).start())

    @pl.when(i > 0)
    def _():
        _wait_rows(rows_ref[i - 1], lambda n_rows: _rows_copy(other, 0, xs_ref, 0, n_rows, sem.at[1 - slot]).wait())

    @pl.when(i == pl.num_programs(0) - 1)
    def _():
        _wait_rows(rows_ref[i], lambda n_rows: _rows_copy(buf, 0, xs_ref, 0, n_rows, sem.at[slot]).wait())
        if not zero_tails:
            return

        zbuf[...] = jnp.zeros_like(zbuf)

        def per_expert(e, carry):
            n = tail_ref[N_EXPERTS + e]
            _repeat(n, lambda j: _rows_copy(zbuf, 0, xs_ref, tail_ref[e] + j * RUN_ROWS, RUN_ROWS,
                                            sem.at[slot]).start())
            _repeat(n, lambda j: _rows_copy(zbuf, 0, xs_ref, 0, RUN_ROWS, sem.at[slot]).wait())
            return carry

        lax.fori_loop(0, N_EXPERTS, per_expert, 0)


def _dispatch(pieces, tile_rows, tail, h, lpos_t, xs_in, n_rows_out, zero_tails, tm):
    n = h.shape[0]
    chained = xs_in is not None
    grid_spec = pltpu.PrefetchScalarGridSpec(
        num_scalar_prefetch=len(pieces) + 2,
        grid=(n // tm,),
        in_specs=[pl.BlockSpec((tm, D_MODEL), lambda i, *_: (i, 0)),
                  pl.BlockSpec((TOP_K, tm), lambda i, *_: (0, i))]
        + ([pl.BlockSpec(memory_space=pl.ANY)] if chained else []),
        out_specs=pl.BlockSpec(memory_space=pl.ANY),
        scratch_shapes=[pltpu.VMEM((2, _tile_slots(tm), HALF), jnp.uint32),
                        pltpu.VMEM((RUN_ROWS, HALF), jnp.uint32),
                        pltpu.SemaphoreType.DMA((2,))],
    )
    return pl.pallas_call(
        functools.partial(_dispatch_body, tm=tm, chained=chained, zero_tails=zero_tails),
        grid_spec=grid_spec,
        out_shape=jax.ShapeDtypeStruct((n_rows_out, HALF), jnp.uint32),
        input_output_aliases={len(pieces) + 4: 0} if chained else {},
        compiler_params=_cparams("arbitrary"),
        name="dispatch",
    )(*pieces, tile_rows, tail, h, lpos_t, *([xs_in] if chained else []))


CAST_ROWS = 128


def _experts_body(be_ref, nused_ref, xs_ref, wgu_ref, bgu_ref, wd_ref, bd_ref, ys_ref, wgu_bf, wd_bf):
    b = pl.program_id(0)
    used = b < nused_ref[0]

    @pl.when(used & ((b == 0) | (be_ref[b] != be_ref[jnp.maximum(b - 1, 0)])))
    def _():
        def cast(r, carry):
            rows = pl.ds(pl.multiple_of(r * CAST_ROWS, CAST_ROWS), CAST_ROWS)
            wgu_bf[rows, :] = wgu_ref[0, rows, :].astype(BF16)
            wd_bf[rows, :] = wd_ref[0, rows, :].astype(BF16)
            return carry

        lax.fori_loop(0, D_MODEL // CAST_ROWS, cast, 0)

    @pl.when(used)
    def _():
        lo, hi = _unpack_rows(xs_ref[...])
        gu = (jnp.dot(lo.astype(BF16), wgu_bf[:HALF, :], preferred_element_type=F32)
              + jnp.dot(hi.astype(BF16), wgu_bf[HALF:, :], preferred_element_type=F32)
              + bgu_ref[0])
        gate = jnp.minimum(gu[:, :D_FF], SWIGLU_LIMIT)
        up = jnp.clip(gu[:, D_FF:], -SWIGLU_LIMIT, SWIGLU_LIMIT)
        glu = gate * jax.nn.sigmoid(gate * SWIGLU_ALPHA)
        act = ((up + 1.0) * glu).astype(BF16)
        y = jnp.dot(act, wd_bf[...], preferred_element_type=F32) + bd_ref[0]
        ys_ref[...] = _pack_rows(y)

    @pl.when(b >= nused_ref[0])
    def _():
        ys_ref[...] = jnp.zeros_like(ys_ref)


def _experts(block_e, n_used, xs, w_gu, b_gu, w_d, b_d):
    p = xs.shape[0]
    nb = p // MOE_BLOCK

    def xmap(b, be, nu):
        return (jnp.minimum(b, jnp.maximum(nu[0] - 1, 0)), 0)

    def wmap(b, be, nu):
        return (be[b], 0, 0)

    grid_spec = pltpu.PrefetchScalarGridSpec(
        num_scalar_prefetch=2,
        grid=(nb,),
        in_specs=[pl.BlockSpec((MOE_BLOCK, HALF), xmap),
                  pl.BlockSpec((1, D_MODEL, 2 * D_FF), wmap),
                  pl.BlockSpec((1, 1, 2 * D_FF), wmap),
                  pl.BlockSpec((1, D_FF, D_MODEL), wmap),
                  pl.BlockSpec((1, 1, D_MODEL), wmap)],
        out_specs=pl.BlockSpec((MOE_BLOCK, HALF), lambda b, be, nu: (b, 0)),
        scratch_shapes=[pltpu.VMEM((D_MODEL, 2 * D_FF), BF16), pltpu.VMEM((D_FF, D_MODEL), BF16)],
    )
    return pl.pallas_call(
        _experts_body,
        grid_spec=grid_spec,
        out_shape=jax.ShapeDtypeStruct((p, HALF), jnp.uint32),
        compiler_params=_cparams("arbitrary"),
        name="experts",
    )(block_e, n_used, xs, w_gu, b_gu, w_d, b_d)


def _combine_body(*refs, tm, slots):
    pieces = (refs[0:3], refs[3:6])
    rows_ref, x2_ref, lposp_ref, wt_ref, g_ref, ys_ref, o_ref, gbuf, sem = refs[6:]
    i = pl.program_id(0)
    slot = i % 2

    def fetch(tile, s):
        _start_pieces(pieces, tile, tm, lambda loc, glb, n_rows: _rows_copy(
            ys_ref, glb, gbuf.at[s], loc, n_rows, sem.at[s]).start())

    @pl.when(i == 0)
    def _():
        gbuf[...] = jnp.zeros_like(gbuf)
        fetch(i, slot)

    @pl.when(i + 1 < pl.num_programs(0))
    def _():
        fetch(i + 1, 1 - slot)

    _wait_rows(rows_ref[i], lambda n_rows: _rows_copy(ys_ref, 0, gbuf.at[slot], 0, n_rows, sem.at[slot]).wait())

    lposp = lposp_ref[...]
    wt = wt_ref[...]
    sio = lax.broadcasted_iota(jnp.int32, (tm, slots), 1)
    comb = jnp.zeros((tm, slots), F32)
    for k in range(TOP_K):
        comb = jnp.where(sio == lposp[:, k:k + 1], wt[:, k:k + 1], comb)
    comb = comb.astype(BF16)
    lo, hi = _unpack_rows(gbuf[slot])
    moe_lo = jnp.dot(comb, lo.astype(BF16), preferred_element_type=F32)
    moe_hi = jnp.dot(comb, hi.astype(BF16), preferred_element_type=F32)

    x2 = x2_ref[...]
    y_lo = x2[:, :HALF] + moe_lo
    y_hi = x2[:, HALF:] + moe_hi
    ms = (jnp.sum(y_lo * y_lo, axis=-1, keepdims=True) + jnp.sum(y_hi * y_hi, axis=-1, keepdims=True)) / D_MODEL
    inv = lax.rsqrt(ms + RMS_EPS)
    g = g_ref[...]
    o_ref[:, :HALF] = y_lo * inv * g[:, :HALF]
    o_ref[:, HALF:] = y_hi * inv * g[:, HALF:]


def _combine(pieces, tile_rows, x2, lposp_tok, wt_tok, g_final, ys, tm):
    n = x2.shape[0]
    slots = _tile_slots(tm)
    grid_spec = pltpu.PrefetchScalarGridSpec(
        num_scalar_prefetch=len(pieces) + 1,
        grid=(n // tm,),
        in_specs=[pl.BlockSpec((tm, D_MODEL), lambda i, *_: (i, 0)),
                  pl.BlockSpec((tm, TOP_K), lambda i, *_: (i, 0)),
                  pl.BlockSpec((tm, TOP_K), lambda i, *_: (i, 0)),
                  pl.BlockSpec((1, D_MODEL), lambda i, *_: (0, 0)),
                  pl.BlockSpec(memory_space=pl.ANY)],
        out_specs=pl.BlockSpec((tm, D_MODEL), lambda i, *_: (i, 0)),
        scratch_shapes=[pltpu.VMEM((2, slots, HALF), jnp.uint32), pltpu.SemaphoreType.DMA((2,))],
    )
    return pl.pallas_call(
        functools.partial(_combine_body, tm=tm, slots=slots),
        grid_spec=grid_spec,
        out_shape=jax.ShapeDtypeStruct((n, D_MODEL), F32),
        compiler_params=_cparams("arbitrary"),
        name="combine",
    )(*pieces, tile_rows, x2, lposp_tok, wt_tok, g_final, ys)


def _rotary_tables(seq):
    half = ROT_DIM // 2
    inv_freq = ROPE_THETA ** (-jnp.arange(half, dtype=F32) * 2.0 / ROT_DIM)
    ang = jnp.arange(seq, dtype=F32)[:, None] * inv_freq[None, :]
    cos, sin = jnp.cos(ang), jnp.sin(ang)
    ones = jnp.ones((seq, HEAD_DIM - ROT_DIM), F32)
    zeros = jnp.zeros((seq, HEAD_DIM - ROT_DIM), F32)
    zh = jnp.zeros((seq, half), F32)
    cos_h = jnp.concatenate([cos, cos, ones], axis=1)
    s1_h = jnp.concatenate([zh, sin, zeros], axis=1)
    s2_h = jnp.concatenate([-sin, zh, zeros], axis=1)
    rep = LANES // HEAD_DIM
    return jnp.tile(cos_h, (1, rep)), jnp.tile(s1_h, (1, rep)), jnp.tile(s2_h, (1, rep))


def _prep_weights(g_mix, w_in, b_in, sinks, w_dw, b_dw, ln_g, ln_b, w_pw, b_pw, w_out,
                  g_ffn, w_router, b_router, w_gate_up, b_gate_up, w_down, b_down, g_final):
    r1 = lambda a: a.reshape(1, -1).astype(F32)
    w_r_t = w_router[0].T.astype(F32)
    w_r_hi = w_r_t.astype(BF16)
    return dict(
        g_mix=r1(g_mix[0]), w_in=w_in[0].astype(BF16), b_in=r1(b_in[0]), sinks=sinks[0].astype(F32),
        w_dw=w_dw[0].reshape(CONV_KERNEL, D_MODEL).astype(F32), b_dw=r1(b_dw[0]),
        ln_g=r1(ln_g[0]), ln_b=r1(ln_b[0]), w_pw=w_pw[0].astype(BF16), b_pw=r1(b_pw[0]),
        w_out=w_out[0].astype(BF16), g_ffn=r1(g_ffn[0]),
        w_r_hi=w_r_hi, w_r_lo=(w_r_t - w_r_hi.astype(F32)).astype(BF16),
        b_r=b_router[0].reshape(N_EXPERTS, 1).astype(F32),
        w_gu=w_gate_up[0].astype(F32), b_gu=b_gate_up[0].reshape(N_EXPERTS, 1, 2 * D_FF).astype(F32),
        w_d=w_down[0].astype(F32), b_d=b_down[0].reshape(N_EXPERTS, 1, D_MODEL).astype(F32),
        g_final=r1(g_final),
    )


def _mixer_stage(x, w, tables, tm_proj, tq, tm_mix):
    b, s, d = x.shape
    n = b * s
    xf = x.reshape(n, d)
    u, q, k, v, gc, ga = _inproj(xf, w["g_mix"], w["w_in"], w["b_in"], *tables, seq=s, tm=tm_proj)
    attn = _attn(q.reshape(b, s, d), k.reshape(b, s, KV_COLS), v.reshape(b, s, KV_COLS), w["sinks"], tq)
    return _mix(xf, u, attn.reshape(n, d), gc, ga, w["w_dw"], w["b_dw"], w["ln_g"], w["ln_b"], w["w_pw"], w["b_pw"],
                w["w_out"], w["g_ffn"], w["w_r_hi"], w["w_r_lo"], w["b_r"], seq=s, tm=tm_mix)


def _moe_stage(mixed, w, tm):
    tiles = [m[0].shape[0] // tm for m in mixed]
    n_tiles = sum(tiles)

    cnt_te = jnp.concatenate([m[4] for m in mixed], axis=1).T.astype(jnp.int32)
    run_te = (cnt_te + RUN_ROWS - 1) // RUN_ROWS * RUN_ROWS
    total = jnp.sum(run_te, axis=0)
    padded = (total + MOE_BLOCK - 1) // MOE_BLOCK * MOE_BLOCK
    pend = jnp.cumsum(padded)
    pstart = pend - padded
    gstart = pstart[None, :] + jnp.cumsum(run_te, axis=0) - run_te
    lstart = jnp.cumsum(run_te, axis=1) - run_te
    tile_rows = jnp.sum(run_te, axis=1)
    flat = lambda a: a.reshape(-1).astype(jnp.int32)
    max_rows = n_tiles * (tm * TOP_K + N_EXPERTS * (RUN_ROWS - 1)) + N_EXPERTS * (MOE_BLOCK - 1)
    n_blocks = -(-max_rows // MOE_BLOCK)
    block_row = jnp.arange(n_blocks, dtype=jnp.int32) * MOE_BLOCK
    block_e = jnp.minimum(jnp.sum((pend[None, :] <= block_row[:, None]).astype(jnp.int32), axis=1), N_EXPERTS - 1)
    n_used = (pend[-1:] // MOE_BLOCK).astype(jnp.int32)
    tail = flat(jnp.concatenate([pstart + total, (padded - total) // RUN_ROWS]))

    piece_lists = []
    done = jnp.zeros_like(run_te)
    for n_rows, cap in zip(PIECE_ROWS, _max_pieces(tm)):
        per_run = (run_te - done) // n_rows
        last = jnp.cumsum(per_run, axis=1)
        slot_id = jnp.arange(cap, dtype=jnp.int32)[None, :, None]
        owner = jnp.sum((last[:, None, :] <= slot_id).astype(jnp.int32), axis=2, keepdims=True)
        mine = owner == jnp.arange(N_EXPERTS, dtype=jnp.int32)[None, None, :]
        pick = lambda a: jnp.sum(jnp.where(mine, a[:, None, :], 0), axis=2)
        offset = pick(done) + (slot_id[:, :, 0] - pick(last - per_run)) * n_rows
        piece_lists.append((last[:, -1], pick(lstart) + offset, pick(gstart) + offset))
        done = done + per_run * n_rows

    starts = [sum(tiles[:g]) for g in range(len(tiles))]
    sliced = lambda t0, nt: tuple(flat(a[t0:t0 + nt]) for lst in piece_lists for a in lst)
    tables = [(sliced(t0, nt), flat(tile_rows[t0:t0 + nt])) for t0, nt in zip(starts, tiles)]
    xs = None
    for g, (m, (pieces, rows)) in enumerate(zip(mixed, tables)):
        xs = _dispatch(pieces, rows, tail, m[1], m[3], xs, n_blocks * MOE_BLOCK, g == len(mixed) - 1, tm)
    ys = _experts(block_e, n_used, xs, w["w_gu"], w["b_gu"], w["w_d"], w["b_d"])
    return [_combine(pieces, rows, m[0], m[3].T, m[2].T, w["g_final"], ys, tm)
            for m, (pieces, rows) in zip(mixed, tables)]


def kernel(x_prompt, x_sample, g_mix, w_in, b_in, sinks, w_dw, b_dw, ln_g, ln_b, w_pw, b_pw, w_out, g_ffn,
           w_router, b_router, w_gate_up, b_gate_up, w_down, b_down, g_final):
    w = _prep_weights(g_mix, w_in, b_in, sinks, w_dw, b_dw, ln_g, ln_b, w_pw, b_pw, w_out, g_ffn,
                      w_router, b_router, w_gate_up, b_gate_up, w_down, b_down, g_final)
    xs = (x_prompt, x_sample)
    tm = min(TOKEN_TILE, *(x.shape[1] for x in xs))
    mixed = [_mixer_stage(x, w, _rotary_tables(x.shape[1]), tm_proj=min(2 * TOKEN_TILE, x.shape[1]),
                          tq=min(2 * TOKEN_TILE, x.shape[1]), tm_mix=tm) for x in xs]
    outs = _moe_stage(mixed, w, tm)
    return tuple(o.reshape(x.shape) for o, x in zip(outs, xs))
```

```python
import functools

import jax
import jax.numpy as jnp
from jax import lax
from jax.experimental import pallas as pl
from jax.experimental.pallas import tpu as pltpu

D_MODEL = 1024
N_Q_HEADS = 16
N_KV_HEADS = 2
HEAD_DIM = 64
GROUP = N_Q_HEADS // N_KV_HEADS
ROT_DIM = HEAD_DIM // 4
ROPE_THETA = 500000.0
WINDOW = 128
ATTN_SCALE = HEAD_DIM ** -0.5
CONV_KERNEL = 31
CONV_PAD = CONV_KERNEL // 2
N_EXPERTS = 32
TOP_K = 4
D_FF = D_MODEL
SWIGLU_LIMIT = 7.0
SWIGLU_ALPHA = 1.702
MOE_BLOCK = 512
RMS_EPS = 1e-5
LN_EPS = 1e-5

KV_COLS = N_KV_HEADS * HEAD_DIM
LANES = 128
SUBLANES = 8
HALF = D_MODEL // 2
HALO = 16
NEG_BIG = -1e30
ATTN_LOOKAHEAD = 2

VMEM_LIMIT = 56 * 1024 * 1024

F32 = jnp.float32
BF16 = jnp.bfloat16


def _cparams(*sem):
    return pltpu.CompilerParams(dimension_semantics=sem, vmem_limit_bytes=VMEM_LIMIT)


def _pack_rows(x):
    lo = lax.bitcast_convert_type(x[:, :HALF].astype(BF16).astype(F32), jnp.uint32)
    hi = lax.bitcast_convert_type(x[:, HALF:].astype(BF16).astype(F32), jnp.uint32)
    return (lo >> 16) | (hi & jnp.uint32(0xFFFF0000))


def _unpack_rows(p):
    lo = lax.bitcast_convert_type(p << 16, F32)
    hi = lax.bitcast_convert_type(p & jnp.uint32(0xFFFF0000), F32)
    return lo, hi


def _inproj_body(x_ref, g_ref, w_ref, b_ref, cos_ref, s1_ref, s2_ref,
                 u_ref, q_ref, k_ref, v_ref, gc_ref, ga_ref):
    x = x_ref[...]
    xn = x * lax.rsqrt(jnp.mean(x * x, axis=-1, keepdims=True) + RMS_EPS) * g_ref[...]
    xb = xn.astype(BF16)

    def proj(c0, c1):
        return jnp.dot(xb, w_ref[:, c0:c1], preferred_element_type=F32) + b_ref[:, c0:c1]

    cos = cos_ref[...]
    s1 = s1_ref[...]
    s2 = s2_ref[...]

    def rotary(t):
        half = ROT_DIM // 2
        outs = []
        for c in range(t.shape[1] // LANES):
            tc = t[:, c * LANES:(c + 1) * LANES]
            outs.append(tc * cos + pltpu.roll(tc, half, 1) * s1 + pltpu.roll(tc, LANES - half, 1) * s2)
        return outs[0] if len(outs) == 1 else jnp.concatenate(outs, axis=1)

    d = D_MODEL
    a = proj(0, d)
    gate = proj(d, 2 * d)
    u_ref[...] = (a * jax.nn.sigmoid(gate)).astype(BF16)
    q = proj(2 * d, 3 * d)
    q_ref[...] = (rotary(q) * ATTN_SCALE).astype(BF16)
    kv = proj(3 * d, 3 * d + 2 * KV_COLS)
    k_ref[...] = rotary(kv[:, :KV_COLS]).astype(BF16)
    v_ref[...] = kv[:, KV_COLS:].astype(BF16)
    c0 = 3 * d + 2 * KV_COLS
    gc_ref[...] = jax.nn.sigmoid(proj(c0, c0 + d)).astype(BF16)
    ga_ref[...] = jax.nn.sigmoid(proj(c0 + d, c0 + 2 * d)).astype(BF16)


def _inproj(x, g_mix, w_in, b_in, cos_t, s1_t, s2_t, seq, tm):
    n = x.shape[0]
    in_cols = w_in.shape[1]
    spt = seq // tm
    row = lambda w: pl.BlockSpec((tm, w), lambda i: (i, 0))
    const = lambda r, c: pl.BlockSpec((r, c), lambda i: (0, 0))
    tab = pl.BlockSpec((tm, LANES), lambda i: (i % spt, 0))
    big = jax.ShapeDtypeStruct((n, D_MODEL), BF16)
    small = jax.ShapeDtypeStruct((n, KV_COLS), BF16)
    return pl.pallas_call(
        _inproj_body,
        grid=(n // tm,),
        in_specs=[row(D_MODEL), const(1, D_MODEL), const(D_MODEL, in_cols), const(1, in_cols), tab, tab, tab],
        out_specs=[row(D_MODEL), row(D_MODEL), row(KV_COLS), row(KV_COLS), row(D_MODEL), row(D_MODEL)],
        out_shape=[big, big, small, small, big, big],
        compiler_params=_cparams("arbitrary"),
        name="inproj",
    )(x, g_mix, w_in, b_in, cos_t, s1_t, s2_t)


def _attn_body(q_ref, kp_ref, kc_ref, kn_ref, vp_ref, vc_ref, vn_ref, sink_ref, o_ref, kbuf, vbuf,
               *, tq, nblk):
    i = pl.program_id(1)
    blk = WINDOW
    kbuf[0:blk, :] = kp_ref[0]
    kbuf[blk:blk + tq, :] = kc_ref[0]
    kbuf[blk + tq:, :] = kn_ref[0]
    vbuf[0:blk, :] = vp_ref[0]
    vbuf[blk:blk + tq, :] = vc_ref[0]
    vbuf[blk + tq:, :] = vn_ref[0]

    qi = lax.broadcasted_iota(jnp.int32, (blk, 3 * blk), 0)
    kj = lax.broadcasted_iota(jnp.int32, (blk, 3 * blk), 1)
    band = jnp.abs(qi + blk - kj) <= WINDOW

    def scores(j, h):
        g = h // GROUP
        qh = q_ref[0, j * blk:(j + 1) * blk, h * HEAD_DIM:(h + 1) * HEAD_DIM]
        kh = kbuf[j * blk:(j + 3) * blk, g * HEAD_DIM:(g + 1) * HEAD_DIM]
        return lax.dot_general(qh, kh, (((1,), (1,)), ((), ())), preferred_element_type=F32)

    units = [(j, h) for j in range(tq // blk) for h in range(N_Q_HEADS)]
    pending = [scores(*u) for u in units[:ATTN_LOOKAHEAD]]
    outs = []
    for idx, (j, h) in enumerate(units):
        if idx + ATTN_LOOKAHEAD < len(units):
            pending.append(scores(*units[idx + ATTN_LOOKAHEAD]))
        s = pending.pop(0)
        n = i * (tq // blk) + j
        valid = band & ((kj >= blk) | (n > 0)) & ((kj < 2 * blk) | (n < nblk - 1))
        g = h // GROUP
        vh = vbuf[j * blk:(j + 3) * blk, g * HEAD_DIM:(g + 1) * HEAD_DIM]
        s = jnp.where(valid, s, NEG_BIG)
        sink = sink_ref[h]
        m = jnp.maximum(jnp.max(s, axis=-1, keepdims=True), sink)
        p = jnp.exp(s - m)
        denom = jnp.sum(p, axis=-1, keepdims=True) + jnp.exp(sink - m)
        o = jnp.dot(p.astype(BF16), vh, preferred_element_type=F32)
        outs.append(o / denom)
        if h == N_Q_HEADS - 1:
            o_ref[0, j * blk:(j + 1) * blk, :] = jnp.concatenate(outs, axis=1).astype(BF16)
            outs = []


def _attn(q, k, v, sinks, tq):
    b, s, _ = q.shape
    r = tq // WINDOW
    nblk = s // WINDOW
    cur = lambda w: pl.BlockSpec((1, tq, w), lambda bi, i: (bi, i, 0))
    prev = pl.BlockSpec((1, WINDOW, KV_COLS), lambda bi, i: (bi, jnp.maximum(i * r - 1, 0), 0))
    nxt = pl.BlockSpec((1, WINDOW, KV_COLS), lambda bi, i: (bi, jnp.minimum((i + 1) * r, nblk - 1), 0))
    return pl.pallas_call(
        functools.partial(_attn_body, tq=tq, nblk=nblk),
        grid=(b, s // tq),
        in_specs=[cur(D_MODEL), prev, cur(KV_COLS), nxt, prev, cur(KV_COLS), nxt,
                  pl.BlockSpec(memory_space=pltpu.SMEM)],
        out_specs=cur(D_MODEL),
        out_shape=jax.ShapeDtypeStruct((b, s, D_MODEL), BF16),
        scratch_shapes=[pltpu.VMEM((tq + 2 * WINDOW, KV_COLS), BF16),
                        pltpu.VMEM((tq + 2 * WINDOW, KV_COLS), BF16)],
        compiler_params=_cparams("arbitrary", "arbitrary"),
        name="attn",
    )(q, k, k, k, v, v, v, sinks)


CONV_ROWS = 128


def _mix_body(x_ref, up_ref, uc_ref, un_ref, at_ref, gc_ref, ga_ref,
              wdw_ref, bdw_ref, lng_ref, lnb_ref, wpw_ref, bpw_ref, wout_ref,
              gffn_ref, wrh_ref, wrl_ref, br_ref,
              x2_ref, h_ref, wt_ref, lposp_ref, cnt_ref,
              ubuf, sbuf, ybuf, hh_ref, hl_ref, *, tm, spt):
    i = pl.program_id(0)
    tile = jnp.minimum(i, pl.num_programs(0) - 2)
    first = (tile % spt) == 0
    last = (tile % spt) == spt - 1

    @pl.when(i == 0)
    def _():
        cnt_ref[...] = jnp.zeros_like(cnt_ref)
        hh_ref[...] = jnp.zeros_like(hh_ref)
        hl_ref[...] = jnp.zeros_like(hl_ref)

    for c in range(D_MODEL // LANES):
        cs = slice(c * LANES, (c + 1) * LANES)
        ubuf[c, 0:HALO, :] = jnp.where(first, 0.0, up_ref[:, cs].astype(F32))
        ubuf[c, HALO:HALO + tm, :] = uc_ref[:, cs].astype(F32)
        ubuf[c, HALO + tm:, :] = jnp.where(last, 0.0, un_ref[:, cs].astype(F32))

    def shift_rows(r0, nrows):
        for c in range(D_MODEL // LANES):
            win = ubuf[c, pl.ds(r0, nrows + SUBLANES), :]
            for s in range(1, SUBLANES):
                sbuf[s - 1, c, pl.ds(r0, nrows), :] = win[s:s + nrows]

    def shift_chunk(r, carry):
        shift_rows(pl.multiple_of(r * CONV_ROWS, CONV_ROWS), CONV_ROWS)
        return carry

    lax.fori_loop(0, tm // CONV_ROWS, shift_chunk, 0)
    shift_rows(tm, 2 * HALO - SUBLANES)

    base = HALO - CONV_PAD

    for c in range(D_MODEL // LANES):
        cs = slice(c * LANES, (c + 1) * LANES)
        wts = [jnp.broadcast_to(wdw_ref[t:t + 1, cs], (SUBLANES, LANES)) for t in range(CONV_KERNEL)]
        bias = jnp.broadcast_to(bdw_ref[:, cs], (SUBLANES, LANES))

        def conv_rows(r, carry, c=c, cs=cs, wts=wts, bias=bias):
            r0 = pl.multiple_of(r * CONV_ROWS, CONV_ROWS)
            accs = [bias] * (CONV_ROWS // SUBLANES)
            for t in range(CONV_KERNEL):
                a, s = divmod(base + t, SUBLANES)
                rows = pl.ds(r0 + a * SUBLANES, CONV_ROWS)
                tap = ubuf[c, rows, :] if s == 0 else sbuf[s - 1, c, rows, :]
                accs = [acc + tap[k * SUBLANES:(k + 1) * SUBLANES] * wts[t] for k, acc in enumerate(accs)]
            ybuf[pl.ds(r0, CONV_ROWS), cs] = jnp.concatenate(accs, axis=0)
            return carry

        lax.fori_loop(0, tm // CONV_ROWS, conv_rows, 0)

    p_hi, p_lo = hh_ref[...], hl_ref[...]
    nt = lambda a, b: lax.dot_general(a, b, (((1,), (1,)), ((), ())), preferred_element_type=F32)
    logits = nt(wrh_ref[...], p_hi) + (nt(wrh_ref[...], p_lo) + nt(wrl_ref[...], p_hi)) + br_ref[...]
    eio = lax.broadcasted_iota(jnp.int32, (N_EXPERTS, tm), 0).astype(F32)
    work = logits
    sels, tops = [], []
    for _ in range(TOP_K):
        m = jnp.max(work, axis=0, keepdims=True)
        idx = jnp.min(jnp.where(work == m, eio, float(N_EXPERTS)), axis=0, keepdims=True)
        sel = eio == idx
        work = jnp.where(sel, -jnp.inf, work)
        sels.append(sel)
        tops.append(m)
    es = [jnp.exp(t - tops[0]) for t in tops]
    tot = es[0] + es[1] + es[2] + es[3]
    wt_ref[...] = jnp.concatenate([e / tot for e in es], axis=0)

    y = ybuf[...]
    mu = jnp.mean(y, axis=-1, keepdims=True)
    yc = y - mu
    var = jnp.mean(yc * yc, axis=-1, keepdims=True)
    yn = yc * lax.rsqrt(var + LN_EPS) * lng_ref[...] + lnb_ref[...]
    act = yn * jax.nn.sigmoid(yn)
    conv = jnp.dot(act.astype(BF16), wpw_ref[...], preferred_element_type=F32) + bpw_ref[...]

    merged = gc_ref[...] * conv.astype(BF16) + ga_ref[...] * at_ref[...]
    x2 = x_ref[...] + jnp.dot(merged, wout_ref[...], preferred_element_type=F32)
    x2_ref[...] = x2

    h = x2 * lax.rsqrt(jnp.mean(x2 * x2, axis=-1, keepdims=True) + RMS_EPS) * gffn_ref[...]
    h_hi = h.astype(BF16)
    h_ref[...] = h_hi
    hh_ref[...] = h_hi
    hl_ref[...] = (h - h_hi.astype(F32)).astype(BF16)

    multihot = (sels[0] | sels[1] | sels[2] | sels[3])
    ti = lax.broadcasted_iota(jnp.int32, (tm, tm), 0)
    tj = lax.broadcasted_iota(jnp.int32, (tm, tm), 1)
    before = (ti < tj).astype(BF16)
    prefix = jnp.dot(multihot.astype(BF16), before, preferred_element_type=F32)
    cnt = jnp.sum(multihot.astype(F32), axis=1, keepdims=True)
    cnt_slot = jnp.floor((cnt + (SUBLANES - 1)) * (1.0 / SUBLANES)) * SUBLANES
    ei = lax.broadcasted_iota(jnp.int32, (N_EXPERTS, N_EXPERTS), 0)
    ej = lax.broadcasted_iota(jnp.int32, (N_EXPERTS, N_EXPERTS), 1)
    earlier = (ej < ei).astype(BF16)

    run_start = jnp.dot(earlier, jnp.broadcast_to(cnt_slot, (N_EXPERTS, LANES)).astype(BF16),
                        preferred_element_type=F32)[:, :1]
    pos = prefix + run_start
    lposp_ref[...] = jnp.concatenate(
        [jnp.sum(jnp.where(s, pos, 0.0), axis=0, keepdims=True) for s in sels], axis=0).astype(jnp.int32)
    tile_lane = lax.broadcasted_iota(jnp.int32, cnt_ref.shape, 1)
    cnt_ref[...] = jnp.where(tile_lane == i - 1, cnt, cnt_ref[...])


def _mix(x, u, attn, gc, ga, w_dw, b_dw, ln_g, ln_b, w_pw, b_pw, w_out, g_ffn, w_r_hi, w_r_lo, b_r, seq, tm):
    n = x.shape[0]
    spt = seq // tm
    hb = tm // HALO
    nh = n // HALO
    n_tiles = n // tm
    tile = lambda i: jnp.minimum(i, n_tiles - 1)
    row = lambda w: pl.BlockSpec((tm, w), lambda i: (tile(i), 0))
    const = lambda r, c: pl.BlockSpec((r, c), lambda i: (0, 0))
    halo_prev = pl.BlockSpec((HALO, D_MODEL), lambda i: (jnp.maximum(tile(i) * hb - 1, 0), 0))
    halo_next = pl.BlockSpec((HALO, D_MODEL), lambda i: (jnp.minimum((tile(i) + 1) * hb, nh - 1), 0))
    tok = pl.BlockSpec((TOP_K, tm), lambda i: (0, jnp.maximum(i - 1, 0)))
    return pl.pallas_call(
        functools.partial(_mix_body, tm=tm, spt=spt),
        grid=(n_tiles + 1,),
        in_specs=[row(D_MODEL), halo_prev, row(D_MODEL), halo_next, row(D_MODEL), row(D_MODEL), row(D_MODEL),
                  const(CONV_KERNEL, D_MODEL), const(1, D_MODEL), const(1, D_MODEL), const(1, D_MODEL),
                  const(D_MODEL, D_MODEL), const(1, D_MODEL), const(D_MODEL, D_MODEL),
                  const(1, D_MODEL), const(N_EXPERTS, D_MODEL), const(N_EXPERTS, D_MODEL), const(N_EXPERTS, 1)],
        out_specs=[row(D_MODEL), row(D_MODEL), tok, tok, const(N_EXPERTS, n // tm)],
        out_shape=[jax.ShapeDtypeStruct((n, D_MODEL), F32),
                   jax.ShapeDtypeStruct((n, D_MODEL), BF16),
                   jax.ShapeDtypeStruct((TOP_K, n), F32),
                   jax.ShapeDtypeStruct((TOP_K, n), jnp.int32),
                   jax.ShapeDtypeStruct((N_EXPERTS, n // tm), F32)],
        scratch_shapes=[pltpu.VMEM((D_MODEL // LANES, tm + 2 * HALO, LANES), F32),
                        pltpu.VMEM((SUBLANES - 1, D_MODEL // LANES, tm + 2 * HALO - SUBLANES, LANES), F32),
                        pltpu.VMEM((tm, D_MODEL), F32),
                        pltpu.VMEM((tm, D_MODEL), BF16),
                        pltpu.VMEM((tm, D_MODEL), BF16)],
        compiler_params=_cparams("arbitrary"),
        name="mix",
    )(x, u, u, u, attn, gc, ga, w_dw, b_dw, ln_g, ln_b, w_pw, b_pw, w_out, g_ffn, w_r_hi, w_r_lo, b_r)


RUN_ROWS = SUBLANES
MM_ROWS = 256
TOKEN_TILE = 256


def _tile_slots(tm):
    return -(-(TOP_K * tm + N_EXPERTS * (RUN_ROWS - 1)) // MM_ROWS) * MM_ROWS


PIECE_ROWS = (32, RUN_ROWS)
WAIT_ROWS = (512, 64, RUN_ROWS)


def _rows_copy(src_ref, src_row, dst_ref, dst_row, n_rows, sem):
    rows = lambda r: pl.ds(pl.multiple_of(r, RUN_ROWS), n_rows)
    return pltpu.make_async_copy(src_ref.at[rows(src_row)], dst_ref.at[rows(dst_row)], sem)


def _repeat(n, fn):
    def body(j, c):
        fn(j)
        return c
    lax.fori_loop(0, n, body, 0)


def _max_pieces(tm):
    return _tile_slots(tm) // PIECE_ROWS[0], N_EXPERTS * (PIECE_ROWS[0] // PIECE_ROWS[1] - 1)


def _start_pieces(piece_refs, tile, tm, start):
    for n_rows, cap, (n_ref, loc_ref, glb_ref) in zip(PIECE_ROWS, _max_pieces(tm), piece_refs):
        _repeat(n_ref[tile], lambda p, n_rows=n_rows, cap=cap, loc_ref=loc_ref, glb_ref=glb_ref: start(
            loc_ref[tile * cap + p], glb_ref[tile * cap + p], n_rows))


def _wait_rows(total_rows, wait):
    done = 0
    for n_rows in WAIT_ROWS:
        n = lax.shift_right_logical(total_rows - done, n_rows.bit_length() - 1)
        _repeat(n, lambda j, n_rows=n_rows: wait(n_rows))
        done = done + n * n_rows


def _dispatch_body(*refs, tm, chained, zero_tails):
    pieces, (rows_ref, tail_ref, h_ref, lpos_ref), rest = (refs[0:3], refs[3:6]), refs[6:10], refs[10:]
    xs_ref, obuf, zbuf, sem = rest[1:] if chained else rest
    i = pl.program_id(0)
    slot = i % 2
    buf, other = obuf.at[slot], obuf.at[1 - slot]

    lpos = lpos_ref[...]
    hb = h_ref[...]
    for c in range(obuf.shape[1] // MM_ROWS):
        rio = lax.broadcasted_iota(jnp.int32, (MM_ROWS, tm), 0) + c * MM_ROWS
        onehot = (rio == lpos[0:1]) | (rio == lpos[1:2]) | (rio == lpos[2:3]) | (rio == lpos[3:4])
        srt = jnp.dot(onehot.astype(BF16), hb, preferred_element_type=F32)
        lo = lax.bitcast_convert_type(srt[:, :HALF], jnp.uint32)
        hi = lax.bitcast_convert_type(srt[:, HALF:], jnp.uint32)
        buf[c * MM_ROWS:(c + 1) * MM_ROWS, :] = (lo >> 16) | hi

    _start_pieces(pieces, i, tm, lambda loc, glb, n_rows: _rows_copy(
        buf, loc, xs_ref, glb, n_rows, sem.at[slot]).start())

    @pl.when(i > 0)
    def _():
        _wait_rows(rows_ref[i - 1], lambda n_rows: _rows_copy(other, 0, xs_ref, 0, n_rows, sem.at[1 - slot]).wait())

    @pl.when(i == pl.num_programs(0) - 1)
    def _():
        _wait_rows(rows_ref[i], lambda n_rows: _rows_copy(buf, 0, xs_ref, 0, n_rows, sem.at[slot]).wait())
        if not zero_tails:
            return

        zbuf[...] = jnp.zeros_like(zbuf)

        def per_expert(e, carry):
            n = tail_ref[N_EXPERTS + e]
            _repeat(n, lambda j: _rows_copy(zbuf, 0, xs_ref, tail_ref[e] + j * RUN_ROWS, RUN_ROWS,
                                            sem.at[slot]).start())
            _repeat(n, lambda j: _rows_copy(zbuf, 0, xs_ref, 0, RUN_ROWS, sem.at[slot]).wait())
            return carry

        lax.fori_loop(0, N_EXPERTS, per_expert, 0)


def _dispatch(pieces, tile_rows, tail, h, lpos_t, xs_in, n_rows_out, zero_tails, tm):
    n = h.shape[0]
    chained = xs_in is not None
    grid_spec = pltpu.PrefetchScalarGridSpec(
        num_scalar_prefetch=len(pieces) + 2,
        grid=(n // tm,),
        in_specs=[pl.BlockSpec((tm, D_MODEL), lambda i, *_: (i, 0)),
                  pl.BlockSpec((TOP_K, tm), lambda i, *_: (0, i))]
        + ([pl.BlockSpec(memory_space=pl.ANY)] if chained else []),
        out_specs=pl.BlockSpec(memory_space=pl.ANY),
        scratch_shapes=[pltpu.VMEM((2, _tile_slots(tm), HALF), jnp.uint32),
                        pltpu.VMEM((RUN_ROWS, HALF), jnp.uint32),
                        pltpu.SemaphoreType.DMA((2,))],
    )
    return pl.pallas_call(
        functools.partial(_dispatch_body, tm=tm, chained=chained, zero_tails=zero_tails),
        grid_spec=grid_spec,
        out_shape=jax.ShapeDtypeStruct((n_rows_out, HALF), jnp.uint32),
        input_output_aliases={len(pieces) + 4: 0} if chained else {},
        compiler_params=_cparams("arbitrary"),
        name="dispatch",
    )(*pieces, tile_rows, tail, h, lpos_t, *([xs_in] if chained else []))


CAST_ROWS = 128
FF_CHUNK = 512


def _experts_body(be_ref, nused_ref, xs_ref, wgu_ref, bgu_ref, wd_ref, bd_ref, ys_ref, wgu_bf, wd_bf):
    b = pl.program_id(0)
    used = b < nused_ref[0]

    @pl.when(used & ((b == 0) | (be_ref[b] != be_ref[jnp.maximum(b - 1, 0)])))
    def _():
        def cast(r, carry):
            rows = pl.ds(pl.multiple_of(r * CAST_ROWS, CAST_ROWS), CAST_ROWS)
            wgu_bf[rows, :] = wgu_ref[0, rows, :].astype(BF16)
            wd_bf[rows, :] = wd_ref[0, rows, :].astype(BF16)
            return carry

        lax.fori_loop(0, D_MODEL // CAST_ROWS, cast, 0)

    @pl.when(used)
    def _():
        lo, hi = _unpack_rows(xs_ref[...])
        lo, hi = lo.astype(BF16), hi.astype(BF16)

        def proj(c0):
            cols = slice(c0, c0 + FF_CHUNK)
            return (jnp.dot(lo, wgu_bf[:HALF, cols], preferred_element_type=F32)
                    + jnp.dot(hi, wgu_bf[HALF:, cols], preferred_element_type=F32) + bgu_ref[0, :, cols])

        chunks = range(D_FF // FF_CHUNK)
        gus = [(proj(0), proj(D_FF))]
        parts = []
        for c in chunks:
            if c + 1 < len(chunks):
                gus.append((proj((c + 1) * FF_CHUNK), proj(D_FF + (c + 1) * FF_CHUNK)))
            gate, up = gus[c]
            gate = jnp.minimum(gate, SWIGLU_LIMIT)
            up = jnp.clip(up, -SWIGLU_LIMIT, SWIGLU_LIMIT)
            glu = gate * jax.nn.sigmoid(gate * SWIGLU_ALPHA)
            parts.append(((up + 1.0) * glu).astype(BF16))
        y = jnp.dot(jnp.concatenate(parts, axis=1), wd_bf[...], preferred_element_type=F32) + bd_ref[0]
        ys_ref[...] = _pack_rows(y)

    @pl.when(b >= nused_ref[0])
    def _():
        ys_ref[...] = jnp.zeros_like(ys_ref)


def _experts(block_e, n_used, xs, w_gu, b_gu, w_d, b_d):
    p = xs.shape[0]
    nb = p // MOE_BLOCK

    def xmap(b, be, nu):
        return (jnp.minimum(b, jnp.maximum(nu[0] - 1, 0)), 0)

    def wmap(b, be, nu):
        return (be[b], 0, 0)

    grid_spec = pltpu.PrefetchScalarGridSpec(
        num_scalar_prefetch=2,
        grid=(nb,),
        in_specs=[pl.BlockSpec((MOE_BLOCK, HALF), xmap),
                  pl.BlockSpec((1, D_MODEL, 2 * D_FF), wmap),
                  pl.BlockSpec((1, 1, 2 * D_FF), wmap),
                  pl.BlockSpec((1, D_FF, D_MODEL), wmap),
                  pl.BlockSpec((1, 1, D_MODEL), wmap)],
        out_specs=pl.BlockSpec((MOE_BLOCK, HALF), lambda b, be, nu: (b, 0)),
        scratch_shapes=[pltpu.VMEM((D_MODEL, 2 * D_FF), BF16), pltpu.VMEM((D_FF, D_MODEL), BF16)],
    )
    return pl.pallas_call(
        _experts_body,
        grid_spec=grid_spec,
        out_shape=jax.ShapeDtypeStruct((p, HALF), jnp.uint32),
        compiler_params=_cparams("arbitrary"),
        name="experts",
    )(block_e, n_used, xs, w_gu, b_gu, w_d, b_d)


def _combine_body(*refs, tm, slots):
    pieces = (refs[0:3], refs[3:6])
    rows_ref, x2_ref, lposp_ref, wt_ref, g_ref, ys_ref, o_ref, gbuf, sem = refs[6:]
    i = pl.program_id(0)
    slot = i % 2

    def fetch(tile, s):
        _start_pieces(pieces, tile, tm, lambda loc, glb, n_rows: _rows_copy(
            ys_ref, glb, gbuf.at[s], loc, n_rows, sem.at[s]).start())

    @pl.when(i == 0)
    def _():
        gbuf[...] = jnp.zeros_like(gbuf)
        fetch(i, slot)

    @pl.when(i + 1 < pl.num_programs(0))
    def _():
        fetch(i + 1, 1 - slot)

    _wait_rows(rows_ref[i], lambda n_rows: _rows_copy(ys_ref, 0, gbuf.at[slot], 0, n_rows, sem.at[slot]).wait())

    lposp = lposp_ref[...]
    wt = wt_ref[...]
    sio = lax.broadcasted_iota(jnp.int32, (tm, slots), 1)
    comb = jnp.zeros((tm, slots), F32)
    for k in range(TOP_K):
        comb = jnp.where(sio == lposp[:, k:k + 1], wt[:, k:k + 1], comb)
    comb = comb.astype(BF16)
    lo, hi = _unpack_rows(gbuf[slot])
    moe_lo = jnp.dot(comb, lo.astype(BF16), preferred_element_type=F32)
    moe_hi = jnp.dot(comb, hi.astype(BF16), preferred_element_type=F32)

    x2 = x2_ref[...]
    y_lo = x2[:, :HALF] + moe_lo
    y_hi = x2[:, HALF:] + moe_hi
    ms = (jnp.sum(y_lo * y_lo, axis=-1, keepdims=True) + jnp.sum(y_hi * y_hi, axis=-1, keepdims=True)) / D_MODEL
    inv = lax.rsqrt(ms + RMS_EPS)
    g = g_ref[...]
    o_ref[:, :HALF] = y_lo * inv * g[:, :HALF]
    o_ref[:, HALF:] = y_hi * inv * g[:, HALF:]


def _combine(pieces, tile_rows, x2, lposp_tok, wt_tok, g_final, ys, tm):
    n = x2.shape[0]
    slots = _tile_slots(tm)
    grid_spec = pltpu.PrefetchScalarGridSpec(
        num_scalar_prefetch=len(pieces) + 1,
        grid=(n // tm,),
        in_specs=[pl.BlockSpec((tm, D_MODEL), lambda i, *_: (i, 0)),
                  pl.BlockSpec((tm, TOP_K), lambda i, *_: (i, 0)),
                  pl.BlockSpec((tm, TOP_K), lambda i, *_: (i, 0)),
                  pl.BlockSpec((1, D_MODEL), lambda i, *_: (0, 0)),
                  pl.BlockSpec(memory_space=pl.ANY)],
        out_specs=pl.BlockSpec((tm, D_MODEL), lambda i, *_: (i, 0)),
        scratch_shapes=[pltpu.VMEM((2, slots, HALF), jnp.uint32), pltpu.SemaphoreType.DMA((2,))],
    )
    return pl.pallas_call(
        functools.partial(_combine_body, tm=tm, slots=slots),
        grid_spec=grid_spec,
        out_shape=jax.ShapeDtypeStruct((n, D_MODEL), F32),
        compiler_params=_cparams("arbitrary"),
        name="combine",
    )(*pieces, tile_rows, x2, lposp_tok, wt_tok, g_final, ys)


def _rotary_tables(seq):
    half = ROT_DIM // 2
    inv_freq = ROPE_THETA ** (-jnp.arange(half, dtype=F32) * 2.0 / ROT_DIM)
    ang = jnp.arange(seq, dtype=F32)[:, None] * inv_freq[None, :]
    cos, sin = jnp.cos(ang), jnp.sin(ang)
    ones = jnp.ones((seq, HEAD_DIM - ROT_DIM), F32)
    zeros = jnp.zeros((seq, HEAD_DIM - ROT_DIM), F32)
    zh = jnp.zeros((seq, half), F32)
    cos_h = jnp.concatenate([cos, cos, ones], axis=1)
    s1_h = jnp.concatenate([zh, sin, zeros], axis=1)
    s2_h = jnp.concatenate([-sin, zh, zeros], axis=1)
    rep = LANES // HEAD_DIM
    return jnp.tile(cos_h, (1, rep)), jnp.tile(s1_h, (1, rep)), jnp.tile(s2_h, (1, rep))


def _prep_weights(g_mix, w_in, b_in, sinks, w_dw, b_dw, ln_g, ln_b, w_pw, b_pw, w_out,
                  g_ffn, w_router, b_router, w_gate_up, b_gate_up, w_down, b_down, g_final):
    r1 = lambda a: a.reshape(1, -1).astype(F32)
    w_r_t = w_router[0].T.astype(F32)
    w_r_hi = w_r_t.astype(BF16)
    return dict(
        g_mix=r1(g_mix[0]), w_in=w_in[0].astype(BF16), b_in=r1(b_in[0]), sinks=sinks[0].astype(F32),
        w_dw=w_dw[0].reshape(CONV_KERNEL, D_MODEL).astype(F32), b_dw=r1(b_dw[0]),
        ln_g=r1(ln_g[0]), ln_b=r1(ln_b[0]), w_pw=w_pw[0].astype(BF16), b_pw=r1(b_pw[0]),
        w_out=w_out[0].astype(BF16), g_ffn=r1(g_ffn[0]),
        w_r_hi=w_r_hi, w_r_lo=(w_r_t - w_r_hi.astype(F32)).astype(BF16),
        b_r=b_router[0].reshape(N_EXPERTS, 1).astype(F32),
        w_gu=w_gate_up[0].astype(F32), b_gu=b_gate_up[0].reshape(N_EXPERTS, 1, 2 * D_FF).astype(F32),
        w_d=w_down[0].astype(F32), b_d=b_down[0].reshape(N_EXPERTS, 1, D_MODEL).astype(F32),
        g_final=r1(g_final),
    )


def _mixer_stage(x, w, tables, tm_proj, tq, tm_mix):
    b, s, d = x.shape
    n = b * s
    xf = x.reshape(n, d)
    u, q, k, v, gc, ga = _inproj(xf, w["g_mix"], w["w_in"], w["b_in"], *tables, seq=s, tm=tm_proj)
    attn = _attn(q.reshape(b, s, d), k.reshape(b, s, KV_COLS), v.reshape(b, s, KV_COLS), w["sinks"], tq)
    return _mix(xf, u, attn.reshape(n, d), gc, ga, w["w_dw"], w["b_dw"], w["ln_g"], w["ln_b"], w["w_pw"], w["b_pw"],
                w["w_out"], w["g_ffn"], w["w_r_hi"], w["w_r_lo"], w["b_r"], seq=s, tm=tm_mix)


def _moe_stage(mixed, w, tm):
    tiles = [m[0].shape[0] // tm for m in mixed]
    n_tiles = sum(tiles)

    cnt_te = jnp.concatenate([m[4] for m in mixed], axis=1).T.astype(jnp.int32)
    run_te = (cnt_te + RUN_ROWS - 1) // RUN_ROWS * RUN_ROWS
    total = jnp.sum(run_te, axis=0)
    padded = (total + MOE_BLOCK - 1) // MOE_BLOCK * MOE_BLOCK
    pend = jnp.cumsum(padded)
    pstart = pend - padded
    gstart = pstart[None, :] + jnp.cumsum(run_te, axis=0) - run_te
    lstart = jnp.cumsum(run_te, axis=1) - run_te
    tile_rows = jnp.sum(run_te, axis=1)
    flat = lambda a: a.reshape(-1).astype(jnp.int32)
    max_rows = n_tiles * (tm * TOP_K + N_EXPERTS * (RUN_ROWS - 1)) + N_EXPERTS * (MOE_BLOCK - 1)
    n_blocks = -(-max_rows // MOE_BLOCK)
    block_row = jnp.arange(n_blocks, dtype=jnp.int32) * MOE_BLOCK
    block_e = jnp.minimum(jnp.sum((pend[None, :] <= block_row[:, None]).astype(jnp.int32), axis=1), N_EXPERTS - 1)
    n_used = (pend[-1:] // MOE_BLOCK).astype(jnp.int32)
    tail = flat(jnp.concatenate([pstart + total, (padded - total) // RUN_ROWS]))

    piece_lists = []
    done = jnp.zeros_like(run_te)
    for n_rows, cap in zip(PIECE_ROWS, _max_pieces(tm)):
        per_run = (run_te - done) // n_rows
        last = jnp.cumsum(per_run, axis=1)
        slot_id = jnp.arange(cap, dtype=jnp.int32)[None, :, None]
        owner = jnp.sum((last[:, None, :] <= slot_id).astype(jnp.int32), axis=2, keepdims=True)
        mine = owner == jnp.arange(N_EXPERTS, dtype=jnp.int32)[None, None, :]
        pick = lambda a: jnp.sum(jnp.where(mine, a[:, None, :], 0), axis=2)
        offset = pick(done) + (slot_id[:, :, 0] - pick(last - per_run)) * n_rows
        piece_lists.append((last[:, -1], pick(lstart) + offset, pick(gstart) + offset))
        done = done + per_run * n_rows

    starts = [sum(tiles[:g]) for g in range(len(tiles))]
    sliced = lambda t0, nt: tuple(flat(a[t0:t0 + nt]) for lst in piece_lists for a in lst)
    tables = [(sliced(t0, nt), flat(tile_rows[t0:t0 + nt])) for t0, nt in zip(starts, tiles)]
    xs = None
    for g, (m, (pieces, rows)) in enumerate(zip(mixed, tables)):
        xs = _dispatch(pieces, rows, tail, m[1], m[3], xs, n_blocks * MOE_BLOCK, g == len(mixed) - 1, tm)
    ys = _experts(block_e, n_used, xs, w["w_gu"], w["b_gu"], w["w_d"], w["b_d"])
    return [_combine(pieces, rows, m[0], m[3].T, m[2].T, w["g_final"], ys, tm)
            for m, (pieces, rows) in zip(mixed, tables)]


def kernel(x_prompt, x_sample, g_mix, w_in, b_in, sinks, w_dw, b_dw, ln_g, ln_b, w_pw, b_pw, w_out, g_ffn,
           w_router, b_router, w_gate_up, b_gate_up, w_down, b_down, g_final):
    w = _prep_weights(g_mix, w_in, b_in, sinks, w_dw, b_dw, ln_g, ln_b, w_pw, b_pw, w_out, g_ffn,
                      w_router, b_router, w_gate_up, b_gate_up, w_down, b_down, g_final)
    xs = (x_prompt, x_sample)
    tm = min(TOKEN_TILE, *(x.shape[1] for x in xs))
    mixed = [_mixer_stage(x, w, _rotary_tables(x.shape[1]), tm_proj=min(2 * TOKEN_TILE, x.shape[1]),
                          tq=min(2 * TOKEN_TILE, x.shape[1]), tm_mix=tm) for x in xs]
    outs = _moe_stage(mixed, w, tm)
    return tuple(o.reshape(x.shape) for o, x in zip(outs, xs))
```
